```python
import math
import jax, jax.numpy as jnp
from jax import lax
import numpy as np

D_MODEL = 2048
BATCH = 1
SEQ = 16384
DEPTH = 4

N_MEM = 256
NSA_HEADS = 16
NSA_GROUPS = 4
NSA_HPG = NSA_HEADS // NSA_GROUPS
HEAD_DIM = 64
NSA_WIDTH = NSA_HEADS * HEAD_DIM
KV_WIDTH = NSA_GROUPS * HEAD_DIM
CMP_LEN = 32
CMP_STRIDE = 16
CMP_HID = 2 * HEAD_DIM
SEL_BLK = 64
SEL_TOPN = 16
WIN = 512
Q_BLK = 128
FORCE_SCORE = 1e4
POOL_WINDOWS = (2, 4, 8, 16)
POOL_GROUPS = 4
POOL_WIDTH = D_MODEL // 2
POOL_GW = POOL_WIDTH // POOL_GROUPS
REL_BUCKETS = 32
REL_MAX_DIST = 2048
X_HEADS = 4
X_HEAD_DIM = 128
X_WIDTH = X_HEADS * X_HEAD_DIM
D_FF = -(-8 * D_MODEL // (3 * 256)) * 256
IN_WIDTH = NSA_WIDTH + 6 * KV_WIDTH + 3 * NSA_HEADS + POOL_WIDTH + 2 * D_MODEL

kernel_name = "nsa_pool_hybrid_block"


def rms_norm(x, g, eps=1e-6):
    xf = x.astype(jnp.float32)
    y = xf * lax.rsqrt(jnp.mean(xf * xf, axis=-1, keepdims=True) + eps)
    return (y * g.astype(jnp.float32)).astype(x.dtype)


def rel_bucket(dist):
    n = jnp.maximum(dist, 0)
    exact = REL_BUCKETS // 2
    nf = jnp.maximum(n, 1).astype(jnp.float32)
    large = exact + (jnp.log(nf / exact) / math.log(REL_MAX_DIST / exact)
                     * (REL_BUCKETS - exact)).astype(jnp.int32)
    return jnp.where(n < exact, n, jnp.minimum(large, REL_BUCKETS - 1))


def masked_softmax(s, mask):
    s = jnp.where(mask, s.astype(jnp.float32), -1e30)
    m = jnp.max(s, axis=-1, keepdims=True)
    e = jnp.where(mask, jnp.exp(s - m), 0.0)
    return e / jnp.maximum(jnp.sum(e, axis=-1, keepdims=True), 1.0)


def compress_kv(raw, pe, w1, w2):
    b, s = raw.shape[0], raw.shape[1]
    n_cmp = (s - CMP_LEN) // CMP_STRIDE + 1
    idx = jnp.arange(n_cmp)[:, None] * CMP_STRIDE + jnp.arange(CMP_LEN)[None, :]
    blk = raw[:, idx] + pe[:, None, :]
    flat = jnp.transpose(blk, (0, 1, 3, 2, 4)).reshape(b, n_cmp, NSA_GROUPS, CMP_LEN * HEAD_DIM)
    return jax.nn.gelu(flat @ w1) @ w2


def cmp_to_sel_overlap(n_cmp, n_sel):
    i = jnp.arange(n_cmp)[:, None]
    j = jnp.arange(n_sel)[None, :]
    lo = jnp.maximum(i * CMP_STRIDE, j * SEL_BLK)
    hi = jnp.minimum(i * CMP_STRIDE + CMP_LEN, (j + 1) * SEL_BLK)
    return jnp.maximum(hi - lo, 0).astype(jnp.float32) / CMP_LEN


def nsa_attention(q, k_cmp_raw, v_cmp_raw, k_sel, v_sel, k_win, v_win, gate_logits,
                  cmp_pe, cmp_w1, cmp_w2, rel_bias):
    b, s = q.shape[0], q.shape[1]
    scale = HEAD_DIM ** -0.5
    kc = compress_kv(k_cmp_raw, cmp_pe[0], cmp_w1[0], cmp_w2[0])
    vc = compress_kv(v_cmp_raw, cmp_pe[1], cmp_w1[1], cmp_w2[1])
    n_cmp = kc.shape[1]
    n_sel = s // SEL_BLK
    topn = min(SEL_TOPN, n_sel)
    overlap = cmp_to_sel_overlap(n_cmp, n_sel)
    cmp_end = jnp.arange(n_cmp) * CMP_STRIDE + CMP_LEN - 1
    kb = k_sel.reshape(b, n_sel, SEL_BLK, NSA_GROUPS, HEAD_DIM).transpose(0, 3, 1, 2, 4)
    vb = v_sel.reshape(b, n_sel, SEL_BLK, NSA_GROUPS, HEAD_DIM).transpose(0, 3, 1, 2, 4)
    kw_pad = jnp.pad(k_win, ((0, 0), (WIN, 0), (0, 0), (0, 0)))
    vw_pad = jnp.pad(v_win, ((0, 0), (WIN, 0), (0, 0), (0, 0)))
    gates = jax.nn.sigmoid(gate_logits.astype(jnp.float32)).astype(q.dtype).reshape(b, s, NSA_HEADS, 3)
    tab_g = rel_bias.reshape(NSA_GROUPS, NSA_HPG, REL_BUCKETS)
    gather_blocks = jax.vmap(jax.vmap(lambda blocks, idx: blocks[idx]))
    group_bias = jax.vmap(jax.vmap(lambda tab, bk: tab[:, bk]), in_axes=(None, 0))

    def head_bias(dist):
        return tab_g[:, :, rel_bucket(dist)]

    def block(qi):
        q0 = qi * Q_BLK
        t = q0 + jnp.arange(Q_BLK)
        qg = lax.dynamic_slice_in_dim(q, q0, Q_BLK, 1).reshape(b, Q_BLK, NSA_GROUPS, NSA_HPG, HEAD_DIM)
        gq = lax.dynamic_slice_in_dim(gates, q0, Q_BLK, 1)
        d_c = t[:, None] - cmp_end[None, :]
        s_c = jnp.einsum('bqghd,bngd->bghqn', qg, kc).astype(jnp.float32) * scale + head_bias(d_c)
        p_c = masked_softmax(s_c, d_c >= 0)
        o_c = jnp.einsum('bghqn,bngd->bqghd', p_c.astype(vc.dtype), vc)
        imp = jnp.einsum('bghqn,nj->bgqj', p_c, overlap)
        cur = t // SEL_BLK
        j = jnp.arange(n_sel)[None, :]
        forced = (j == 0) | (j == cur[:, None]) | (j == cur[:, None] - 1)
        imp = jnp.where(forced, FORCE_SCORE, imp)
        imp = jnp.where(j > cur[:, None], -1.0, imp)
        top_val, top_idx = lax.top_k(imp, topn)
        ks = gather_blocks(kb, top_idx)
        vs = gather_blocks(vb, top_idx)
        pos = top_idx[..., None] * SEL_BLK + jnp.arange(SEL_BLK)
        d_s = t[:, None, None] - pos
        mask_s = (top_val >= 0)[..., None] & (d_s >= 0)
        bias_s = group_bias(tab_g, rel_bucket(d_s))
        s_s = jnp.einsum('bqghd,bgqnkd->bghqnk', qg, ks).astype(jnp.float32) * scale + bias_s
        m_len = topn * SEL_BLK
        p_s = masked_softmax(s_s.reshape(b, NSA_GROUPS, NSA_HPG, Q_BLK, m_len),
                             mask_s.reshape(b, NSA_GROUPS, 1, Q_BLK, m_len))
        o_s = jnp.einsum('bghqm,bgqmd->bqghd', p_s.astype(vs.dtype),
                         vs.reshape(b, NSA_GROUPS, Q_BLK, m_len, HEAD_DIM))
        kw = lax.dynamic_slice_in_dim(kw_pad, q0, WIN + Q_BLK, 1)
        vw = lax.dynamic_slice_in_dim(vw_pad, q0, WIN + Q_BLK, 1)
        s_pos = q0 - WIN + jnp.arange(WIN + Q_BLK)
        d_w = t[:, None] - s_pos[None, :]
        mask_w = (d_w >= 0) & (d_w < WIN) & (s_pos[None, :] >= 0)
        s_w = jnp.einsum('bqghd,bkgd->bghqk', qg, kw).astype(jnp.float32) * scale + head_bias(d_w)
        p_w = masked_softmax(s_w, mask_w)
        o_w = jnp.einsum('bghqk,bkgd->bqghd', p_w.astype(vw.dtype), vw)
        o = jnp.stack([o_c, o_s, o_w], axis=-1).reshape(b, Q_BLK, NSA_HEADS, HEAD_DIM, 3)
        return jnp.einsum('bqhdc,bqhc->bqhd', o, gq)

    out = lax.map(block, jnp.arange(s // Q_BLK))
    return jnp.transpose(out, (1, 0, 2, 3, 4)).reshape(b, s, NSA_WIDTH)


def pool_mixer(u, w_pool, pool_scale):
    b, s, _ = u.shape
    uf = u.astype(jnp.float32)
    cnt_base = jnp.arange(s) + 1
    outs = []
    for gi, w in enumerate(POOL_WINDOWS):
        ug = uf[..., gi * POOL_GW:(gi + 1) * POOL_GW]
        c = jnp.concatenate([jnp.zeros((b, 1, POOL_GW), jnp.float32), jnp.cumsum(ug, axis=1)], axis=1)
        hi = c[:, 1:]
        lo = jnp.pad(c[:, :s + 1 - w], ((0, 0), (w - 1, 0), (0, 0)))
        cnt = jnp.minimum(cnt_base, w).astype(jnp.float32)[None, :, None]
        outs.append((hi - lo) / cnt - ug)
    y = jnp.stack(outs, axis=2).astype(u.dtype)
    y = jnp.einsum('bsgc,gcd->bsgd', y, w_pool)
    return y.reshape(b, s, POOL_WIDTH) * pool_scale


def memory_cross_attention(h, mem_n, wq, wkv, wo):
    b, s, _ = h.shape
    q = (h @ wq).reshape(b, s, X_HEADS, X_HEAD_DIM)
    k, v = jnp.split(mem_n @ wkv, 2, axis=-1)
    k = k.reshape(b, -1, X_HEADS, X_HEAD_DIM)
    v = v.reshape(b, -1, X_HEADS, X_HEAD_DIM)
    sc = jnp.einsum('bshd,bmhd->bhsm', q, k).astype(jnp.float32) * (X_HEAD_DIM ** -0.5)
    p = jax.nn.softmax(sc, axis=-1).astype(v.dtype)
    o = jnp.einsum('bhsm,bmhd->bshd', p, v).reshape(b, s, X_WIDTH)
    return o @ wo


def setup_inputs(seed: int = 0) -> dict:
    key = jax.random.key(seed)
    ks = jax.random.split(key, 32)
    f32 = jnp.float32

    def w(k, shape, fan_in):
        return jax.random.normal(k, shape, f32) * (fan_in ** -0.5)

    def gain(k, shape):
        return 1.0 + 0.05 * jax.random.normal(k, shape, f32)

    L = DEPTH
    return {
        "x": jax.random.normal(ks[0], (BATCH, SEQ, D_MODEL), f32),
        "mem": jax.random.normal(ks[1], (BATCH, N_MEM, D_MODEL), f32),
        "rel_bias": 0.5 * jax.random.normal(ks[2], (NSA_HEADS, REL_BUCKETS), f32),
        "ln_mix_pre": gain(ks[3], (L, D_MODEL)),
        "ln_mix_post": gain(ks[4], (L, D_MODEL)),
        "ln_x_pre": gain(ks[5], (L, D_MODEL)),
        "ln_x_post": gain(ks[6], (L, D_MODEL)),
        "ln_mem": gain(ks[7], (L, D_MODEL)),
        "ln_ffn_pre": gain(ks[8], (L, D_MODEL)),
        "ln_ffn_post": gain(ks[9], (L, D_MODEL)),
        "w_in": w(ks[10], (L, D_MODEL, IN_WIDTH), D_MODEL),
        "cmp_pe": 0.1 * jax.random.normal(ks[11], (L, 2, CMP_LEN, HEAD_DIM), f32),
        "cmp_w1": w(ks[12], (L, 2, CMP_LEN * HEAD_DIM, CMP_HID), CMP_LEN * HEAD_DIM),
        "cmp_w2": w(ks[13], (L, 2, CMP_HID, HEAD_DIM), CMP_HID),
        "w_pool": w(ks[14], (L, POOL_GROUPS, POOL_GW, POOL_GW), POOL_GW),
        "pool_scale": gain(ks[15], (L, POOL_WIDTH)),
        "w_br_attn": w(ks[16], (L, NSA_WIDTH, D_MODEL), NSA_WIDTH),
        "w_br_pool": w(ks[17], (L, POOL_WIDTH, D_MODEL), POOL_WIDTH),
        "w_mix_out": w(ks[18], (L, D_MODEL, D_MODEL), D_MODEL),
        "w_xq": w(ks[19], (L, D_MODEL, X_WIDTH), D_MODEL),
        "w_xkv": w(ks[20], (L, D_MODEL, 2 * X_WIDTH), D_MODEL),
        "w_xo": w(ks[21], (L, X_WIDTH, D_MODEL), X_WIDTH),
        "w_gate": w(ks[22], (L, D_MODEL, D_FF), D_MODEL),
        "w_up": w(ks[23], (L, D_MODEL, D_FF), D_MODEL),
        "w_down": w(ks[24], (L, D_FF, D_MODEL), D_FF),
    }


def reference(x, mem, rel_bias, ln_mix_pre, ln_mix_post, ln_x_pre, ln_x_post, ln_mem,
              ln_ffn_pre, ln_ffn_post, w_in, cmp_pe, cmp_w1, cmp_w2, w_pool, pool_scale,
              w_br_attn, w_br_pool, w_mix_out, w_xq, w_xkv, w_xo, w_gate, w_up, w_down):
    b, s, _ = x.shape
    splits = [NSA_WIDTH,
              NSA_WIDTH + 6 * KV_WIDTH,
              NSA_WIDTH + 6 * KV_WIDTH + 3 * NSA_HEADS,
              NSA_WIDTH + 6 * KV_WIDTH + 3 * NSA_HEADS + POOL_WIDTH]
    for l in range(DEPTH):
        h = rms_norm(x, ln_mix_pre[l])
        z = h @ w_in[l]
        q, kv, nsa_gl, u, merge_logits = jnp.split(z, splits, axis=-1)
        q = q.reshape(b, s, NSA_HEADS, HEAD_DIM)
        kv = kv.reshape(b, s, 6, NSA_GROUPS, HEAD_DIM)
        a = nsa_attention(q, kv[:, :, 0], kv[:, :, 1], kv[:, :, 2], kv[:, :, 3],
                          kv[:, :, 4], kv[:, :, 5], nsa_gl,
                          cmp_pe[l], cmp_w1[l], cmp_w2[l], rel_bias)
        p = pool_mixer(u, w_pool[l], pool_scale[l])
        g_a, g_p = jnp.split(jax.nn.sigmoid(merge_logits.astype(jnp.float32)).astype(x.dtype), 2, axis=-1)
        y = (g_a * (a @ w_br_attn[l]) + g_p * (p @ w_br_pool[l])) @ w_mix_out[l]
        x = x + rms_norm(y, ln_mix_post[l])
        h = rms_norm(x, ln_x_pre[l])
        mem_n = rms_norm(mem, ln_mem[l])
        y = memory_cross_attention(h, mem_n, w_xq[l], w_xkv[l], w_xo[l])
        x = x + rms_norm(y, ln_x_post[l])
        h = rms_norm(x, ln_ffn_pre[l])
        y = (jax.nn.silu(h @ w_gate[l]) * (h @ w_up[l])) @ w_down[l]
        x = x + rms_norm(y, ln_ffn_post[l])
    return x
```

```python
import functools
import math

import numpy as np
import jax
import jax.numpy as jnp
from jax import lax
from jax.experimental import pallas as pl
from jax.experimental.pallas import tpu as pltpu

F32 = jnp.float32
BF16 = jnp.bfloat16

D_MODEL = 2048
DEPTH = 4
N_MEM = 256
NSA_HEADS = 16
NSA_GROUPS = 4
NSA_HPG = 4
HEAD_DIM = 64
NSA_WIDTH = 1024
KV_WIDTH = 256
CMP_LEN = 32
CMP_STRIDE = 16
CMP_HID = 128
SEL_BLK = 64
SEL_TOPN = 16
WIN = 512
Q_BLK = 128
FORCE_SCORE = 1e4
POOL_WINDOWS = (2, 4, 8, 16)
POOL_GW = 256
POOL_WIDTH = 1024
REL_BUCKETS = 32
REL_MAX_DIST = 2048
X_HEADS = 4
X_HEAD_DIM = 128
X_WIDTH = 512
D_FF = 5632
NEG = -1e30

GL_PAD = 256
OFF_ML, OFF_U, OFF_Q, OFF_KV, OFF_GL = 0, 4096, 5120, 6144, 7680
IN_PAD = 7936

VMEM_LIMIT_V7X = 56 * 1024 * 1024
LANES = 128


def _cparams(sem):
    return pltpu.CompilerParams(dimension_semantics=sem, vmem_limit_bytes=VMEM_LIMIT_V7X)


def _rms(x, g):
    ms = jnp.mean(x * x, axis=-1, keepdims=True)
    return x * lax.rsqrt(ms + 1e-6) * g


def _sigmoid(x):
    return 1.0 / (1.0 + jnp.exp(-x))


def _dot(a, b):
    return jnp.dot(a, b, preferred_element_type=F32)


def _dot_nt(a, b):
    return lax.dot_general(a, b, (((1,), (1,)), ((), ())), preferred_element_type=F32)


def _norm_matmul_kernel(x_ref, g_ref, w_ref, o_ref, h_ref):
    @pl.when(pl.program_id(1) == 0)
    def _():
        h_ref[...] = _rms(x_ref[...], g_ref[...]).astype(BF16)

    o_ref[...] = _dot(h_ref[...], w_ref[...]).astype(o_ref.dtype)


def norm_matmul(x, g, w, *, tm, tn, out_dtype=F32):
    m, k = x.shape
    n = w.shape[1]
    return pl.pallas_call(
        _norm_matmul_kernel,
        grid=(m // tm, n // tn),
        in_specs=[
            pl.BlockSpec((tm, k), lambda i, j: (i, 0)),
            pl.BlockSpec((1, k), lambda i, j: (0, 0)),
            pl.BlockSpec((k, tn), lambda i, j: (0, j)),
        ],
        out_specs=pl.BlockSpec((tm, tn), lambda i, j: (i, j)),
        out_shape=jax.ShapeDtypeStruct((m, n), out_dtype),
        scratch_shapes=[pltpu.VMEM((tm, k), BF16)],
        compiler_params=_cparams(("parallel", "arbitrary")),
        name="norm_matmul",
    )(x, g.reshape(1, k), w)


def _bucket(d):
    n = jnp.maximum(d, 0)
    exact = REL_BUCKETS // 2
    nf = jnp.maximum(n, 1).astype(F32)
    large = exact + (jnp.log(nf / exact) / math.log(REL_MAX_DIST / exact)
                     * (REL_BUCKETS - exact)).astype(jnp.int32)
    return jnp.where(n < exact, n, jnp.minimum(large, REL_BUCKETS - 1))


def _lookup(tab_ref, h, bk):
    out = jnp.full(bk.shape, tab_ref[h, 0], F32)
    for b in range(1, REL_BUCKETS):
        out = jnp.where(bk == b, tab_ref[h, b], out)
    return out


def _bias_tables_kernel(tab_ref, ts_ref, bc_ref, *, nch, off, ncw, cmp_shift):
    h = pl.program_id(0)
    qi = lax.broadcasted_iota(jnp.int32, (Q_BLK, LANES), 0)
    cc = lax.broadcasted_iota(jnp.int32, (Q_BLK, LANES), 1)

    def strip_chunk(ch, carry):
        d = qi - (ch * LANES + cc) + off
        ts_ref[0, ch] = _lookup(tab_ref, h, _bucket(d))
        return carry

    lax.fori_loop(0, nch, strip_chunk, 0)

    def cmp_chunk(ch, carry):
        d = qi - CMP_STRIDE * (ch * LANES + cc) + cmp_shift
        val = _lookup(tab_ref, h, _bucket(d))
        bc_ref[0, :, pl.ds(pl.multiple_of(ch * LANES, LANES), LANES)] = jnp.where(d >= 0, val, NEG)
        return carry

    lax.fori_loop(0, ncw // LANES, cmp_chunk, 0)


def bias_tables(rel_bias, *, s, tk):
    off = REL_MAX_DIST + tk
    ls = REL_MAX_DIST + 3 * tk - Q_BLK
    ls = max(ls, off + WIN + Q_BLK)
    nch = ls // LANES
    ncw = s // CMP_STRIDE
    cmp_shift = CMP_STRIDE * (ncw - 8) - (CMP_LEN - 1)
    kern = functools.partial(_bias_tables_kernel, nch=nch, off=off, ncw=ncw, cmp_shift=cmp_shift)
    return pl.pallas_call(
        kern,
        grid=(NSA_HEADS,),
        in_specs=[pl.BlockSpec(memory_space=pltpu.SMEM)],
        out_specs=[
            pl.BlockSpec((1, nch, Q_BLK, LANES), lambda h: (h, 0, 0, 0)),
            pl.BlockSpec((1, Q_BLK, ncw), lambda h: (h, 0, 0)),
        ],
        out_shape=[
            jax.ShapeDtypeStruct((NSA_HEADS, nch, Q_BLK, LANES), F32),
            jax.ShapeDtypeStruct((NSA_HEADS, Q_BLK, ncw), F32),
        ],
        compiler_params=_cparams(("arbitrary",)),
        name="bias_tables",
    )(rel_bias)


def _gelu_tanh(x):
    return 0.5 * x * (1.0 + jnp.tanh(math.sqrt(2.0 / math.pi) * (x + 0.044715 * x * x * x)))


def _compress_kernel(c_ref, pe_ref, w1_ref, w2_ref, o_ref, scr_ref, *, nc):
    c = c_ref[0, 0]
    ca = (c + pe_ref[0, 0]).astype(BF16)
    cb = (c + pe_ref[0, 1]).astype(BF16)
    h1 = _dot(ca, w1_ref[0, 0])
    h2 = _dot(cb, w1_ref[0, 1])
    scr_ref[pl.ds(0, nc), :] = h2
    scr_ref[pl.ds(nc, 8), :] = jnp.zeros((8, CMP_HID), F32)
    hid = h1 + scr_ref[pl.ds(1, nc), :]
    out = _dot(_gelu_tanh(hid).astype(BF16), w2_ref[0])
    row = lax.broadcasted_iota(jnp.int32, out.shape, 0)
    o_ref[0, 0] = jnp.where(row < nc - 1, out, 0.0)


def compress(c, pe, w1, w2, *, s):
    nc = s // CMP_STRIDE
    half = CMP_STRIDE * HEAD_DIM
    return pl.pallas_call(
        functools.partial(_compress_kernel, nc=nc),
        grid=(2, NSA_GROUPS),
        in_specs=[
            pl.BlockSpec((1, 1, nc, half), lambda a, g: (a, g, 0, 0)),
            pl.BlockSpec((1, 2, 1, half), lambda a, g: (a, 0, 0, 0)),
            pl.BlockSpec((1, 2, half, CMP_HID), lambda a, g: (a, 0, 0, 0)),
            pl.BlockSpec((1, CMP_HID, HEAD_DIM), lambda a, g: (a, 0, 0)),
        ],
        out_specs=pl.BlockSpec((1, 1, nc, HEAD_DIM), lambda a, g: (a, g, 0, 0)),
        out_shape=jax.ShapeDtypeStruct((2, NSA_GROUPS, nc, HEAD_DIM), F32),
        scratch_shapes=[pltpu.VMEM((nc + 8, CMP_HID), F32)],
        compiler_params=_cparams(("arbitrary", "arbitrary")),
        name="compress_kv",
    )(c, pe, w1, w2)


def _softmax_rows(s3):
    m = jnp.max(s3, axis=-1, keepdims=True)
    e = jnp.exp(s3 - m)
    l = jnp.sum(e, axis=-1, keepdims=True)
    return jnp.where(m > 0.5 * NEG, e / l, 0.0)


def _nsa_kernel(q_ref, kvc_ref, kvs_ref, kvw_ref, gl_ref, ts_ref, bc_ref, ov_ref, o_ref,
                *, tk, ncw, nsel, off):
    i = pl.program_id(1)
    q0 = i * Q_BLK
    rows = NSA_HPG * Q_BLK
    q = q_ref[...].reshape(rows, HEAD_DIM)
    qa = jnp.concatenate([q, jnp.zeros_like(q)], axis=-1)

    def strip_bias(cb, nchunks):
        per_head = []
        for hh in range(NSA_HPG):
            per_head.append(jnp.concatenate(
                [ts_ref[hh, cb + c] for c in range(nchunks)], axis=-1))
        return jnp.stack(per_head, axis=0)

    kvc = kvc_ref[0, pl.ds(pl.multiple_of(8 * i, 8), ncw), :].astype(BF16)
    s3 = _dot_nt(qa, kvc).reshape(NSA_HPG, Q_BLK, ncw) + bc_ref[...]
    m_iota = lax.broadcasted_iota(jnp.int32, (1, 1, ncw), 2)
    s3 = jnp.where(m_iota >= (ncw - 8) - 8 * i, s3, NEG)
    p3 = _softmax_rows(s3)
    o_c = _dot(p3.reshape(rows, ncw).astype(BF16), kvc)

    psum = jnp.sum(p3, axis=0)
    p_hi = psum.astype(BF16)
    p_lo = (psum - p_hi.astype(F32)).astype(BF16)
    imp = _dot(p_hi, ov_ref[...]) + _dot(p_lo, ov_ref[...])

    jp = lax.broadcasted_iota(jnp.int32, (Q_BLK, nsel), 1)
    jabs = jp + (2 * i - (nsel - 2))
    qrow = lax.broadcasted_iota(jnp.int32, (Q_BLK, nsel), 0)
    cur = 2 * i + (qrow >= SEL_BLK).astype(jnp.int32)
    forced = (jabs == 0) | (jabs == cur) | (jabs == cur - 1)
    val = jnp.where(forced, FORCE_SCORE, imp)
    val = jnp.where(jabs > cur, -1.0, val)
    val = jnp.where(jabs < 0, -3.0, val)
    work = val
    chosen = jnp.zeros((Q_BLK, nsel), F32)
    jpf = jp.astype(F32)
    for _ in range(SEL_TOPN):
        mx = jnp.max(work, axis=-1, keepdims=True)
        jmin = jnp.min(jnp.where(work == mx, jpf, 2.0 * nsel), axis=-1, keepdims=True)
        hit = jpf == jmin
        chosen = jnp.where(hit, 1.0, chosen)
        work = jnp.where(hit, -1e38, work)
    sel = jnp.where(val >= 0.0, chosen, 0.0).astype(BF16)

    t_col = q0 + lax.broadcasted_iota(jnp.int32, (Q_BLK, tk), 0)
    k_lane = lax.broadcasted_iota(jnp.int32, (Q_BLK, tk), 1)
    e_row = lax.broadcasted_iota(jnp.int32, (nsel, tk), 0) + (2 * i - (nsel - 2))
    e_col = lax.broadcasted_iota(jnp.int32, (nsel, tk), 1)

    def sel_tile(kt, carry):
        m, l, acc = carry
        k0 = kt * tk
        kv = kvs_ref[0, pl.ds(pl.multiple_of(k0, tk), tk), :]
        s = _dot_nt(qa, kv).reshape(NSA_HPG, Q_BLK, tk)
        cb = jnp.maximum(off - (q0 - k0), 0) // LANES
        s = s + strip_bias(cb, tk // LANES)
        expand = jnp.where(e_row == ((k0 + e_col) >> 6), 1.0, 0.0).astype(BF16)
        selm = _dot(sel, expand)
        valid = (selm > 0.5) & ((k0 + k_lane) <= t_col)
        s = jnp.where(valid[None], s, NEG)
        m_new = jnp.maximum(m, jnp.max(s, axis=-1, keepdims=True))
        alpha = jnp.exp(m - m_new)
        p = jnp.exp(s - m_new)
        l = alpha * l + jnp.sum(p, axis=-1, keepdims=True)
        acc = alpha.reshape(rows, 1) * acc + _dot(p.reshape(rows, tk).astype(BF16), kv)
        return m_new, l, acc

    n_tiles = (q0 + Q_BLK + tk - 1) // tk
    init = (jnp.full((NSA_HPG, Q_BLK, 1), NEG, F32),
            jnp.zeros((NSA_HPG, Q_BLK, 1), F32),
            jnp.zeros((rows, LANES), F32))
    _, l_s, acc_s = lax.fori_loop(0, n_tiles, sel_tile, init)
    o_s = acc_s / l_s.reshape(rows, 1)

    wk = WIN + Q_BLK
    start = jnp.maximum(q0 - WIN, 0)
    kvw = kvw_ref[0, pl.ds(pl.multiple_of(start, Q_BLK), wk), :]
    s = _dot_nt(qa, kvw).reshape(NSA_HPG, Q_BLK, wk)
    s = s + strip_bias((off - (q0 - start)) // LANES, wk // LANES)
    d_w = (q0 + lax.broadcasted_iota(jnp.int32, (Q_BLK, wk), 0)) - (
        start + lax.broadcasted_iota(jnp.int32, (Q_BLK, wk), 1))
    s = jnp.where(((d_w >= 0) & (d_w < WIN))[None], s, NEG)
    p_w = _softmax_rows(s)
    o_w = _dot(p_w.reshape(rows, wk).astype(BF16), kvw)

    gates = _sigmoid(gl_ref[0])
    def gate_col(c):
        return jnp.concatenate([gates[:, 3 * hh + c:3 * hh + c + 1] for hh in range(NSA_HPG)], axis=0)
    o = gate_col(0) * o_c + gate_col(1) * o_s + gate_col(2) * o_w
    o_ref[...] = jnp.concatenate(
        [o[hh * Q_BLK:(hh + 1) * Q_BLK, HEAD_DIM:] for hh in range(NSA_HPG)], axis=-1).astype(o_ref.dtype)


def _overlap_matrix(ncw, nsel):
    m = np.arange(ncw)[:, None]
    j = np.arange(nsel)[None, :]
    lo = np.maximum(m * CMP_STRIDE, j * SEL_BLK)
    hi = np.minimum(m * CMP_STRIDE + CMP_LEN, (j + 1) * SEL_BLK)
    return (np.maximum(hi - lo, 0).astype(np.float32) / CMP_LEN)


def nsa_attention(qh, kvc_pad, kvs, kvw, glt, ts, bc, *, s, tk):
    ncw = s // CMP_STRIDE
    nsel = s // SEL_BLK
    off = REL_MAX_DIST + tk
    nch = ts.shape[1]
    ov = jnp.asarray(_overlap_matrix(ncw, nsel), BF16)
    kern = functools.partial(_nsa_kernel, tk=tk, ncw=ncw, nsel=nsel, off=off)
    return pl.pallas_call(
        kern,
        grid=(NSA_GROUPS, s // Q_BLK),
        in_specs=[
            pl.BlockSpec((NSA_HPG, Q_BLK, HEAD_DIM), lambda g, i: (g, i, 0)),
            pl.BlockSpec((1, 2 * ncw, LANES), lambda g, i: (g, 0, 0)),
            pl.BlockSpec((1, s, LANES), lambda g, i: (g, 0, 0)),
            pl.BlockSpec((1, s, LANES), lambda g, i: (g, 0, 0)),
            pl.BlockSpec((1, Q_BLK, 3 * NSA_HPG), lambda g, i: (g, i, 0)),
            pl.BlockSpec((NSA_HPG, nch, Q_BLK, LANES), lambda g, i: (g, 0, 0, 0)),
            pl.BlockSpec((NSA_HPG, Q_BLK, ncw), lambda g, i: (g, 0, 0)),
            pl.BlockSpec((ncw, nsel), lambda g, i: (0, 0)),
        ],
        out_specs=pl.BlockSpec((Q_BLK, NSA_HPG * HEAD_DIM), lambda g, i: (i, g)),
        out_shape=jax.ShapeDtypeStruct((s, NSA_WIDTH), BF16),
        compiler_params=_cparams(("arbitrary", "arbitrary")),
        name="nsa_attention",
    )(qh, kvc_pad, kvs, kvw, glt, ts, bc, ov)


HALO = 16


def _pool_kernel(u_ref, halo_ref, w_ref, sc_ref, o_ref, ext_ref, *, tm):
    i = pl.program_id(0)
    halo = jnp.where(i > 0, halo_ref[...], 0.0)
    ext_ref[pl.ds(0, HALO), :] = halo
    ext_ref[pl.ds(HALO, tm), :] = u_ref[...]
    t = i * tm + lax.broadcasted_iota(jnp.int32, (tm, 1), 0)
    outs = []
    for gi, w in enumerate(POOL_WINDOWS):
        cols = pl.ds(gi * POOL_GW, POOL_GW)
        x = ext_ref[pl.ds(HALO, tm), cols]
        acc = x
        for k in range(1, w):
            acc = acc + ext_ref[pl.ds(HALO - k, tm), cols]
        cnt = jnp.minimum(t + 1, w).astype(F32)
        y = acc / cnt - x
        outs.append(_dot(y.astype(BF16), w_ref[gi]))
    o_ref[...] = (jnp.concatenate(outs, axis=-1) * sc_ref[...]).astype(o_ref.dtype)


def pool_mixer(z, w_pool, pool_scale, *, tm):
    s = z.shape[0]
    ucol = OFF_U // POOL_WIDTH
    return pl.pallas_call(
        functools.partial(_pool_kernel, tm=tm),
        grid=(s // tm,),
        in_specs=[
            pl.BlockSpec((tm, POOL_WIDTH), lambda i: (i, ucol)),
            pl.BlockSpec((HALO, POOL_WIDTH), lambda i: (jnp.maximum(i * (tm // HALO) - 1, 0), ucol)),
            pl.BlockSpec((len(POOL_WINDOWS), POOL_GW, POOL_GW), lambda i: (0, 0, 0)),
            pl.BlockSpec((1, POOL_WIDTH), lambda i: (0, 0)),
        ],
        out_specs=pl.BlockSpec((tm, POOL_WIDTH), lambda i: (i, 0)),
        out_shape=jax.ShapeDtypeStruct((s, POOL_WIDTH), BF16),
        scratch_shapes=[pltpu.VMEM((HALO + tm, POOL_WIDTH), F32)],
        compiler_params=_cparams(("parallel",)),
        name="pool_mixer",
    )(z, z, w_pool, pool_scale.reshape(1, POOL_WIDTH))


def _merge_kernel(a_ref, p_ref, ml_ref, x_ref, wa_ref, wp_ref, wo_ref, g_ref, o_ref):
    pa = _dot(a_ref[...], wa_ref[...])
    pp = _dot(p_ref[...], wp_ref[...])
    ml = ml_ref[...]
    y = _sigmoid(ml[:, :D_MODEL]) * pa + _sigmoid(ml[:, D_MODEL:]) * pp
    y = _dot(y.astype(BF16), wo_ref[...])
    o_ref[...] = x_ref[...] + _rms(y, g_ref[...])


def merge_mix(a, p, z, x, wa, wp, wo, g, *, tm):
    s = x.shape[0]
    const = lambda i: (0, 0)
    return pl.pallas_call(
        _merge_kernel,
        grid=(s // tm,),
        in_specs=[
            pl.BlockSpec((tm, NSA_WIDTH), lambda i: (i, 0)),
            pl.BlockSpec((tm, POOL_WIDTH), lambda i: (i, 0)),
            pl.BlockSpec((tm, 2 * D_MODEL), lambda i: (i, OFF_ML // (2 * D_MODEL))),
            pl.BlockSpec((tm, D_MODEL), lambda i: (i, 0)),
            pl.BlockSpec((NSA_WIDTH, D_MODEL), const, pipeline_mode=pl.Buffered(1)),
            pl.BlockSpec((POOL_WIDTH, D_MODEL), const, pipeline_mode=pl.Buffered(1)),
            pl.BlockSpec((D_MODEL, D_MODEL), const, pipeline_mode=pl.Buffered(1)),
            pl.BlockSpec((1, D_MODEL), const),
        ],
        out_specs=pl.BlockSpec((tm, D_MODEL), lambda i: (i, 0)),
        out_shape=jax.ShapeDtypeStruct((s, D_MODEL), F32),
        compiler_params=_cparams(("parallel",)),
        name="merge_mix",
    )(a, p, z, x, wa, wp, wo, g.reshape(1, D_MODEL))


def _xattn_kernel(x_ref, gpre_ref, wq_ref, kv_ref, wo_ref, gpost_ref, o_ref):
    x = x_ref[...]
    h = _rms(x, gpre_ref[...]).astype(BF16)
    q = _dot(h, wq_ref[...]).astype(BF16)
    kv = kv_ref[...]
    outs = []
    for hh in range(X_HEADS):
        lo = hh * X_HEAD_DIM
        sc = _dot_nt(q[:, lo:lo + X_HEAD_DIM], kv[:, lo:lo + X_HEAD_DIM]) * (X_HEAD_DIM ** -0.5)
        m = jnp.max(sc, axis=-1, keepdims=True)
        e = jnp.exp(sc - m)
        p = (e / jnp.sum(e, axis=-1, keepdims=True)).astype(BF16)
        outs.append(_dot(p, kv[:, X_WIDTH + lo:X_WIDTH + lo + X_HEAD_DIM]))
    o = jnp.concatenate(outs, axis=-1).astype(BF16)
    y = _dot(o, wo_ref[...])
    o_ref[...] = x + _rms(y, gpost_ref[...])


def cross_attention(x, gpre, wq, memkv, wo, gpost, *, tm):
    s = x.shape[0]
    const = lambda i: (0, 0)
    return pl.pallas_call(
        _xattn_kernel,
        grid=(s // tm,),
        in_specs=[
            pl.BlockSpec((tm, D_MODEL), lambda i: (i, 0)),
            pl.BlockSpec((1, D_MODEL), const),
            pl.BlockSpec((D_MODEL, X_WIDTH), const),
            pl.BlockSpec((N_MEM, 2 * X_WIDTH), const),
            pl.BlockSpec((X_WIDTH, D_MODEL), const),
            pl.BlockSpec((1, D_MODEL), const),
        ],
        out_specs=pl.BlockSpec((tm, D_MODEL), lambda i: (i, 0)),
        out_shape=jax.ShapeDtypeStruct((s, D_MODEL), F32),
        compiler_params=_cparams(("parallel",)),
        name="cross_attention",
    )(x, gpre.reshape(1, D_MODEL), wq, memkv, wo, gpost.reshape(1, D_MODEL))


def _ffn_kernel(x_ref, gpre_ref, wg_ref, wu_ref, wd_ref, gpost_ref, o_ref, h_ref, acc_ref):
    f = pl.program_id(1)

    @pl.when(f == 0)
    def _():
        h_ref[...] = _rms(x_ref[...], gpre_ref[...]).astype(BF16)
        acc_ref[...] = jnp.zeros_like(acc_ref)

    h = h_ref[...]
    a = _dot(h, wg_ref[...])
    b = _dot(h, wu_ref[...])
    t = (a * _sigmoid(a) * b).astype(BF16)
    acc_ref[...] += _dot(t, wd_ref[...])

    @pl.when(f == pl.num_programs(1) - 1)
    def _():
        o_ref[...] = x_ref[...] + _rms(acc_ref[...], gpost_ref[...])


def ffn(x, gpre, wg, wu, wd, gpost, *, tm, tf):
    s = x.shape[0]
    return pl.pallas_call(
        _ffn_kernel,
        grid=(s // tm, D_FF // tf),
        in_specs=[
            pl.BlockSpec((tm, D_MODEL), lambda i, f: (i, 0)),
            pl.BlockSpec((1, D_MODEL), lambda i, f: (0, 0)),
            pl.BlockSpec((D_MODEL, tf), lambda i, f: (0, f)),
            pl.BlockSpec((D_MODEL, tf), lambda i, f: (0, f)),
            pl.BlockSpec((tf, D_MODEL), lambda i, f: (f, 0)),
            pl.BlockSpec((1, D_MODEL), lambda i, f: (0, 0)),
        ],
        out_specs=pl.BlockSpec((tm, D_MODEL), lambda i, f: (i, 0)),
        out_shape=jax.ShapeDtypeStruct((s, D_MODEL), F32),
        scratch_shapes=[pltpu.VMEM((tm, D_MODEL), BF16), pltpu.VMEM((tm, D_MODEL), F32)],
        compiler_params=_cparams(("parallel", "arbitrary")),
        name="ffn_swiglu",
    )(x, gpre.reshape(1, D_MODEL), wg, wu, wd, gpost.reshape(1, D_MODEL))


def _tiles(s):
    big = s >= 4096
    return dict(
        tm_in=512 if big else 256, tn_in=256,
        tk_sel=512 if big else 256,
        tm_pool=512 if big else 256,
        tm_merge=256,
        tm_x=512 if big else 256,
        tm_ffn=512 if big else 256, tf=512,
    )


def _pad_w_in(w_in):
    d = w_in.shape[0]
    n_gl = 3 * NSA_HEADS
    o_gl = NSA_WIDTH + 6 * KV_WIDTH
    o_u = o_gl + n_gl
    o_ml = o_u + POOL_WIDTH
    pad = jnp.zeros((d, GL_PAD - n_gl), w_in.dtype)
    return jnp.concatenate(
        [w_in[:, o_ml:], w_in[:, o_u:o_ml], w_in[:, :o_gl], w_in[:, o_gl:o_u], pad], axis=1).astype(BF16)


def _forward(x, mem, rel_bias, ln_mix_pre, ln_mix_post, ln_x_pre, ln_x_post, ln_mem,
             ln_ffn_pre, ln_ffn_post, w_in, cmp_pe, cmp_w1, cmp_w2, w_pool, pool_scale,
             w_br_attn, w_br_pool, w_mix_out, w_xq, w_xkv, w_xo, w_gate, w_up, w_down):
    b, s, _ = x.shape
    assert b == 1 and s % 1024 == 0
    depth = w_in.shape[0]
    tl = _tiles(s)
    nc = s // CMP_STRIDE
    half = CMP_STRIDE * HEAD_DIM
    ts, bc = bias_tables(rel_bias, s=s, tk=tl["tk_sel"])
    xc = x[0]
    mem2 = mem[0]
    for l in range(depth):
        z = norm_matmul(xc, ln_mix_pre[l], _pad_w_in(w_in[l]), tm=tl["tm_in"], tn=tl["tn_in"])
        qh = (z[:, OFF_Q:OFF_Q + NSA_WIDTH] * (HEAD_DIM ** -0.5)).reshape(s, NSA_HEADS, HEAD_DIM)
        qh = qh.transpose(1, 0, 2).astype(BF16)
        kv = z[:, OFF_KV:OFF_KV + 6 * KV_WIDTH].reshape(s, 6, NSA_GROUPS, HEAD_DIM)
        c = kv[:, 0:2].reshape(nc, CMP_STRIDE, 2, NSA_GROUPS, HEAD_DIM)
        c = c.transpose(2, 3, 0, 1, 4).reshape(2, NSA_GROUPS, nc, half)
        pe = cmp_pe[l].reshape(2, 2, 1, half)
        w1 = cmp_w1[l].reshape(2, 2, half, CMP_HID).astype(BF16)
        kvc = compress(c, pe, w1, cmp_w2[l].astype(BF16), s=s)
        kvc = jnp.concatenate([kvc[0], kvc[1]], axis=-1)
        kvc_pad = jnp.pad(kvc, ((0, 0), (nc - 8, 8), (0, 0)))
        kvs = jnp.concatenate([kv[:, 2], kv[:, 3]], axis=-1).transpose(1, 0, 2).astype(BF16)
        kvw = jnp.concatenate([kv[:, 4], kv[:, 5]], axis=-1).transpose(1, 0, 2).astype(BF16)
        glt = z[:, OFF_GL:OFF_GL + 3 * NSA_HEADS].reshape(s, NSA_GROUPS, 3 * NSA_HPG).transpose(1, 0, 2)
        a = nsa_attention(qh, kvc_pad, kvs, kvw, glt, ts, bc, s=s, tk=tl["tk_sel"])
        p = pool_mixer(z, w_pool[l].astype(BF16), pool_scale[l], tm=tl["tm_pool"])
        xc = merge_mix(a, p, z, xc, w_br_attn[l].astype(BF16), w_br_pool[l].astype(BF16),
                       w_mix_out[l].astype(BF16), ln_mix_post[l], tm=tl["tm_merge"])
        memkv = norm_matmul(mem2, ln_mem[l], w_xkv[l].astype(BF16), tm=N_MEM, tn=512, out_dtype=BF16)
        xc = cross_attention(xc, ln_x_pre[l], w_xq[l].astype(BF16), memkv, w_xo[l].astype(BF16),
                             ln_x_post[l], tm=tl["tm_x"])
        xc = ffn(xc, ln_ffn_pre[l], w_gate[l].astype(BF16), w_up[l].astype(BF16),
                 w_down[l].astype(BF16), ln_ffn_post[l], tm=tl["tm_ffn"], tf=tl["tf"])
    return xc[None]


def kernel(x, mem, rel_bias, ln_mix_pre, ln_mix_post, ln_x_pre, ln_x_post, ln_mem, ln_ffn_pre,
           ln_ffn_post, w_in, cmp_pe, cmp_w1, cmp_w2, w_pool, pool_scale, w_br_attn, w_br_pool,
           w_mix_out, w_xq, w_xkv, w_xo, w_gate, w_up, w_down):
    return _forward(x, mem, rel_bias, ln_mix_pre, ln_mix_post, ln_x_pre, ln_x_post, ln_mem,
                    ln_ffn_pre, ln_ffn_post, w_in, cmp_pe, cmp_w1, cmp_w2, w_pool, pool_scale,
                    w_br_attn, w_br_pool, w_mix_out, w_xq, w_xkv, w_xo, w_gate, w_up, w_down)
```

```python
import functools
import math

import numpy as np
import jax
import jax.numpy as jnp
from jax import lax
from jax.experimental import pallas as pl
from jax.experimental.pallas import tpu as pltpu

F32 = jnp.float32
BF16 = jnp.bfloat16

D_MODEL = 2048
DEPTH = 4
N_MEM = 256
NSA_HEADS = 16
NSA_GROUPS = 4
NSA_HPG = 4
HEAD_DIM = 64
NSA_WIDTH = 1024
KV_WIDTH = 256
CMP_LEN = 32
CMP_STRIDE = 16
CMP_HID = 128
SEL_BLK = 64
SEL_TOPN = 16
WIN = 512
Q_BLK = 128
FORCE_SCORE = 1e4
POOL_WINDOWS = (2, 4, 8, 16)
POOL_GW = 256
POOL_WIDTH = 1024
REL_BUCKETS = 32
REL_MAX_DIST = 2048
X_HEADS = 4
X_HEAD_DIM = 128
X_WIDTH = 512
D_FF = 5632
NEG = -1e30

GL_PAD = 512
OFF_ML, OFF_U, OFF_Q, OFF_KV, OFF_GL = 0, 4096, 5120, 6144, 7680
IN_PAD = 8192

VMEM_LIMIT_V7X = 56 * 1024 * 1024
LANES = 128


def _cparams(sem):
    return pltpu.CompilerParams(dimension_semantics=sem, vmem_limit_bytes=VMEM_LIMIT_V7X)


def _rms(x, g):
    ms = jnp.mean(x * x, axis=-1, keepdims=True)
    return x * lax.rsqrt(ms + 1e-6) * g


def _sigmoid(x):
    return 1.0 / (1.0 + jnp.exp(-x))


def _dot(a, b):
    return jnp.dot(a, b, preferred_element_type=F32)


def _dot_nt(a, b):
    return lax.dot_general(a, b, (((1,), (1,)), ((), ())), preferred_element_type=F32)


def _norm_matmul_kernel(x_ref, g_ref, w_ref, o_ref, h_ref):
    @pl.when(pl.program_id(1) == 0)
    def _():
        h_ref[...] = _rms(x_ref[...], g_ref[...]).astype(BF16)

    o_ref[...] = _dot(h_ref[...], w_ref[...]).astype(o_ref.dtype)


def norm_matmul(x, g, w, *, tm, tn, out_dtype=F32):
    m, k = x.shape
    n = w.shape[1]
    return pl.pallas_call(
        _norm_matmul_kernel,
        grid=(m // tm, n // tn),
        in_specs=[
            pl.BlockSpec((tm, k), lambda i, j: (i, 0)),
            pl.BlockSpec((1, k), lambda i, j: (0, 0)),
            pl.BlockSpec((k, tn), lambda i, j: (0, j)),
        ],
        out_specs=pl.BlockSpec((tm, tn), lambda i, j: (i, j)),
        out_shape=jax.ShapeDtypeStruct((m, n), out_dtype),
        scratch_shapes=[pltpu.VMEM((tm, k), BF16)],
        compiler_params=_cparams(("parallel", "arbitrary")),
        name="norm_matmul",
    )(x, g.reshape(1, k), w)


def _bucket(d):
    n = jnp.maximum(d, 0)
    exact = REL_BUCKETS // 2
    nf = jnp.maximum(n, 1).astype(F32)
    large = exact + (jnp.log(nf / exact) / math.log(REL_MAX_DIST / exact)
                     * (REL_BUCKETS - exact)).astype(jnp.int32)
    return jnp.where(n < exact, n, jnp.minimum(large, REL_BUCKETS - 1))


def _lookup(tab_ref, h, bk):
    out = jnp.full(bk.shape, tab_ref[h, 0], F32)
    for b in range(1, REL_BUCKETS):
        out = jnp.where(bk == b, tab_ref[h, b], out)
    return out


def _bias_tables_kernel(tab_ref, ts_ref, wb_ref, bc_ref, *, ls, off, lw, ncw, cmp_shift):
    h = pl.program_id(0)
    r = lax.broadcasted_iota(jnp.int32, (LANES, Q_BLK), 0)
    qi = lax.broadcasted_iota(jnp.int32, (LANES, Q_BLK), 1)

    def chunked(ref, n_rows, dist, valid):
        def body(ch, carry):
            d = dist(ch * LANES + r)
            val = _lookup(tab_ref, h, _bucket(d))
            ref[0, pl.ds(pl.multiple_of(ch * LANES, LANES), LANES), :] = jnp.where(valid(d), val, NEG)
            return carry
        lax.fori_loop(0, n_rows // LANES, body, 0)

    chunked(ts_ref, ls, lambda c: qi - c + off, lambda d: d >= 0)
    chunked(wb_ref, lw, lambda c: qi - c + WIN, lambda d: (d >= 0) & (d < WIN))
    chunked(bc_ref, ncw, lambda m: qi - CMP_STRIDE * m + cmp_shift, lambda d: d >= 0)


def _table_sizes(s, tk):
    off = REL_MAX_DIST + tk
    ls = off + 2 * tk - Q_BLK
    lw = 2 * WIN + Q_BLK
    return off, ls, lw


def bias_tables(rel_bias, *, s, tk):
    off, ls, lw = _table_sizes(s, tk)
    ncw = s // CMP_STRIDE
    cmp_shift = CMP_STRIDE * (ncw - 8) - (CMP_LEN - 1)
    kern = functools.partial(_bias_tables_kernel, ls=ls, off=off, lw=lw, ncw=ncw, cmp_shift=cmp_shift)
    return pl.pallas_call(
        kern,
        grid=(NSA_HEADS,),
        in_specs=[pl.BlockSpec(memory_space=pltpu.SMEM)],
        out_specs=[
            pl.BlockSpec((1, ls, Q_BLK), lambda h: (h, 0, 0)),
            pl.BlockSpec((1, lw, Q_BLK), lambda h: (h, 0, 0)),
            pl.BlockSpec((1, ncw, Q_BLK), lambda h: (h, 0, 0)),
        ],
        out_shape=[
            jax.ShapeDtypeStruct((NSA_HEADS, ls, Q_BLK), F32),
            jax.ShapeDtypeStruct((NSA_HEADS, lw, Q_BLK), F32),
            jax.ShapeDtypeStruct((NSA_HEADS, ncw, Q_BLK), F32),
        ],
        compiler_params=_cparams(("arbitrary",)),
        name="bias_tables",
    )(rel_bias)


def _gelu_tanh(x):
    return 0.5 * x * (1.0 + jnp.tanh(math.sqrt(2.0 / math.pi) * (x + 0.044715 * x * x * x)))


def _compress_kernel(c_ref, pe_ref, w1_ref, w2_ref, o_ref, scr_ref, *, nc):
    c = c_ref[0, 0]
    ca = (c + pe_ref[0, 0]).astype(BF16)
    cb = (c + pe_ref[0, 1]).astype(BF16)
    h1 = _dot(ca, w1_ref[0, 0])
    h2 = _dot(cb, w1_ref[0, 1])
    scr_ref[pl.ds(0, nc), :] = h2
    scr_ref[pl.ds(nc, 8), :] = jnp.zeros((8, CMP_HID), F32)
    hid = h1 + scr_ref[pl.ds(1, nc), :]
    out = _dot(_gelu_tanh(hid).astype(BF16), w2_ref[0])
    row = lax.broadcasted_iota(jnp.int32, out.shape, 0)
    o_ref[0, 0] = jnp.where(row < nc - 1, out, 0.0)


def compress(c, pe, w1, w2, *, s):
    nc = s // CMP_STRIDE
    half = CMP_STRIDE * HEAD_DIM
    return pl.pallas_call(
        functools.partial(_compress_kernel, nc=nc),
        grid=(2, NSA_GROUPS),
        in_specs=[
            pl.BlockSpec((1, 1, nc, half), lambda a, g: (a, g, 0, 0)),
            pl.BlockSpec((1, 2, 1, half), lambda a, g: (a, 0, 0, 0)),
            pl.BlockSpec((1, 2, half, CMP_HID), lambda a, g: (a, 0, 0, 0)),
            pl.BlockSpec((1, CMP_HID, HEAD_DIM), lambda a, g: (a, 0, 0)),
        ],
        out_specs=pl.BlockSpec((1, 1, nc, HEAD_DIM), lambda a, g: (a, g, 0, 0)),
        out_shape=jax.ShapeDtypeStruct((2, NSA_GROUPS, nc, HEAD_DIM), F32),
        scratch_shapes=[pltpu.VMEM((nc + 8, CMP_HID), F32)],
        compiler_params=_cparams(("arbitrary", "arbitrary")),
        name="compress_kv",
    )(c, pe, w1, w2)


def _softmax_keys(s):
    m = jnp.max(s, axis=0, keepdims=True)
    e = jnp.exp(s - m)
    l = jnp.sum(e, axis=0, keepdims=True)
    return e * jnp.where(m > 0.5 * NEG, 1.0 / l, 0.0)


def _lanes4(fn):
    return jnp.concatenate([fn(hh) for hh in range(NSA_HPG)], axis=-1)


def _nsa_kernel(q_ref, kvc_ref, ks_ref, vs_ref, kw_ref, vw_ref, gl_ref, ts_ref, wb_ref, bc_ref,
                ov_ref, o_ref, mask_ref, *, tk, ncw, nsel, off):
    i = pl.program_id(1)
    q0 = i * Q_BLK
    rows = NSA_HPG * Q_BLK
    qt = q_ref[0, 0]
    qa = jnp.concatenate([qt, jnp.zeros_like(qt)], axis=0)

    kvc = kvc_ref[0, pl.ds(pl.multiple_of(8 * i, 8), ncw), :]
    vct = kvc.T[HEAD_DIM:, :].astype(BF16)
    s = _dot(kvc.astype(BF16), qa) + _lanes4(lambda hh: bc_ref[hh])
    m_row = lax.broadcasted_iota(jnp.int32, (ncw, 1), 0)
    s = jnp.where(m_row >= (ncw - 8) - 8 * i, s, NEG)
    p = _softmax_keys(s)
    o_c = _dot(vct, p.astype(BF16))

    psum = (p[:, 0:Q_BLK] + p[:, Q_BLK:2 * Q_BLK]) + (p[:, 2 * Q_BLK:3 * Q_BLK] + p[:, 3 * Q_BLK:])
    p_hi = psum.astype(BF16)
    p_lo = (psum - p_hi.astype(F32)).astype(BF16)
    imp = _dot(ov_ref[...], p_hi) + _dot(ov_ref[...], p_lo)

    jp = lax.broadcasted_iota(jnp.int32, (nsel, Q_BLK), 0)
    jabs = jp + (2 * i - (nsel - 2))
    cur = 2 * i + (lax.broadcasted_iota(jnp.int32, (nsel, Q_BLK), 1) >= SEL_BLK).astype(jnp.int32)
    forced = (jabs == 0) | (jabs == cur) | (jabs == cur - 1)
    val = jnp.where(forced, FORCE_SCORE, imp)
    val = jnp.where(jabs > cur, -1.0, val)
    val = jnp.where(jabs < 0, -3.0, val)
    work = val
    chosen = jnp.zeros((nsel, Q_BLK), F32)
    jpf = jp.astype(F32)
    for _ in range(SEL_TOPN):
        mx = jnp.max(work, axis=0, keepdims=True)
        jmin = jnp.min(jnp.where(work == mx, jpf, 2.0 * nsel), axis=0, keepdims=True)
        hit = jpf == jmin
        chosen = jnp.where(hit, 1.0, chosen)
        work = jnp.where(hit, -1e38, work)
    sel = jnp.where(val >= 0.0, chosen, 0.0).astype(BF16)
    a_row = lax.broadcasted_iota(jnp.int32, (nsel, nsel), 0)
    s_col = lax.broadcasted_iota(jnp.int32, (nsel, nsel), 1)
    shift = jnp.where(s_col == a_row - (2 * i - (nsel - 2)), 1.0, 0.0).astype(BF16)
    mask_abs = (_dot(shift, sel) - 1.0) * (-NEG)
    mask_ref[...] = jnp.concatenate([mask_abs] * NSA_HPG, axis=-1)

    nblk = tk // SEL_BLK
    far_bias = _lanes4(lambda hh: ts_ref[hh, 0:1, :])

    n_tiles = (q0 + Q_BLK + tk - 1) // tk
    n_far = jnp.maximum(q0 - off + tk, 0) // tk

    def sel_tile(kt, carry, near):
        m, l, acc = carry
        k0 = pl.multiple_of(kt * tk, tk)
        s = _dot(ks_ref[0, pl.ds(k0, tk), :], qa)
        vt = vs_ref[0, :, pl.ds(k0, tk)]
        slab = mask_ref[pl.ds(pl.multiple_of(kt * nblk, nblk), nblk), :]
        if near:
            cb = pl.multiple_of(off - (q0 - k0), Q_BLK)
            s = s + _lanes4(lambda hh: ts_ref[hh, pl.ds(cb, tk), :])
        else:
            slab = slab + far_bias
        s = (s.reshape(nblk, SEL_BLK, rows) + slab[:, None, :]).reshape(tk, rows)
        m_new = jnp.maximum(m, jnp.max(s, axis=0, keepdims=True))
        alpha = jnp.exp(m - m_new)
        p = jnp.exp(s - m_new)
        l = alpha * l + jnp.sum(p, axis=0, keepdims=True)
        acc = alpha * acc + _dot(vt, p.astype(BF16))
        return m_new, l, acc

    init = (jnp.full((1, rows), NEG, F32), jnp.zeros((1, rows), F32), jnp.zeros((HEAD_DIM, rows), F32))
    carry = lax.fori_loop(0, n_far, functools.partial(sel_tile, near=False), init)
    _, l_s, acc_s = lax.fori_loop(n_far, n_tiles, functools.partial(sel_tile, near=True), carry)
    o_s = acc_s * (1.0 / l_s)

    wk = WIN + Q_BLK
    start = pl.multiple_of(jnp.maximum(q0 - WIN, 0), Q_BLK)
    kvw = kw_ref[0, pl.ds(start, wk), :]
    vwt = vw_ref[0, :, pl.ds(start, wk)]
    wb0 = pl.multiple_of(WIN - (q0 - start), Q_BLK)
    s = _dot(kvw, qa) + _lanes4(lambda hh: wb_ref[hh, pl.ds(wb0, wk), :])
    o_w = _dot(vwt, _softmax_keys(s).astype(BF16))

    gates = _sigmoid(gl_ref[0])
    def gate_row(c):
        return _lanes4(lambda hh: gates[3 * hh + c:3 * hh + c + 1, :])
    o = gate_row(0) * o_c + gate_row(1) * o_s + gate_row(2) * o_w
    o = jnp.concatenate([o[:, hh * Q_BLK:(hh + 1) * Q_BLK] for hh in range(NSA_HPG)], axis=0)
    o_ref[...] = o.T.astype(o_ref.dtype)


def _overlap_matrix_t(ncw, nsel):
    m = np.arange(ncw)[None, :]
    j = np.arange(nsel)[:, None]
    lo = np.maximum(m * CMP_STRIDE, j * SEL_BLK)
    hi = np.minimum(m * CMP_STRIDE + CMP_LEN, (j + 1) * SEL_BLK)
    return (np.maximum(hi - lo, 0).astype(np.float32) / CMP_LEN)


def nsa_attention(qt, kvc_pad, ks, vst, kw, vwt, glt, ts, wb, bc, *, s, tk):
    ncw = s // CMP_STRIDE
    nsel = s // SEL_BLK
    off, ls, lw = _table_sizes(s, tk)
    rows = NSA_HPG * Q_BLK
    ov = jnp.asarray(_overlap_matrix_t(ncw, nsel), BF16)
    kern = functools.partial(_nsa_kernel, tk=tk, ncw=ncw, nsel=nsel, off=off)
    per_group = dict(pipeline_mode=pl.Buffered(1))
    return pl.pallas_call(
        kern,
        grid=(NSA_GROUPS, s // Q_BLK),
        in_specs=[
            pl.BlockSpec((1, 1, HEAD_DIM, rows), lambda g, i: (g, i, 0, 0)),
            pl.BlockSpec((1, 2 * ncw, LANES), lambda g, i: (g, 0, 0), **per_group),
            pl.BlockSpec((1, s, LANES), lambda g, i: (g, 0, 0), **per_group),
            pl.BlockSpec((1, HEAD_DIM, s), lambda g, i: (g, 0, 0), **per_group),
            pl.BlockSpec((1, s, LANES), lambda g, i: (g, 0, 0), **per_group),
            pl.BlockSpec((1, HEAD_DIM, s), lambda g, i: (g, 0, 0), **per_group),
            pl.BlockSpec((1, 3 * NSA_HPG, Q_BLK), lambda g, i: (g, 0, i)),
            pl.BlockSpec((NSA_HPG, ls, Q_BLK), lambda g, i: (g, 0, 0), **per_group),
            pl.BlockSpec((NSA_HPG, lw, Q_BLK), lambda g, i: (g, 0, 0), **per_group),
            pl.BlockSpec((NSA_HPG, ncw, Q_BLK), lambda g, i: (g, 0, 0), **per_group),
            pl.BlockSpec((nsel, ncw), lambda g, i: (0, 0), **per_group),
        ],
        out_specs=pl.BlockSpec((Q_BLK, NSA_HPG * HEAD_DIM), lambda g, i: (i, g)),
        out_shape=jax.ShapeDtypeStruct((s, NSA_WIDTH), BF16),
        scratch_shapes=[pltpu.VMEM((nsel, rows), F32)],
        compiler_params=_cparams(("arbitrary", "arbitrary")),
        name="nsa_attention",
    )(qt, kvc_pad, ks, vst, kw, vwt, glt, ts, wb, bc, ov)


HALO = 16


def _pool_kernel(u_ref, halo_ref, w_ref, sc_ref, o_ref, ext_ref, *, tm):
    i = pl.program_id(0)
    halo = jnp.where(i > 0, halo_ref[...], 0.0)
    ext_ref[pl.ds(0, HALO), :] = halo
    ext_ref[pl.ds(HALO, tm), :] = u_ref[...]
    t = i * tm + lax.broadcasted_iota(jnp.int32, (tm, 1), 0)
    outs = []
    for gi, w in enumerate(POOL_WINDOWS):
        cols = pl.ds(gi * POOL_GW, POOL_GW)
        x = ext_ref[pl.ds(HALO, tm), cols]
        acc = x
        for k in range(1, w):
            acc = acc + ext_ref[pl.ds(HALO - k, tm), cols]
        cnt = jnp.minimum(t + 1, w).astype(F32)
        y = acc / cnt - x
        outs.append(_dot(y.astype(BF16), w_ref[gi]))
    o_ref[...] = (jnp.concatenate(outs, axis=-1) * sc_ref[...]).astype(o_ref.dtype)


def pool_mixer(z, w_pool, pool_scale, *, tm):
    s = z.shape[0]
    ucol = OFF_U // POOL_WIDTH
    return pl.pallas_call(
        functools.partial(_pool_kernel, tm=tm),
        grid=(s // tm,),
        in_specs=[
            pl.BlockSpec((tm, POOL_WIDTH), lambda i: (i, ucol)),
            pl.BlockSpec((HALO, POOL_WIDTH), lambda i: (jnp.maximum(i * (tm // HALO) - 1, 0), ucol)),
            pl.BlockSpec((len(POOL_WINDOWS), POOL_GW, POOL_GW), lambda i: (0, 0, 0)),
            pl.BlockSpec((1, POOL_WIDTH), lambda i: (0, 0)),
        ],
        out_specs=pl.BlockSpec((tm, POOL_WIDTH), lambda i: (i, 0)),
        out_shape=jax.ShapeDtypeStruct((s, POOL_WIDTH), BF16),
        scratch_shapes=[pltpu.VMEM((HALO + tm, POOL_WIDTH), F32)],
        compiler_params=_cparams(("parallel",)),
        name="pool_mixer",
    )(z, z, w_pool, pool_scale.reshape(1, POOL_WIDTH))


def _merge_kernel(a_ref, p_ref, ml_ref, x_ref, wa_ref, wp_ref, wo_ref, g_ref, o_ref):
    pa = _dot(a_ref[...], wa_ref[...])
    pp = _dot(p_ref[...], wp_ref[...])
    ml = ml_ref[...]
    y = _sigmoid(ml[:, :D_MODEL]) * pa + _sigmoid(ml[:, D_MODEL:]) * pp
    y = _dot(y.astype(BF16), wo_ref[...])
    o_ref[...] = x_ref[...] + _rms(y, g_ref[...])


def merge_mix(a, p, z, x, wa, wp, wo, g, *, tm):
    s = x.shape[0]
    const = lambda i: (0, 0)
    return pl.pallas_call(
        _merge_kernel,
        grid=(s // tm,),
        in_specs=[
            pl.BlockSpec((tm, NSA_WIDTH), lambda i: (i, 0)),
            pl.BlockSpec((tm, POOL_WIDTH), lambda i: (i, 0)),
            pl.BlockSpec((tm, 2 * D_MODEL), lambda i: (i, OFF_ML // (2 * D_MODEL))),
            pl.BlockSpec((tm, D_MODEL), lambda i: (i, 0)),
            pl.BlockSpec((NSA_WIDTH, D_MODEL), const, pipeline_mode=pl.Buffered(1)),
            pl.BlockSpec((POOL_WIDTH, D_MODEL), const, pipeline_mode=pl.Buffered(1)),
            pl.BlockSpec((D_MODEL, D_MODEL), const, pipeline_mode=pl.Buffered(1)),
            pl.BlockSpec((1, D_MODEL), const),
        ],
        out_specs=pl.BlockSpec((tm, D_MODEL), lambda i: (i, 0)),
        out_shape=jax.ShapeDtypeStruct((s, D_MODEL), F32),
        compiler_params=_cparams(("parallel",)),
        name="merge_mix",
    )(a, p, z, x, wa, wp, wo, g.reshape(1, D_MODEL))


def _xattn_kernel(x_ref, gpre_ref, wq_ref, kv_ref, wo_ref, gpost_ref, o_ref):
    x = x_ref[...]
    h = _rms(x, gpre_ref[...]).astype(BF16)
    q = _dot(h, wq_ref[...]).astype(BF16)
    kv = kv_ref[...]
    outs = []
    for hh in range(X_HEADS):
        lo = hh * X_HEAD_DIM
        sc = _dot_nt(q[:, lo:lo + X_HEAD_DIM], kv[:, lo:lo + X_HEAD_DIM]) * (X_HEAD_DIM ** -0.5)
        m = jnp.max(sc, axis=-1, keepdims=True)
        e = jnp.exp(sc - m)
        p = (e / jnp.sum(e, axis=-1, keepdims=True)).astype(BF16)
        outs.append(_dot(p, kv[:, X_WIDTH + lo:X_WIDTH + lo + X_HEAD_DIM]))
    o = jnp.concatenate(outs, axis=-1).astype(BF16)
    y = _dot(o, wo_ref[...])
    o_ref[...] = x + _rms(y, gpost_ref[...])


def cross_attention(x, gpre, wq, memkv, wo, gpost, *, tm):
    s = x.shape[0]
    const = lambda i: (0, 0)
    return pl.pallas_call(
        _xattn_kernel,
        grid=(s // tm,),
        in_specs=[
            pl.BlockSpec((tm, D_MODEL), lambda i: (i, 0)),
            pl.BlockSpec((1, D_MODEL), const),
            pl.BlockSpec((D_MODEL, X_WIDTH), const),
            pl.BlockSpec((N_MEM, 2 * X_WIDTH), const),
            pl.BlockSpec((X_WIDTH, D_MODEL), const),
            pl.BlockSpec((1, D_MODEL), const),
        ],
        out_specs=pl.BlockSpec((tm, D_MODEL), lambda i: (i, 0)),
        out_shape=jax.ShapeDtypeStruct((s, D_MODEL), F32),
        compiler_params=_cparams(("parallel",)),
        name="cross_attention",
    )(x, gpre.reshape(1, D_MODEL), wq, memkv, wo, gpost.reshape(1, D_MODEL))


def _ffn_kernel(x_ref, gpre_ref, wg_ref, wu_ref, wd_ref, gpost_ref, o_ref, h_ref, acc_ref):
    f = pl.program_id(1)

    @pl.when(f == 0)
    def _():
        h_ref[...] = _rms(x_ref[...], gpre_ref[...]).astype(BF16)
        acc_ref[...] = jnp.zeros_like(acc_ref)

    h = h_ref[...]
    a = _dot(h, wg_ref[...])
    b = _dot(h, wu_ref[...])
    t = (a * _sigmoid(a) * b).astype(BF16)
    acc_ref[...] += _dot(t, wd_ref[...])

    @pl.when(f == pl.num_programs(1) - 1)
    def _():
        o_ref[...] = x_ref[...] + _rms(acc_ref[...], gpost_ref[...])


def ffn(x, gpre, wg, wu, wd, gpost, *, tm, tf):
    s = x.shape[0]
    return pl.pallas_call(
        _ffn_kernel,
        grid=(s // tm, D_FF // tf),
        in_specs=[
            pl.BlockSpec((tm, D_MODEL), lambda i, f: (i, 0)),
            pl.BlockSpec((1, D_MODEL), lambda i, f: (0, 0)),
            pl.BlockSpec((D_MODEL, tf), lambda i, f: (0, f)),
            pl.BlockSpec((D_MODEL, tf), lambda i, f: (0, f)),
            pl.BlockSpec((tf, D_MODEL), lambda i, f: (f, 0)),
            pl.BlockSpec((1, D_MODEL), lambda i, f: (0, 0)),
        ],
        out_specs=pl.BlockSpec((tm, D_MODEL), lambda i, f: (i, 0)),
        out_shape=jax.ShapeDtypeStruct((s, D_MODEL), F32),
        scratch_shapes=[pltpu.VMEM((tm, D_MODEL), BF16), pltpu.VMEM((tm, D_MODEL), F32)],
        compiler_params=_cparams(("parallel", "arbitrary")),
        name="ffn_swiglu",
    )(x, gpre.reshape(1, D_MODEL), wg, wu, wd, gpost.reshape(1, D_MODEL))


def _tiles(s):
    big = s >= 4096
    return dict(
        tm_in=1024 if big else 256, tn_in=512,
        tk_sel=512 if big else 256,
        tm_pool=512 if big else 256,
        tm_merge=256,
        tm_x=512 if big else 256,
        tm_ffn=512 if big else 256, tf=512,
    )


def _pad_w_in(w_in):
    d = w_in.shape[0]
    n_gl = 3 * NSA_HEADS
    o_gl = NSA_WIDTH + 6 * KV_WIDTH
    o_u = o_gl + n_gl
    o_ml = o_u + POOL_WIDTH
    pad = jnp.zeros((d, GL_PAD - n_gl), w_in.dtype)
    return jnp.concatenate(
        [w_in[:, o_ml:], w_in[:, o_u:o_ml], w_in[:, :o_gl], w_in[:, o_gl:o_u], pad], axis=1).astype(BF16)


def _forward(x, mem, rel_bias, ln_mix_pre, ln_mix_post, ln_x_pre, ln_x_post, ln_mem,
             ln_ffn_pre, ln_ffn_post, w_in, cmp_pe, cmp_w1, cmp_w2, w_pool, pool_scale,
             w_br_attn, w_br_pool, w_mix_out, w_xq, w_xkv, w_xo, w_gate, w_up, w_down):
    b, s, _ = x.shape
    assert b == 1 and s % 1024 == 0
    depth = w_in.shape[0]
    tl = _tiles(s)
    nc = s // CMP_STRIDE
    nq = s // Q_BLK
    half = CMP_STRIDE * HEAD_DIM
    ts, wb, bc = bias_tables(rel_bias, s=s, tk=tl["tk_sel"])
    xc = x[0]
    mem2 = mem[0]
    for l in range(depth):
        z = norm_matmul(xc, ln_mix_pre[l], _pad_w_in(w_in[l]), tm=tl["tm_in"], tn=tl["tn_in"])
        q = z[:, OFF_Q:OFF_Q + NSA_WIDTH] * (HEAD_DIM ** -0.5)
        qt = q.reshape(nq, Q_BLK, NSA_GROUPS, NSA_HPG, HEAD_DIM).transpose(2, 0, 4, 3, 1)
        qt = qt.reshape(NSA_GROUPS, nq, HEAD_DIM, NSA_HPG * Q_BLK).astype(BF16)
        kv = z[:, OFF_KV:OFF_KV + 6 * KV_WIDTH].reshape(s, 6, NSA_GROUPS, HEAD_DIM)
        c = kv[:, 0:2].reshape(nc, CMP_STRIDE, 2, NSA_GROUPS, HEAD_DIM)
        c = c.transpose(2, 3, 0, 1, 4).reshape(2, NSA_GROUPS, nc, half)
        pe = cmp_pe[l].reshape(2, 2, 1, half)
        w1 = cmp_w1[l].reshape(2, 2, half, CMP_HID).astype(BF16)
        kvc = compress(c, pe, w1, cmp_w2[l].astype(BF16), s=s)
        kvc = jnp.concatenate([kvc[0], kvc[1]], axis=-1)
        kvc_pad = jnp.pad(kvc, ((0, 0), (nc - 8, 8), (0, 0)))
        ks = jnp.concatenate([kv[:, 2], kv[:, 3]], axis=-1).transpose(1, 0, 2).astype(BF16)
        kw = jnp.concatenate([kv[:, 4], kv[:, 5]], axis=-1).transpose(1, 0, 2).astype(BF16)
        vst = kv[:, 3].transpose(1, 2, 0).astype(BF16)
        vwt = kv[:, 5].transpose(1, 2, 0).astype(BF16)
        glt = z[:, OFF_GL:OFF_GL + 3 * NSA_HEADS].reshape(s, NSA_GROUPS, 3 * NSA_HPG).transpose(1, 2, 0)
        a = nsa_attention(qt, kvc_pad, ks, vst, kw, vwt, glt, ts, wb, bc, s=s, tk=tl["tk_sel"])
        p = pool_mixer(z, w_pool[l].astype(BF16), pool_scale[l], tm=tl["tm_pool"])
        xc = merge_mix(a, p, z, xc, w_br_attn[l].astype(BF16), w_br_pool[l].astype(BF16),
                       w_mix_out[l].astype(BF16), ln_mix_post[l], tm=tl["tm_merge"])
        memkv = norm_matmul(mem2, ln_mem[l], w_xkv[l].astype(BF16), tm=N_MEM, tn=512, out_dtype=BF16)
        xc = cross_attention(xc, ln_x_pre[l], w_xq[l].astype(BF16), memkv, w_xo[l].astype(BF16),
                             ln_x_post[l], tm=tl["tm_x"])
        xc = ffn(xc, ln_ffn_pre[l], w_gate[l].astype(BF16), w_up[l].astype(BF16),
                 w_down[l].astype(BF16), ln_ffn_post[l], tm=tl["tm_ffn"], tf=tl["tf"])
    return xc[None]


def kernel(x, mem, rel_bias, ln_mix_pre, ln_mix_post, ln_x_pre, ln_x_post, ln_mem, ln_ffn_pre,
           ln_ffn_post, w_in, cmp_pe, cmp_w1, cmp_w2, w_pool, pool_scale, w_br_attn, w_br_pool,
           w_mix_out, w_xq, w_xkv, w_xo, w_gate, w_up, w_down):
    return _forward(x, mem, rel_bias, ln_mix_pre, ln_mix_post, ln_x_pre, ln_x_post, ln_mem,
                    ln_ffn_pre, ln_ffn_post, w_in, cmp_pe, cmp_w1, cmp_w2, w_pool, pool_scale,
                    w_br_attn, w_br_pool, w_mix_out, w_xq, w_xkv, w_xo, w_gate, w_up, w_down)
```

```python
import functools
import math

import numpy as np
import jax
import jax.numpy as jnp
from jax import lax
from jax.experimental import pallas as pl
from jax.experimental.pallas import tpu as pltpu

F32 = jnp.float32
BF16 = jnp.bfloat16

D_MODEL = 2048
DEPTH = 4
N_MEM = 256
NSA_HEADS = 16
NSA_GROUPS = 4
NSA_HPG = 4
HEAD_DIM = 64
NSA_WIDTH = 1024
KV_WIDTH = 256
CMP_LEN = 32
CMP_STRIDE = 16
CMP_HID = 128
SEL_BLK = 64
SEL_TOPN = 16
WIN = 512
Q_BLK = 128
FORCE_SCORE = 1e4
POOL_WINDOWS = (2, 4, 8, 16)
POOL_GW = 256
POOL_WIDTH = 1024
REL_BUCKETS = 32
REL_MAX_DIST = 2048
X_HEADS = 4
X_HEAD_DIM = 128
X_WIDTH = 512
D_FF = 5632
NEG = -1e30
LOG2E = 1.4426950408889634

GL_PAD = 512
OFF_ML, OFF_U, OFF_Q, OFF_KV, OFF_GL = 0, 4096, 5120, 6144, 7680
IN_PAD = 8192

VMEM_LIMIT_V7X = 56 * 1024 * 1024
LANES = 128


def _cparams(sem):
    return pltpu.CompilerParams(dimension_semantics=sem, vmem_limit_bytes=VMEM_LIMIT_V7X)


def _rms(x, g):
    ms = jnp.mean(x * x, axis=-1, keepdims=True)
    return x * lax.rsqrt(ms + 1e-6) * g


def _sigmoid(x):
    return 1.0 / (1.0 + jnp.exp(-x))


def _dot(a, b):
    return jnp.dot(a, b, preferred_element_type=F32)


def _dot_nt(a, b):
    return lax.dot_general(a, b, (((1,), (1,)), ((), ())), preferred_element_type=F32)


def _norm_matmul_kernel(x_ref, g_ref, w_ref, o_ref, h_ref):
    @pl.when(pl.program_id(1) == 0)
    def _():
        h_ref[...] = _rms(x_ref[...], g_ref[...]).astype(BF16)

    o_ref[...] = _dot(h_ref[...], w_ref[...]).astype(o_ref.dtype)


def norm_matmul(x, g, w, *, tm, tn, out_dtype=F32):
    m, k = x.shape
    n = w.shape[1]
    return pl.pallas_call(
        _norm_matmul_kernel,
        grid=(m // tm, n // tn),
        in_specs=[
            pl.BlockSpec((tm, k), lambda i, j: (i, 0)),
            pl.BlockSpec((1, k), lambda i, j: (0, 0)),
            pl.BlockSpec((k, tn), lambda i, j: (0, j)),
        ],
        out_specs=pl.BlockSpec((tm, tn), lambda i, j: (i, j)),
        out_shape=jax.ShapeDtypeStruct((m, n), out_dtype),
        scratch_shapes=[pltpu.VMEM((tm, k), BF16)],
        compiler_params=_cparams(("parallel", "arbitrary")),
        name="norm_matmul",
    )(x, g.reshape(1, k), w)


def _bucket(d):
    n = jnp.maximum(d, 0)
    exact = REL_BUCKETS // 2
    nf = jnp.maximum(n, 1).astype(F32)
    large = exact + (jnp.log(nf / exact) / math.log(REL_MAX_DIST / exact)
                     * (REL_BUCKETS - exact)).astype(jnp.int32)
    return jnp.where(n < exact, n, jnp.minimum(large, REL_BUCKETS - 1))


def _lookup(tab_ref, h, bk):
    out = jnp.full(bk.shape, tab_ref[h, 0], F32)
    for b in range(1, REL_BUCKETS):
        out = jnp.where(bk == b, tab_ref[h, b], out)
    return out


def _bias_tables_kernel(tab_ref, ts_ref, wb_ref, bc_ref, *, ls, off, lw, ncw, cmp_shift):
    h = pl.program_id(0)
    r = lax.broadcasted_iota(jnp.int32, (LANES, Q_BLK), 0)
    qi = lax.broadcasted_iota(jnp.int32, (LANES, Q_BLK), 1)

    def chunked(ref, n_rows, dist, valid):
        def body(ch, carry):
            d = dist(ch * LANES + r)
            val = _lookup(tab_ref, h, _bucket(d))
            ref[0, pl.ds(pl.multiple_of(ch * LANES, LANES), LANES), :] = jnp.where(valid(d), val * LOG2E, NEG)
            return carry
        lax.fori_loop(0, n_rows // LANES, body, 0)

    chunked(ts_ref, ls, lambda c: qi - c + off, lambda d: d >= 0)
    chunked(wb_ref, lw, lambda c: qi - c + WIN, lambda d: (d >= 0) & (d < WIN))
    chunked(bc_ref, ncw, lambda m: qi - CMP_STRIDE * m + cmp_shift, lambda d: d >= 0)


def _table_sizes(s, tk):
    off = REL_MAX_DIST + tk
    ls = off + 3 * tk - Q_BLK
    lw = 2 * WIN + Q_BLK
    return off, ls, lw


def bias_tables(rel_bias, *, s, tk):
    off, ls, lw = _table_sizes(s, tk)
    ncw = s // CMP_STRIDE
    cmp_shift = CMP_STRIDE * (ncw - 8) - (CMP_LEN - 1)
    kern = functools.partial(_bias_tables_kernel, ls=ls, off=off, lw=lw, ncw=ncw, cmp_shift=cmp_shift)
    return pl.pallas_call(
        kern,
        grid=(NSA_HEADS,),
        in_specs=[pl.BlockSpec(memory_space=pltpu.SMEM)],
        out_specs=[
            pl.BlockSpec((1, ls, Q_BLK), lambda h: (h, 0, 0)),
            pl.BlockSpec((1, lw, Q_BLK), lambda h: (h, 0, 0)),
            pl.BlockSpec((1, ncw, Q_BLK), lambda h: (h, 0, 0)),
        ],
        out_shape=[
            jax.ShapeDtypeStruct((NSA_HEADS, ls, Q_BLK), F32),
            jax.ShapeDtypeStruct((NSA_HEADS, lw, Q_BLK), F32),
            jax.ShapeDtypeStruct((NSA_HEADS, ncw, Q_BLK), F32),
        ],
        compiler_params=_cparams(("arbitrary",)),
        name="bias_tables",
    )(rel_bias)


def _gelu_tanh(x):
    return 0.5 * x * (1.0 + jnp.tanh(math.sqrt(2.0 / math.pi) * (x + 0.044715 * x * x * x)))


def _compress_kernel(c_ref, pe_ref, w1_ref, w2_ref, o_ref, scr_ref, *, nc):
    c = c_ref[0, 0]
    ca = (c + pe_ref[0, 0]).astype(BF16)
    cb = (c + pe_ref[0, 1]).astype(BF16)
    h1 = _dot(ca, w1_ref[0, 0])
    h2 = _dot(cb, w1_ref[0, 1])
    scr_ref[pl.ds(0, nc), :] = h2
    scr_ref[pl.ds(nc, 8), :] = jnp.zeros((8, CMP_HID), F32)
    hid = h1 + scr_ref[pl.ds(1, nc), :]
    out = _dot(_gelu_tanh(hid).astype(BF16), w2_ref[0])
    row = lax.broadcasted_iota(jnp.int32, out.shape, 0)
    o_ref[0, 0] = jnp.where(row < nc - 1, out, 0.0)


def compress(c, pe, w1, w2, *, s):
    nc = s // CMP_STRIDE
    half = CMP_STRIDE * HEAD_DIM
    return pl.pallas_call(
        functools.partial(_compress_kernel, nc=nc),
        grid=(2, NSA_GROUPS),
        in_specs=[
            pl.BlockSpec((1, 1, nc, half), lambda a, g: (a, g, 0, 0)),
            pl.BlockSpec((1, 2, 1, half), lambda a, g: (a, 0, 0, 0)),
            pl.BlockSpec((1, 2, half, CMP_HID), lambda a, g: (a, 0, 0, 0)),
            pl.BlockSpec((1, CMP_HID, HEAD_DIM), lambda a, g: (a, 0, 0)),
        ],
        out_specs=pl.BlockSpec((1, 1, nc, HEAD_DIM), lambda a, g: (a, g, 0, 0)),
        out_shape=jax.ShapeDtypeStruct((2, NSA_GROUPS, nc, HEAD_DIM), F32),
        scratch_shapes=[pltpu.VMEM((nc + 8, CMP_HID), F32)],
        compiler_params=_cparams(("arbitrary", "arbitrary")),
        name="compress_kv",
    )(c, pe, w1, w2)


def _softmax_keys(s):
    m = jnp.max(s, axis=0, keepdims=True)
    e = jnp.exp2(s - m)
    l = jnp.sum(e, axis=0, keepdims=True)
    return e * jnp.where(m > 0.5 * NEG, 1.0 / l, 0.0)


def _lanes4(fn):
    return jnp.concatenate([fn(hh) for hh in range(NSA_HPG)], axis=-1)


def _nsa_kernel(q_ref, kvc_ref, ks_ref, vs_ref, kw_ref, vw_ref, gl_ref, ts_ref, wb_ref, bc_ref,
                ov_ref, o_ref, mask_ref, sa_ref, sb_ref, *, tk, ncw, nsel, off):
    i = pl.program_id(1)
    q0 = i * Q_BLK
    rows = NSA_HPG * Q_BLK
    qt = q_ref[0, 0]
    qa = jnp.concatenate([qt, jnp.zeros_like(qt)], axis=0)

    kvc = kvc_ref[0, pl.ds(pl.multiple_of(8 * i, 8), ncw), :]
    vct = kvc.T[HEAD_DIM:, :].astype(BF16)
    s = _dot(kvc.astype(BF16), qa) + _lanes4(lambda hh: bc_ref[hh])
    m_row = lax.broadcasted_iota(jnp.int32, (ncw, 1), 0)
    s = jnp.where(m_row >= (ncw - 8) - 8 * i, s, NEG)
    p = _softmax_keys(s)
    o_c = _dot(vct, p.astype(BF16))

    psum = (p[:, 0:Q_BLK] + p[:, Q_BLK:2 * Q_BLK]) + (p[:, 2 * Q_BLK:3 * Q_BLK] + p[:, 3 * Q_BLK:])
    p_hi = psum.astype(BF16)
    p_lo = (psum - p_hi.astype(F32)).astype(BF16)
    imp = _dot(ov_ref[...], p_hi) + _dot(ov_ref[...], p_lo)

    jp = lax.broadcasted_iota(jnp.int32, (nsel, Q_BLK), 0)
    jabs = jp + (2 * i - (nsel - 2))
    cur = 2 * i + (lax.broadcasted_iota(jnp.int32, (nsel, Q_BLK), 1) >= SEL_BLK).astype(jnp.int32)
    forced = (jabs == 0) | (jabs == cur) | (jabs == cur - 1)
    val = jnp.where(forced, FORCE_SCORE, imp)
    val = jnp.where(jabs > cur, -1.0, val)
    val = jnp.where(jabs < 0, -3.0, val)
    work = val
    chosen = jnp.zeros((nsel, Q_BLK), F32)
    jpf = jp.astype(F32)
    for _ in range(SEL_TOPN):
        mx = jnp.max(work, axis=0, keepdims=True)
        jmin = jnp.min(jnp.where(work == mx, jpf, 2.0 * nsel), axis=0, keepdims=True)
        hit = jpf == jmin
        chosen = jnp.where(hit, 1.0, chosen)
        work = jnp.where(hit, -1e38, work)
    sel = jnp.where(val >= 0.0, chosen, 0.0).astype(BF16)
    a_row = lax.broadcasted_iota(jnp.int32, (nsel, nsel), 0)
    s_col = lax.broadcasted_iota(jnp.int32, (nsel, nsel), 1)
    shift = jnp.where(s_col == a_row - (2 * i - (nsel - 2)), 1.0, 0.0).astype(BF16)
    mask_abs = (_dot(shift, sel) - 1.0) * (-NEG)
    mask_ref[...] = jnp.concatenate([mask_abs] * NSA_HPG, axis=-1)

    wk = WIN + Q_BLK
    start = pl.multiple_of(jnp.maximum(q0 - WIN, 0), Q_BLK)
    kvw = kw_ref[0, pl.ds(start, wk), :]
    vwt = vw_ref[0, :, pl.ds(start, wk)]
    wb0 = pl.multiple_of(WIN - (q0 - start), Q_BLK)
    s = _dot(kvw, qa) + _lanes4(lambda hh: wb_ref[hh, pl.ds(wb0, wk), :])
    o_w = _dot(vwt, _softmax_keys(s).astype(BF16))

    nblk = tk // SEL_BLK
    far_bias = _lanes4(lambda hh: ts_ref[hh, 0:1, :])
    far_hi = far_bias.astype(BF16).astype(F32)
    far_lo = (far_bias - far_hi).astype(BF16).astype(F32)
    far_lo2 = (far_bias - far_hi - far_lo).astype(BF16).astype(F32)
    far_rows = jnp.concatenate([far_hi, far_lo, far_lo2, jnp.zeros((8 - FAR_TERMS, rows), F32)], axis=0)
    n_pad = AUG_ROWS - nblk - 8
    aug_pad = [jnp.zeros((n_pad, rows), F32)] if n_pad else []
    rhs_tail = jnp.zeros((LANES - HEAD_DIM - AUG_ROWS, rows), BF16)
    n_tiles = (q0 + Q_BLK + tk - 1) // tk
    n_pairs = (n_tiles + 1) // 2
    n_far_pairs = (jnp.maximum(q0 - off + tk, 0) // tk) // 2
    last_tile = ks_ref.shape[1] // tk - 1

    def scores_into(buf, kt):
        kt = jnp.minimum(kt, last_tile)
        k0 = pl.multiple_of(kt * tk, tk)
        slab = mask_ref[pl.ds(pl.multiple_of(kt * nblk, nblk), nblk), :]
        far = jnp.where(kt < 2 * n_far_pairs, far_rows, 0.0)
        aug = jnp.concatenate([slab, far] + aug_pad, axis=0).astype(BF16)
        rhs = jnp.concatenate([qt, aug, rhs_tail], axis=0)
        buf[...] = _dot(ks_ref[0, pl.ds(k0, tk), :], rhs)

    def sel_tile(buf, kt, carry, near):
        m, l, acc = carry
        k0 = pl.multiple_of(kt * tk, tk)
        s = buf[...]
        vt = vs_ref[0, :, pl.ds(k0, tk)]
        if near:
            cb = pl.multiple_of(jnp.maximum(off - (q0 - k0), 0), Q_BLK)
            s = s + _lanes4(lambda hh: ts_ref[hh, pl.ds(cb, tk), :])
        m_new = jnp.maximum(m, jnp.max(s, axis=0, keepdims=True))
        alpha = jnp.exp2(m - m_new)
        p = jnp.exp2(s - m_new)
        l = alpha * l + jnp.sum(p, axis=0, keepdims=True)
        acc = alpha * acc + _dot(vt, p.astype(BF16))
        return m_new, l, acc

    def sel_pair(pi, carry, near):
        kt = 2 * pi
        scores_into(sb_ref, kt + 1)
        carry = sel_tile(sa_ref, kt, carry, near)
        scores_into(sa_ref, kt + 2)
        return sel_tile(sb_ref, kt + 1, carry, near)

    scores_into(sa_ref, 0)
    init = (jnp.full((1, rows), NEG, F32), jnp.zeros((1, rows), F32), jnp.zeros((HEAD_DIM, rows), F32))
    carry = lax.fori_loop(0, n_far_pairs, functools.partial(sel_pair, near=False), init)
    _, l_s, acc_s = lax.fori_loop(n_far_pairs, n_pairs, functools.partial(sel_pair, near=True), carry)
    o_s = acc_s * (1.0 / l_s)

    gates = _sigmoid(gl_ref[0])
    def gate_row(c):
        return _lanes4(lambda hh: gates[3 * hh + c:3 * hh + c + 1, :])
    o = gate_row(0) * o_c + gate_row(1) * o_s + gate_row(2) * o_w
    o = jnp.concatenate([o[:, hh * Q_BLK:(hh + 1) * Q_BLK] for hh in range(NSA_HPG)], axis=0)
    o_ref[...] = o.T.astype(o_ref.dtype)


FAR_TERMS = 3
AUG_ROWS = 16


def _key_aug_lanes(s, tk):
    nblk = tk // SEL_BLK
    aug = np.zeros((s, HEAD_DIM), np.float32)
    aug[np.arange(s), (np.arange(s) % tk) // SEL_BLK] = 1.0
    aug[:, nblk:nblk + FAR_TERMS] = 1.0
    return aug


def _overlap_matrix_t(ncw, nsel):
    m = np.arange(ncw)[None, :]
    j = np.arange(nsel)[:, None]
    lo = np.maximum(m * CMP_STRIDE, j * SEL_BLK)
    hi = np.minimum(m * CMP_STRIDE + CMP_LEN, (j + 1) * SEL_BLK)
    return (np.maximum(hi - lo, 0).astype(np.float32) / CMP_LEN)


def nsa_attention(qt, kvc_pad, ks, vst, kw, vwt, glt, ts, wb, bc, *, s, tk):
    ncw = s // CMP_STRIDE
    nsel = s // SEL_BLK
    off, ls, lw = _table_sizes(s, tk)
    rows = NSA_HPG * Q_BLK
    ov = jnp.asarray(_overlap_matrix_t(ncw, nsel), BF16)
    kern = functools.partial(_nsa_kernel, tk=tk, ncw=ncw, nsel=nsel, off=off)
    per_group = dict(pipeline_mode=pl.Buffered(1))
    return pl.pallas_call(
        kern,
        grid=(NSA_GROUPS, s // Q_BLK),
        in_specs=[
            pl.BlockSpec((1, 1, HEAD_DIM, rows), lambda g, i: (g, i, 0, 0)),
            pl.BlockSpec((1, 2 * ncw, LANES), lambda g, i: (g, 0, 0), **per_group),
            pl.BlockSpec((1, s, LANES), lambda g, i: (g, 0, 0), **per_group),
            pl.BlockSpec((1, HEAD_DIM, s), lambda g, i: (g, 0, 0), **per_group),
            pl.BlockSpec((1, s, LANES), lambda g, i: (g, 0, 0), **per_group),
            pl.BlockSpec((1, HEAD_DIM, s), lambda g, i: (g, 0, 0), **per_group),
            pl.BlockSpec((1, 3 * NSA_HPG, Q_BLK), lambda g, i: (g, 0, i)),
            pl.BlockSpec((NSA_HPG, ls, Q_BLK), lambda g, i: (g, 0, 0), **per_group),
            pl.BlockSpec((NSA_HPG, lw, Q_BLK), lambda g, i: (g, 0, 0), **per_group),
            pl.BlockSpec((NSA_HPG, ncw, Q_BLK), lambda g, i: (g, 0, 0), **per_group),
            pl.BlockSpec((nsel, ncw), lambda g, i: (0, 0), **per_group),
        ],
        out_specs=pl.BlockSpec((Q_BLK, NSA_HPG * HEAD_DIM), lambda g, i: (i, g)),
        out_shape=jax.ShapeDtypeStruct((s, NSA_WIDTH), BF16),
        scratch_shapes=[pltpu.VMEM((nsel, rows), F32), pltpu.VMEM((tk, rows), F32),
                        pltpu.VMEM((tk, rows), F32)],
        compiler_params=_cparams(("arbitrary", "arbitrary")),
        name="nsa_attention",
    )(qt, kvc_pad, ks, vst, kw, vwt, glt, ts, wb, bc, ov)


HALO = 16


def _pool_kernel(u_ref, halo_ref, w_ref, sc_ref, o_ref, ext_ref, *, tm):
    i = pl.program_id(0)
    halo = jnp.where(i > 0, halo_ref[...], 0.0)
    ext_ref[pl.ds(0, HALO), :] = halo
    ext_ref[pl.ds(HALO, tm), :] = u_ref[...]
    t = i * tm + lax.broadcasted_iota(jnp.int32, (tm, 1), 0)
    outs = []
    for gi, w in enumerate(POOL_WINDOWS):
        cols = pl.ds(gi * POOL_GW, POOL_GW)
        x = ext_ref[pl.ds(HALO, tm), cols]
        acc = x
        for k in range(1, w):
            acc = acc + ext_ref[pl.ds(HALO - k, tm), cols]
        cnt = jnp.minimum(t + 1, w).astype(F32)
        y = acc / cnt - x
        outs.append(_dot(y.astype(BF16), w_ref[gi]))
    o_ref[...] = (jnp.concatenate(outs, axis=-1) * sc_ref[...]).astype(o_ref.dtype)


def pool_mixer(z, w_pool, pool_scale, *, tm):
    s = z.shape[0]
    ucol = OFF_U // POOL_WIDTH
    return pl.pallas_call(
        functools.partial(_pool_kernel, tm=tm),
        grid=(s // tm,),
        in_specs=[
            pl.BlockSpec((tm, POOL_WIDTH), lambda i: (i, ucol)),
            pl.BlockSpec((HALO, POOL_WIDTH), lambda i: (jnp.maximum(i * (tm // HALO) - 1, 0), ucol)),
            pl.BlockSpec((len(POOL_WINDOWS), POOL_GW, POOL_GW), lambda i: (0, 0, 0)),
            pl.BlockSpec((1, POOL_WIDTH), lambda i: (0, 0)),
        ],
        out_specs=pl.BlockSpec((tm, POOL_WIDTH), lambda i: (i, 0)),
        out_shape=jax.ShapeDtypeStruct((s, POOL_WIDTH), BF16),
        scratch_shapes=[pltpu.VMEM((HALO + tm, POOL_WIDTH), F32)],
        compiler_params=_cparams(("parallel",)),
        name="pool_mixer",
    )(z, z, w_pool, pool_scale.reshape(1, POOL_WIDTH))


def _merge_kernel(a_ref, p_ref, ml_ref, x_ref, wa_ref, wp_ref, wo_ref, g_ref, o_ref):
    pa = _dot(a_ref[...], wa_ref[...])
    pp = _dot(p_ref[...], wp_ref[...])
    ml = ml_ref[...]
    y = _sigmoid(ml[:, :D_MODEL]) * pa + _sigmoid(ml[:, D_MODEL:]) * pp
    y = _dot(y.astype(BF16), wo_ref[...])
    o_ref[...] = x_ref[...] + _rms(y, g_ref[...])


def merge_mix(a, p, z, x, wa, wp, wo, g, *, tm):
    s = x.shape[0]
    const = lambda i: (0, 0)
    return pl.pallas_call(
        _merge_kernel,
        grid=(s // tm,),
        in_specs=[
            pl.BlockSpec((tm, NSA_WIDTH), lambda i: (i, 0)),
            pl.BlockSpec((tm, POOL_WIDTH), lambda i: (i, 0)),
            pl.BlockSpec((tm, 2 * D_MODEL), lambda i: (i, OFF_ML // (2 * D_MODEL))),
            pl.BlockSpec((tm, D_MODEL), lambda i: (i, 0)),
            pl.BlockSpec((NSA_WIDTH, D_MODEL), const, pipeline_mode=pl.Buffered(1)),
            pl.BlockSpec((POOL_WIDTH, D_MODEL), const, pipeline_mode=pl.Buffered(1)),
            pl.BlockSpec((D_MODEL, D_MODEL), const, pipeline_mode=pl.Buffered(1)),
            pl.BlockSpec((1, D_MODEL), const),
        ],
        out_specs=pl.BlockSpec((tm, D_MODEL), lambda i: (i, 0)),
        out_shape=jax.ShapeDtypeStruct((s, D_MODEL), F32),
        compiler_params=_cparams(("parallel",)),
        name="merge_mix",
    )(a, p, z, x, wa, wp, wo, g.reshape(1, D_MODEL))


def _xattn_kernel(x_ref, gpre_ref, wq_ref, kv_ref, wo_ref, gpost_ref, o_ref):
    x = x_ref[...]
    h = _rms(x, gpre_ref[...]).astype(BF16)
    q = _dot(h, wq_ref[...]).astype(BF16)
    kv = kv_ref[...]
    outs = []
    for hh in range(X_HEADS):
        lo = hh * X_HEAD_DIM
        sc = _dot_nt(q[:, lo:lo + X_HEAD_DIM], kv[:, lo:lo + X_HEAD_DIM]) * (X_HEAD_DIM ** -0.5)
        m = jnp.max(sc, axis=-1, keepdims=True)
        e = jnp.exp(sc - m)
        p = (e / jnp.sum(e, axis=-1, keepdims=True)).astype(BF16)
        outs.append(_dot(p, kv[:, X_WIDTH + lo:X_WIDTH + lo + X_HEAD_DIM]))
    o = jnp.concatenate(outs, axis=-1).astype(BF16)
    y = _dot(o, wo_ref[...])
    o_ref[...] = x + _rms(y, gpost_ref[...])


def cross_attention(x, gpre, wq, memkv, wo, gpost, *, tm):
    s = x.shape[0]
    const = lambda i: (0, 0)
    return pl.pallas_call(
        _xattn_kernel,
        grid=(s // tm,),
        in_specs=[
            pl.BlockSpec((tm, D_MODEL), lambda i: (i, 0)),
            pl.BlockSpec((1, D_MODEL), const),
            pl.BlockSpec((D_MODEL, X_WIDTH), const),
            pl.BlockSpec((N_MEM, 2 * X_WIDTH), const),
            pl.BlockSpec((X_WIDTH, D_MODEL), const),
            pl.BlockSpec((1, D_MODEL), const),
        ],
        out_specs=pl.BlockSpec((tm, D_MODEL), lambda i: (i, 0)),
        out_shape=jax.ShapeDtypeStruct((s, D_MODEL), F32),
        compiler_params=_cparams(("parallel",)),
        name="cross_attention",
    )(x, gpre.reshape(1, D_MODEL), wq, memkv, wo, gpost.reshape(1, D_MODEL))


def _ffn_kernel(x_ref, gpre_ref, wg_ref, wu_ref, wd_ref, gpost_ref, o_ref, h_ref, acc_ref):
    f = pl.program_id(1)

    @pl.when(f == 0)
    def _():
        h_ref[...] = _rms(x_ref[...], gpre_ref[...]).astype(BF16)
        acc_ref[...] = jnp.zeros_like(acc_ref)

    h = h_ref[...]
    a = _dot(h, wg_ref[...])
    b = _dot(h, wu_ref[...])
    t = (a * _sigmoid(a) * b).astype(BF16)
    acc_ref[...] += _dot(t, wd_ref[...])

    @pl.when(f == pl.num_programs(1) - 1)
    def _():
        o_ref[...] = x_ref[...] + _rms(acc_ref[...], gpost_ref[...])


def ffn(x, gpre, wg, wu, wd, gpost, *, tm, tf):
    s = x.shape[0]
    return pl.pallas_call(
        _ffn_kernel,
        grid=(s // tm, D_FF // tf),
        in_specs=[
            pl.BlockSpec((tm, D_MODEL), lambda i, f: (i, 0)),
            pl.BlockSpec((1, D_MODEL), lambda i, f: (0, 0)),
            pl.BlockSpec((D_MODEL, tf), lambda i, f: (0, f)),
            pl.BlockSpec((D_MODEL, tf), lambda i, f: (0, f)),
            pl.BlockSpec((tf, D_MODEL), lambda i, f: (f, 0)),
            pl.BlockSpec((1, D_MODEL), lambda i, f: (0, 0)),
        ],
        out_specs=pl.BlockSpec((tm, D_MODEL), lambda i, f: (i, 0)),
        out_shape=jax.ShapeDtypeStruct((s, D_MODEL), F32),
        scratch_shapes=[pltpu.VMEM((tm, D_MODEL), BF16), pltpu.VMEM((tm, D_MODEL), F32)],
        compiler_params=_cparams(("parallel", "arbitrary")),
        name="ffn_swiglu",
    )(x, gpre.reshape(1, D_MODEL), wg, wu, wd, gpost.reshape(1, D_MODEL))


def _tiles(s):
    big = s >= 4096
    return dict(
        tm_in=1024 if big else 256, tn_in=512,
        tk_sel=512 if big else 256,
        tm_pool=512 if big else 256,
        tm_merge=256,
        tm_x=512 if big else 256,
        tm_ffn=512 if big else 256, tf=512,
    )


def _pad_w_in(w_in):
    d = w_in.shape[0]
    n_gl = 3 * NSA_HEADS
    o_gl = NSA_WIDTH + 6 * KV_WIDTH
    o_u = o_gl + n_gl
    o_ml = o_u + POOL_WIDTH
    pad = jnp.zeros((d, GL_PAD - n_gl), w_in.dtype)
    return jnp.concatenate(
        [w_in[:, o_ml:], w_in[:, o_u:o_ml], w_in[:, :o_gl], w_in[:, o_gl:o_u], pad], axis=1).astype(BF16)


def _forward(x, mem, rel_bias, ln_mix_pre, ln_mix_post, ln_x_pre, ln_x_post, ln_mem,
             ln_ffn_pre, ln_ffn_post, w_in, cmp_pe, cmp_w1, cmp_w2, w_pool, pool_scale,
             w_br_attn, w_br_pool, w_mix_out, w_xq, w_xkv, w_xo, w_gate, w_up, w_down):
    b, s, _ = x.shape
    assert b == 1 and s % 1024 == 0
    depth = w_in.shape[0]
    tl = _tiles(s)
    nc = s // CMP_STRIDE
    nq = s // Q_BLK
    half = CMP_STRIDE * HEAD_DIM
    ts, wb, bc = bias_tables(rel_bias, s=s, tk=tl["tk_sel"])
    key_aug = jnp.asarray(_key_aug_lanes(s, tl["tk_sel"]))
    xc = x[0]
    mem2 = mem[0]
    for l in range(depth):
        z = norm_matmul(xc, ln_mix_pre[l], _pad_w_in(w_in[l]), tm=tl["tm_in"], tn=tl["tn_in"])
        q = z[:, OFF_Q:OFF_Q + NSA_WIDTH] * ((HEAD_DIM ** -0.5) * LOG2E)
        qt = q.reshape(nq, Q_BLK, NSA_GROUPS, NSA_HPG, HEAD_DIM).transpose(2, 0, 4, 3, 1)
        qt = qt.reshape(NSA_GROUPS, nq, HEAD_DIM, NSA_HPG * Q_BLK).astype(BF16)
        kv = z[:, OFF_KV:OFF_KV + 6 * KV_WIDTH].reshape(s, 6, NSA_GROUPS, HEAD_DIM)
        c = kv[:, 0:2].reshape(nc, CMP_STRIDE, 2, NSA_GROUPS, HEAD_DIM)
        c = c.transpose(2, 3, 0, 1, 4).reshape(2, NSA_GROUPS, nc, half)
        pe = cmp_pe[l].reshape(2, 2, 1, half)
        w1 = cmp_w1[l].reshape(2, 2, half, CMP_HID).astype(BF16)
        kvc = compress(c, pe, w1, cmp_w2[l].astype(BF16), s=s)
        kvc = jnp.concatenate([kvc[0], kvc[1]], axis=-1)
        kvc_pad = jnp.pad(kvc, ((0, 0), (nc - 8, 8), (0, 0)))
        ks = jnp.concatenate([kv[:, 2], jnp.broadcast_to(key_aug[:, None, :], (s, NSA_GROUPS, HEAD_DIM))],
                             axis=-1).transpose(1, 0, 2).astype(BF16)
        kw = jnp.concatenate([kv[:, 4], kv[:, 5]], axis=-1).transpose(1, 0, 2).astype(BF16)
        vst = kv[:, 3].transpose(1, 2, 0).astype(BF16)
        vwt = kv[:, 5].transpose(1, 2, 0).astype(BF16)
        glt = z[:, OFF_GL:OFF_GL + 3 * NSA_HEADS].reshape(s, NSA_GROUPS, 3 * NSA_HPG).transpose(1, 2, 0)
        a = nsa_attention(qt, kvc_pad, ks, vst, kw, vwt, glt, ts, wb, bc, s=s, tk=tl["tk_sel"])
        p = pool_mixer(z, w_pool[l].astype(BF16), pool_scale[l], tm=tl["tm_pool"])
        xc = merge_mix(a, p, z, xc, w_br_attn[l].astype(BF16), w_br_pool[l].astype(BF16),
                       w_mix_out[l].astype(BF16), ln_mix_post[l], tm=tl["tm_merge"])
        memkv = norm_matmul(mem2, ln_mem[l], w_xkv[l].astype(BF16), tm=N_MEM, tn=512, out_dtype=BF16)
        xc = cross_attention(xc, ln_x_pre[l], w_xq[l].astype(BF16), memkv, w_xo[l].astype(BF16),
                             ln_x_post[l], tm=tl["tm_x"])
        xc = ffn(xc, ln_ffn_pre[l], w_gate[l].astype(BF16), w_up[l].astype(BF16),
                 w_down[l].astype(BF16), ln_ffn_post[l], tm=tl["tm_ffn"], tf=tl["tf"])
    return xc[None]


def kernel(x, mem, rel_bias, ln_mix_pre, ln_mix_post, ln_x_pre, ln_x_post, ln_mem, ln_ffn_pre,
           ln_ffn_post, w_in, cmp_pe, cmp_w1, cmp_w2, w_pool, pool_scale, w_br_attn, w_br_pool,
           w_mix_out, w_xq, w_xkv, w_xo, w_gate, w_up, w_down):
    return _forward(x, mem, rel_bias, ln_mix_pre, ln_mix_post, ln_x_pre, ln_x_post, ln_mem,
                    ln_ffn_pre, ln_ffn_post, w_in, cmp_pe, cmp_w1, cmp_w2, w_pool, pool_scale,
                    w_br_attn, w_br_pool, w_mix_out, w_xq, w_xkv, w_xo, w_gate, w_up, w_down)
```

```python
import functools
import math

import numpy as np
import jax
import jax.numpy as jnp
from jax import lax
from jax.experimental import pallas as pl
from jax.experimental.pallas import tpu as pltpu

F32 = jnp.float32
BF16 = jnp.bfloat16

D_MODEL = 2048
DEPTH = 4
N_MEM = 256
NSA_HEADS = 16
NSA_GROUPS = 4
NSA_HPG = 4
HEAD_DIM = 64
NSA_WIDTH = 1024
KV_WIDTH = 256
CMP_LEN = 32
CMP_STRIDE = 16
CMP_HID = 128
SEL_BLK = 64
SEL_TOPN = 16
WIN = 512
Q_BLK = 128
FORCE_SCORE = 1e4
POOL_WINDOWS = (2, 4, 8, 16)
POOL_GW = 256
POOL_WIDTH = 1024
REL_BUCKETS = 32
REL_MAX_DIST = 2048
X_HEADS = 4
X_HEAD_DIM = 128
X_WIDTH = 512
D_FF = 5632
NEG = -1e30
LOG2E = 1.4426950408889634

GL_PAD = 512
OFF_ML, OFF_U, OFF_Q, OFF_KV, OFF_GL = 0, 4096, 5120, 6144, 7680
IN_PAD = 8192

VMEM_LIMIT_V7X = 56 * 1024 * 1024
LANES = 128


def _cparams(sem, flags=None):
    return pltpu.CompilerParams(dimension_semantics=sem, vmem_limit_bytes=VMEM_LIMIT_V7X, flags=flags)


def _rms(x, g):
    ms = jnp.mean(x * x, axis=-1, keepdims=True)
    return x * lax.rsqrt(ms + 1e-6) * g


def _sigmoid(x):
    return 1.0 / (1.0 + jnp.exp(-x))


def _dot(a, b):
    return jnp.dot(a, b, preferred_element_type=F32)


def _dot_nt(a, b):
    return lax.dot_general(a, b, (((1,), (1,)), ((), ())), preferred_element_type=F32)


def _norm_matmul_kernel(x_ref, g_ref, w_ref, o_ref, h_ref):
    @pl.when(pl.program_id(1) == 0)
    def _():
        h_ref[...] = _rms(x_ref[...], g_ref[...]).astype(BF16)

    o_ref[...] = _dot(h_ref[...], w_ref[...]).astype(o_ref.dtype)


def norm_matmul(x, g, w, l, *, tm, tn, out_dtype=F32):
    m, k = x.shape
    n = w.shape[2]
    return pl.pallas_call(
        _norm_matmul_kernel,
        grid=(m // tm, n // tn),
        in_specs=[
            pl.BlockSpec((tm, k), lambda i, j: (i, 0)),
            pl.BlockSpec((1, k), lambda i, j: (0, 0)),
            pl.BlockSpec((None, k, tn), lambda i, j: (l, 0, j)),
        ],
        out_specs=pl.BlockSpec((tm, tn), lambda i, j: (i, j)),
        out_shape=jax.ShapeDtypeStruct((m, n), out_dtype),
        scratch_shapes=[pltpu.VMEM((tm, k), BF16)],
        compiler_params=_cparams(("parallel", "arbitrary")),
        name="norm_matmul",
    )(x, g.reshape(1, k), w)


def _bucket(d):
    n = jnp.maximum(d, 0)
    exact = REL_BUCKETS // 2
    nf = jnp.maximum(n, 1).astype(F32)
    large = exact + (jnp.log(nf / exact) / math.log(REL_MAX_DIST / exact)
                     * (REL_BUCKETS - exact)).astype(jnp.int32)
    return jnp.where(n < exact, n, jnp.minimum(large, REL_BUCKETS - 1))


def _lookup(tab_ref, h, bk):
    out = jnp.full(bk.shape, tab_ref[h, 0], F32)
    for b in range(1, REL_BUCKETS):
        out = jnp.where(bk == b, tab_ref[h, b], out)
    return out


def _bias_tables_kernel(tab_ref, ts_ref, wb_ref, bc_ref, *, ls, off, lw, ncw, cmp_shift):
    h = pl.program_id(0)
    r = lax.broadcasted_iota(jnp.int32, (LANES, Q_BLK), 0)
    qi = lax.broadcasted_iota(jnp.int32, (LANES, Q_BLK), 1)

    def chunked(ref, n_rows, dist, valid):
        def body(ch, carry):
            d = dist(ch * LANES + r)
            val = _lookup(tab_ref, h, _bucket(d))
            ref[0, pl.ds(pl.multiple_of(ch * LANES, LANES), LANES), :] = jnp.where(valid(d), val * LOG2E, NEG)
            return carry
        lax.fori_loop(0, n_rows // LANES, body, 0)

    chunked(ts_ref, ls, lambda c: qi - c + off, lambda d: d >= 0)
    chunked(wb_ref, lw, lambda c: qi - c + WIN, lambda d: (d >= 0) & (d < WIN))
    chunked(bc_ref, ncw, lambda m: qi - CMP_STRIDE * m + cmp_shift, lambda d: d >= 0)


def _table_sizes(s, tk):
    off = REL_MAX_DIST + tk
    ls = off + 3 * tk - Q_BLK
    lw = 2 * WIN + Q_BLK
    return off, ls, lw


def bias_tables(rel_bias, *, s, tk):
    off, ls, lw = _table_sizes(s, tk)
    ncw = s // CMP_STRIDE
    cmp_shift = CMP_STRIDE * (ncw - 8) - (CMP_LEN - 1)
    kern = functools.partial(_bias_tables_kernel, ls=ls, off=off, lw=lw, ncw=ncw, cmp_shift=cmp_shift)
    return pl.pallas_call(
        kern,
        grid=(NSA_HEADS,),
        in_specs=[pl.BlockSpec(memory_space=pltpu.SMEM)],
        out_specs=[
            pl.BlockSpec((1, ls, Q_BLK), lambda h: (h, 0, 0)),
            pl.BlockSpec((1, lw, Q_BLK), lambda h: (h, 0, 0)),
            pl.BlockSpec((1, ncw, Q_BLK), lambda h: (h, 0, 0)),
        ],
        out_shape=[
            jax.ShapeDtypeStruct((NSA_HEADS, ls, Q_BLK), F32),
            jax.ShapeDtypeStruct((NSA_HEADS, lw, Q_BLK), F32),
            jax.ShapeDtypeStruct((NSA_HEADS, ncw, Q_BLK), F32),
        ],
        compiler_params=_cparams(("arbitrary",)),
        name="bias_tables",
    )(rel_bias)


def _gelu_tanh(x):
    return 0.5 * x * (1.0 + jnp.tanh(math.sqrt(2.0 / math.pi) * (x + 0.044715 * x * x * x)))


def _compress_kernel(c_ref, pe_ref, w1_ref, w2_ref, o_ref, scr_ref, *, nc):
    c = c_ref[0, 0]
    ca = (c + pe_ref[0, 0]).astype(BF16)
    cb = (c + pe_ref[0, 1]).astype(BF16)
    h1 = _dot(ca, w1_ref[0, 0])
    h2 = _dot(cb, w1_ref[0, 1])
    scr_ref[pl.ds(0, nc), :] = h2
    scr_ref[pl.ds(nc, 8), :] = jnp.zeros((8, CMP_HID), F32)
    hid = h1 + scr_ref[pl.ds(1, nc), :]
    out = _dot(_gelu_tanh(hid).astype(BF16), w2_ref[0])
    row = lax.broadcasted_iota(jnp.int32, out.shape, 0)
    o_ref[0, 0] = jnp.where(row < nc - 1, out, 0.0)


def compress(c, pe, w1, w2, l, *, s):
    nc = s // CMP_STRIDE
    half = CMP_STRIDE * HEAD_DIM
    return pl.pallas_call(
        functools.partial(_compress_kernel, nc=nc),
        grid=(2, NSA_GROUPS),
        in_specs=[
            pl.BlockSpec((1, 1, nc, half), lambda a, g: (a, g, 0, 0)),
            pl.BlockSpec((None, 1, 2, 1, half), lambda a, g: (l, a, 0, 0, 0)),
            pl.BlockSpec((None, 1, 2, half, CMP_HID), lambda a, g: (l, a, 0, 0, 0)),
            pl.BlockSpec((None, 1, CMP_HID, HEAD_DIM), lambda a, g: (l, a, 0, 0)),
        ],
        out_specs=pl.BlockSpec((1, 1, nc, HEAD_DIM), lambda a, g: (a, g, 0, 0)),
        out_shape=jax.ShapeDtypeStruct((2, NSA_GROUPS, nc, HEAD_DIM), F32),
        scratch_shapes=[pltpu.VMEM((nc + 8, CMP_HID), F32)],
        compiler_params=_cparams(("arbitrary", "arbitrary")),
        name="compress_kv",
    )(c, pe, w1, w2)


V_ROWS = HEAD_DIM + 16


def _attend(s, vt_aug):
    m = jnp.max(s, axis=0, keepdims=True)
    e = jnp.exp2(s - m)
    oa = _dot(vt_aug, e.astype(BF16))
    inv = jnp.where(m > 0.5 * NEG, 1.0 / oa[HEAD_DIM:HEAD_DIM + 1, :], 0.0)
    return e, oa[:HEAD_DIM, :] * inv, inv


def _lanes4(fn):
    return jnp.concatenate([fn(hh) for hh in range(NSA_HPG)], axis=-1)


def _nsa_kernel(q_ref, kvc_ref, ks_ref, vs_ref, kw_ref, vw_ref, gl_ref, ts_ref, wb_ref, bc_ref,
                ov_ref, o_ref, mask_ref, sa_ref, sb_ref, *, tk, ncw, nsel, off):
    i = pl.program_id(1)
    q0 = i * Q_BLK
    rows = NSA_HPG * Q_BLK
    qt = q_ref[0, 0]
    qa = jnp.concatenate([qt, jnp.zeros_like(qt)], axis=0)

    kvc = kvc_ref[0, pl.ds(pl.multiple_of(8 * i, 8), ncw), :]
    ones_rows = (lax.broadcasted_iota(jnp.int32, (V_ROWS - HEAD_DIM, ncw), 0) == 0).astype(BF16)
    vct = jnp.concatenate([kvc.T[HEAD_DIM:, :].astype(BF16), ones_rows], axis=0)
    s = _dot(kvc.astype(BF16), qa) + _lanes4(lambda hh: bc_ref[hh])
    m_row = lax.broadcasted_iota(jnp.int32, (ncw, 1), 0)
    s = jnp.where(m_row >= (ncw - 8) - 8 * i, s, NEG)
    e, o_c, inv = _attend(s, vct)

    pn = [e[:, hh * Q_BLK:(hh + 1) * Q_BLK] * inv[:, hh * Q_BLK:(hh + 1) * Q_BLK] for hh in range(NSA_HPG)]
    psum = (pn[0] + pn[1]) + (pn[2] + pn[3])
    p_hi = psum.astype(BF16)
    p_lo = (psum - p_hi.astype(F32)).astype(BF16)
    imp = _dot(ov_ref[...], p_hi) + _dot(ov_ref[...], p_lo)

    jp = lax.broadcasted_iota(jnp.int32, (nsel, Q_BLK), 0)
    jabs = jp + (2 * i - (nsel - 2))
    cur = 2 * i + (lax.broadcasted_iota(jnp.int32, (nsel, Q_BLK), 1) >= SEL_BLK).astype(jnp.int32)
    forced = (jabs == 0) | (jabs == cur) | (jabs == cur - 1)
    val = jnp.where(forced, FORCE_SCORE, imp)
    val = jnp.where(jabs > cur, -1.0, val)
    val = jnp.where(jabs < 0, -3.0, val)
    work = val
    chosen = jnp.zeros((nsel, Q_BLK), F32)
    jpf = jp.astype(F32)
    for _ in range(SEL_TOPN):
        mx = jnp.max(work, axis=0, keepdims=True)
        jmin = jnp.min(jnp.where(work == mx, jpf, 2.0 * nsel), axis=0, keepdims=True)
        hit = jpf == jmin
        chosen = jnp.where(hit, 1.0, chosen)
        work = jnp.where(hit, -1e38, work)
    sel = jnp.where(val >= 0.0, chosen, 0.0).astype(BF16)
    a_row = lax.broadcasted_iota(jnp.int32, (nsel, nsel), 0)
    s_col = lax.broadcasted_iota(jnp.int32, (nsel, nsel), 1)
    shift = jnp.where(s_col == a_row - (2 * i - (nsel - 2)), 1.0, 0.0).astype(BF16)
    mask_abs = (_dot(shift, sel) - 1.0) * (-NEG)
    mask_ref[...] = jnp.concatenate([mask_abs] * NSA_HPG, axis=-1)

    wk = WIN + Q_BLK
    start = pl.multiple_of(jnp.maximum(q0 - WIN, 0), Q_BLK)
    kvw = kw_ref[0, pl.ds(start, wk), :]
    vwt = vw_ref[0, :, pl.ds(start, wk)]
    wb0 = pl.multiple_of(WIN - (q0 - start), Q_BLK)
    s = _dot(kvw, qa) + _lanes4(lambda hh: wb_ref[hh, pl.ds(wb0, wk), :])
    _, o_w, _ = _attend(s, vwt)

    nblk = tk // SEL_BLK
    far_bias = _lanes4(lambda hh: ts_ref[hh, 0:1, :])
    far_hi = far_bias.astype(BF16).astype(F32)
    far_lo = (far_bias - far_hi).astype(BF16).astype(F32)
    far_lo2 = (far_bias - far_hi - far_lo).astype(BF16).astype(F32)
    far_rows = jnp.concatenate([far_hi, far_lo, far_lo2, jnp.zeros((8 - FAR_TERMS, rows), F32)], axis=0)
    n_pad = AUG_ROWS - nblk - 8
    aug_pad = [jnp.zeros((n_pad, rows), F32)] if n_pad else []
    rhs_tail = jnp.zeros((LANES - HEAD_DIM - AUG_ROWS, rows), BF16)
    n_tiles = (q0 + Q_BLK + tk - 1) // tk
    n_pairs = (n_tiles + 1) // 2
    n_far_pairs = (jnp.maximum(q0 - off + tk, 0) // tk) // (2 * FAR_PAIRS_PER_ITER) * FAR_PAIRS_PER_ITER
    last_tile = ks_ref.shape[1] // tk - 1

    def scores_into(buf, kt):
        kt = jnp.minimum(kt, last_tile)
        k0 = pl.multiple_of(kt * tk, tk)
        slab = mask_ref[pl.ds(pl.multiple_of(kt * nblk, nblk), nblk), :]
        far = jnp.where(kt < 2 * n_far_pairs, far_rows, 0.0)
        aug = jnp.concatenate([slab, far] + aug_pad, axis=0).astype(BF16)
        rhs = jnp.concatenate([qt, aug, rhs_tail], axis=0)
        buf[...] = _dot(ks_ref[0, pl.ds(k0, tk), :], rhs)

    def sel_tile(buf, kt, carry, near):
        m, acc = carry
        k0 = pl.multiple_of(kt * tk, tk)
        s = buf[...]
        vt = vs_ref[0, :, pl.ds(k0, tk)]
        if near:
            cb = pl.multiple_of(jnp.maximum(off - (q0 - k0), 0), Q_BLK)
            s = s + _lanes4(lambda hh: ts_ref[hh, pl.ds(cb, tk), :])
        m_new = jnp.maximum(m, jnp.max(s, axis=0, keepdims=True))
        p = jnp.exp2(s - m_new)
        acc = jnp.exp2(m - m_new) * acc + _dot(vt, p.astype(BF16))
        return m_new, acc

    def sel_pairs(it, carry, near, pairs):
        for j in range(pairs):
            kt = 2 * (pairs * it + j)
            scores_into(sb_ref, kt + 1)
            carry = sel_tile(sa_ref, kt, carry, near)
            scores_into(sa_ref, kt + 2)
            carry = sel_tile(sb_ref, kt + 1, carry, near)
        return carry

    scores_into(sa_ref, 0)
    init = (jnp.full((1, rows), NEG, F32), jnp.zeros((V_ROWS, rows), F32))
    carry = lax.fori_loop(0, n_far_pairs // FAR_PAIRS_PER_ITER,
                          functools.partial(sel_pairs, near=False, pairs=FAR_PAIRS_PER_ITER), init)
    _, acc_s = lax.fori_loop(n_far_pairs, n_pairs, functools.partial(sel_pairs, near=True, pairs=1), carry)
    o_s = acc_s[:HEAD_DIM, :] * (1.0 / acc_s[HEAD_DIM:HEAD_DIM + 1, :])

    gates = _sigmoid(gl_ref[0])
    def gate_row(c):
        return _lanes4(lambda hh: gates[3 * hh + c:3 * hh + c + 1, :])
    o = gate_row(0) * o_c + gate_row(1) * o_s + gate_row(2) * o_w
    o = jnp.concatenate([o[:, hh * Q_BLK:(hh + 1) * Q_BLK] for hh in range(NSA_HPG)], axis=0)
    o_ref[...] = o.T.astype(o_ref.dtype)


FAR_PAIRS_PER_ITER = 2
FAR_TERMS = 3
AUG_ROWS = 16


def _key_aug_lanes(s, tk):
    nblk = tk // SEL_BLK
    aug = np.zeros((s, HEAD_DIM), np.float32)
    aug[np.arange(s), (np.arange(s) % tk) // SEL_BLK] = 1.0
    aug[:, nblk:nblk + FAR_TERMS] = 1.0
    return aug


def _overlap_matrix_t(ncw, nsel):
    m = np.arange(ncw)[None, :]
    j = np.arange(nsel)[:, None]
    lo = np.maximum(m * CMP_STRIDE, j * SEL_BLK)
    hi = np.minimum(m * CMP_STRIDE + CMP_LEN, (j + 1) * SEL_BLK)
    return (np.maximum(hi - lo, 0).astype(np.float32) / CMP_LEN)


def nsa_attention(qt, kvc_pad, ks, vst, kw, vwt, glt, ts, wb, bc, *, s, tk):
    ncw = s // CMP_STRIDE
    nsel = s // SEL_BLK
    off, ls, lw = _table_sizes(s, tk)
    rows = NSA_HPG * Q_BLK
    ov = jnp.asarray(_overlap_matrix_t(ncw, nsel), BF16)
    kern = functools.partial(_nsa_kernel, tk=tk, ncw=ncw, nsel=nsel, off=off)
    per_group = dict(pipeline_mode=pl.Buffered(1))
    return pl.pallas_call(
        kern,
        grid=(NSA_GROUPS, s // Q_BLK),
        in_specs=[
            pl.BlockSpec((1, 1, HEAD_DIM, rows), lambda g, i: (g, i, 0, 0)),
            pl.BlockSpec((1, 2 * ncw, LANES), lambda g, i: (g, 0, 0), **per_group),
            pl.BlockSpec((1, s, LANES), lambda g, i: (g, 0, 0), **per_group),
            pl.BlockSpec((1, V_ROWS, s), lambda g, i: (g, 0, 0), **per_group),
            pl.BlockSpec((1, s, LANES), lambda g, i: (g, 0, 0), **per_group),
            pl.BlockSpec((1, V_ROWS, s), lambda g, i: (g, 0, 0), **per_group),
            pl.BlockSpec((1, 3 * NSA_HPG, Q_BLK), lambda g, i: (g, 0, i)),
            pl.BlockSpec((NSA_HPG, ls, Q_BLK), lambda g, i: (g, 0, 0), **per_group),
            pl.BlockSpec((NSA_HPG, lw, Q_BLK), lambda g, i: (g, 0, 0), **per_group),
            pl.BlockSpec((NSA_HPG, ncw, Q_BLK), lambda g, i: (g, 0, 0), **per_group),
            pl.BlockSpec((nsel, ncw), lambda g, i: (0, 0), **per_group),
        ],
        out_specs=pl.BlockSpec((Q_BLK, NSA_HPG * HEAD_DIM), lambda g, i: (i, g)),
        out_shape=jax.ShapeDtypeStruct((s, NSA_WIDTH), BF16),
        scratch_shapes=[pltpu.VMEM((nsel, rows), F32), pltpu.VMEM((tk, rows), F32),
                        pltpu.VMEM((tk, rows), F32)],
        compiler_params=_cparams(("arbitrary", "arbitrary")),
        name="nsa_attention",
    )(qt, kvc_pad, ks, vst, kw, vwt, glt, ts, wb, bc, ov)


HALO = 16


def _pool_kernel(u_ref, halo_ref, w_ref, sc_ref, o_ref, ext_ref, *, tm):
    i = pl.program_id(0)
    halo = jnp.where(i > 0, halo_ref[...], 0.0)
    ext_ref[pl.ds(0, HALO), :] = halo
    ext_ref[pl.ds(HALO, tm), :] = u_ref[...]
    t = i * tm + lax.broadcasted_iota(jnp.int32, (tm, 1), 0)
    outs = []
    for gi, w in enumerate(POOL_WINDOWS):
        cols = pl.ds(gi * POOL_GW, POOL_GW)
        x = ext_ref[pl.ds(HALO, tm), cols]
        acc = x
        for k in range(1, w):
            acc = acc + ext_ref[pl.ds(HALO - k, tm), cols]
        cnt = jnp.minimum(t + 1, w).astype(F32)
        y = acc / cnt - x
        outs.append(_dot(y.astype(BF16), w_ref[gi]))
    o_ref[...] = (jnp.concatenate(outs, axis=-1) * sc_ref[...]).astype(o_ref.dtype)


def pool_mixer(z, w_pool, pool_scale, l, *, tm):
    s = z.shape[0]
    ucol = OFF_U // POOL_WIDTH
    return pl.pallas_call(
        functools.partial(_pool_kernel, tm=tm),
        grid=(s // tm,),
        in_specs=[
            pl.BlockSpec((tm, POOL_WIDTH), lambda i: (i, ucol)),
            pl.BlockSpec((HALO, POOL_WIDTH), lambda i: (jnp.maximum(i * (tm // HALO) - 1, 0), ucol)),
            pl.BlockSpec((None, len(POOL_WINDOWS), POOL_GW, POOL_GW), lambda i: (l, 0, 0, 0)),
            pl.BlockSpec((1, POOL_WIDTH), lambda i: (0, 0)),
        ],
        out_specs=pl.BlockSpec((tm, POOL_WIDTH), lambda i: (i, 0)),
        out_shape=jax.ShapeDtypeStruct((s, POOL_WIDTH), BF16),
        scratch_shapes=[pltpu.VMEM((HALO + tm, POOL_WIDTH), F32)],
        compiler_params=_cparams(("parallel",)),
        name="pool_mixer",
    )(z, z, w_pool, pool_scale.reshape(1, POOL_WIDTH))


def _merge_kernel(a_ref, p_ref, ml_ref, x_ref, wa_ref, wp_ref, wo_ref, g_ref, o_ref):
    pa = _dot(a_ref[...], wa_ref[...])
    pp = _dot(p_ref[...], wp_ref[...])
    ml = ml_ref[...]
    y = _sigmoid(ml[:, :D_MODEL]) * pa + _sigmoid(ml[:, D_MODEL:]) * pp
    y = _dot(y.astype(BF16), wo_ref[...])
    o_ref[...] = x_ref[...] + _rms(y, g_ref[...])


def merge_mix(a, p, z, x, wa, wp, wo, g, l, *, tm):
    s = x.shape[0]
    const = lambda i: (0, 0)
    layer = lambda i: (l, 0, 0)
    return pl.pallas_call(
        _merge_kernel,
        grid=(s // tm,),
        in_specs=[
            pl.BlockSpec((tm, NSA_WIDTH), lambda i: (i, 0)),
            pl.BlockSpec((tm, POOL_WIDTH), lambda i: (i, 0)),
            pl.BlockSpec((tm, 2 * D_MODEL), lambda i: (i, OFF_ML // (2 * D_MODEL))),
            pl.BlockSpec((tm, D_MODEL), lambda i: (i, 0)),
            pl.BlockSpec((None, NSA_WIDTH, D_MODEL), layer, pipeline_mode=pl.Buffered(1)),
            pl.BlockSpec((None, POOL_WIDTH, D_MODEL), layer, pipeline_mode=pl.Buffered(1)),
            pl.BlockSpec((None, D_MODEL, D_MODEL), layer, pipeline_mode=pl.Buffered(1)),
            pl.BlockSpec((1, D_MODEL), const),
        ],
        out_specs=pl.BlockSpec((tm, D_MODEL), lambda i: (i, 0)),
        out_shape=jax.ShapeDtypeStruct((s, D_MODEL), F32),
        compiler_params=_cparams(("parallel",)),
        name="merge_mix",
    )(a, p, z, x, wa, wp, wo, g.reshape(1, D_MODEL))


def _xattn_kernel(x_ref, gpre_ref, wq_ref, kv_ref, wo_ref, gpost_ref, o_ref):
    x = x_ref[...]
    h = _rms(x, gpre_ref[...]).astype(BF16)
    q = _dot(h, wq_ref[...]).astype(BF16)
    kv = kv_ref[...]
    outs = []
    for hh in range(X_HEADS):
        lo = hh * X_HEAD_DIM
        sc = _dot_nt(q[:, lo:lo + X_HEAD_DIM], kv[:, lo:lo + X_HEAD_DIM]) * (X_HEAD_DIM ** -0.5)
        m = jnp.max(sc, axis=-1, keepdims=True)
        e = jnp.exp(sc - m)
        p = (e / jnp.sum(e, axis=-1, keepdims=True)).astype(BF16)
        outs.append(_dot(p, kv[:, X_WIDTH + lo:X_WIDTH + lo + X_HEAD_DIM]))
    o = jnp.concatenate(outs, axis=-1).astype(BF16)
    y = _dot(o, wo_ref[...])
    o_ref[...] = x + _rms(y, gpost_ref[...])


def cross_attention(x, gpre, wq, memkv, wo, gpost, l, *, tm):
    s = x.shape[0]
    const = lambda i: (0, 0)
    layer = lambda i: (l, 0, 0)
    return pl.pallas_call(
        _xattn_kernel,
        grid=(s // tm,),
        in_specs=[
            pl.BlockSpec((tm, D_MODEL), lambda i: (i, 0)),
            pl.BlockSpec((1, D_MODEL), const),
            pl.BlockSpec((None, D_MODEL, X_WIDTH), layer),
            pl.BlockSpec((N_MEM, 2 * X_WIDTH), const),
            pl.BlockSpec((None, X_WIDTH, D_MODEL), layer),
            pl.BlockSpec((1, D_MODEL), const),
        ],
        out_specs=pl.BlockSpec((tm, D_MODEL), lambda i: (i, 0)),
        out_shape=jax.ShapeDtypeStruct((s, D_MODEL), F32),
        compiler_params=_cparams(("parallel",)),
        name="cross_attention",
    )(x, gpre.reshape(1, D_MODEL), wq, memkv, wo, gpost.reshape(1, D_MODEL))


def _ffn_kernel(x_ref, gpre_ref, wg_ref, wu_ref, wd_ref, gpost_ref, o_ref, h_ref, acc_ref):
    f = pl.program_id(1)

    @pl.when(f == 0)
    def _():
        h_ref[...] = _rms(x_ref[...], gpre_ref[...]).astype(BF16)
        acc_ref[...] = jnp.zeros_like(acc_ref)

    h = h_ref[...]
    a = _dot(h, wg_ref[...])
    b = _dot(h, wu_ref[...])
    t = (a * _sigmoid(a) * b).astype(BF16)
    acc_ref[...] += _dot(t, wd_ref[...])

    @pl.when(f == pl.num_programs(1) - 1)
    def _():
        o_ref[...] = x_ref[...] + _rms(acc_ref[...], gpost_ref[...])


def ffn(x, gpre, wg, wu, wd, gpost, l, *, tm, tf):
    s = x.shape[0]
    return pl.pallas_call(
        _ffn_kernel,
        grid=(s // tm, D_FF // tf),
        in_specs=[
            pl.BlockSpec((tm, D_MODEL), lambda i, f: (i, 0)),
            pl.BlockSpec((1, D_MODEL), lambda i, f: (0, 0)),
            pl.BlockSpec((None, D_MODEL, tf), lambda i, f: (l, 0, f)),
            pl.BlockSpec((None, D_MODEL, tf), lambda i, f: (l, 0, f)),
            pl.BlockSpec((None, tf, D_MODEL), lambda i, f: (l, f, 0)),
            pl.BlockSpec((1, D_MODEL), lambda i, f: (0, 0)),
        ],
        out_specs=pl.BlockSpec((tm, D_MODEL), lambda i, f: (i, 0)),
        out_shape=jax.ShapeDtypeStruct((s, D_MODEL), F32),
        scratch_shapes=[pltpu.VMEM((tm, D_MODEL), BF16), pltpu.VMEM((tm, D_MODEL), F32)],
        compiler_params=_cparams(("parallel", "arbitrary")),
        name="ffn_swiglu",
    )(x, gpre.reshape(1, D_MODEL), wg, wu, wd, gpost.reshape(1, D_MODEL))


def _tiles(s):
    big = s >= 4096
    return dict(
        tm_in=1024 if big else 256, tn_in=512,
        tk_sel=512 if big else 256,
        tm_pool=512 if big else 256,
        tm_merge=256,
        tm_x=512 if big else 256,
        tm_ffn=512 if big else 256, tf=512,
    )


def _pad_w_in(w_in):
    w = w_in.astype(BF16)
    n_gl = 3 * NSA_HEADS
    o_gl = NSA_WIDTH + 6 * KV_WIDTH
    o_u = o_gl + n_gl
    o_ml = o_u + POOL_WIDTH
    pad = jnp.zeros(w.shape[:2] + (GL_PAD - n_gl,), BF16)
    return jnp.concatenate(
        [w[..., o_ml:], w[..., o_u:o_ml], w[..., :o_gl], w[..., o_gl:o_u], pad], axis=-1)


def _forward(x, mem, rel_bias, ln_mix_pre, ln_mix_post, ln_x_pre, ln_x_post, ln_mem,
             ln_ffn_pre, ln_ffn_post, w_in, cmp_pe, cmp_w1, cmp_w2, w_pool, pool_scale,
             w_br_attn, w_br_pool, w_mix_out, w_xq, w_xkv, w_xo, w_gate, w_up, w_down):
    b, s, _ = x.shape
    assert b == 1 and s % 1024 == 0
    depth = w_in.shape[0]
    tl = _tiles(s)
    nc = s // CMP_STRIDE
    nq = s // Q_BLK
    half = CMP_STRIDE * HEAD_DIM
    ts, wb, bc = bias_tables(rel_bias, s=s, tk=tl["tk_sel"])
    key_aug = jnp.asarray(_key_aug_lanes(s, tl["tk_sel"]))
    v_ones = jnp.zeros((NSA_GROUPS, V_ROWS - HEAD_DIM, s), BF16).at[:, 0, :].set(1.0)
    w_in_b = _pad_w_in(w_in)
    pe_b = cmp_pe.reshape(depth, 2, 2, 1, half)
    w1_b = cmp_w1.reshape(depth, 2, 2, half, CMP_HID).astype(BF16)
    w2_b, w_pool_b = cmp_w2.astype(BF16), w_pool.astype(BF16)
    wa_b, wp_b, wo_b = w_br_attn.astype(BF16), w_br_pool.astype(BF16), w_mix_out.astype(BF16)
    wxq_b, wxkv_b, wxo_b = w_xq.astype(BF16), w_xkv.astype(BF16), w_xo.astype(BF16)
    wg_b, wu_b, wd_b = w_gate.astype(BF16), w_up.astype(BF16), w_down.astype(BF16)
    xc = x[0]
    mem2 = mem[0]
    for l in range(depth):
        z = norm_matmul(xc, ln_mix_pre[l], w_in_b, l, tm=tl["tm_in"], tn=tl["tn_in"])
        q = z[:, OFF_Q:OFF_Q + NSA_WIDTH] * ((HEAD_DIM ** -0.5) * LOG2E)
        qt = q.reshape(nq, Q_BLK, NSA_GROUPS, NSA_HPG, HEAD_DIM).transpose(2, 0, 4, 3, 1)
        qt = qt.reshape(NSA_GROUPS, nq, HEAD_DIM, NSA_HPG * Q_BLK).astype(BF16)
        kv = z[:, OFF_KV:OFF_KV + 6 * KV_WIDTH].reshape(s, 6, NSA_GROUPS, HEAD_DIM)
        c = kv[:, 0:2].reshape(nc, CMP_STRIDE, 2, NSA_GROUPS, HEAD_DIM)
        c = c.transpose(2, 3, 0, 1, 4).reshape(2, NSA_GROUPS, nc, half)
        kvc = compress(c, pe_b, w1_b, w2_b, l, s=s)
        kvc = jnp.concatenate([kvc[0], kvc[1]], axis=-1)
        kvc_pad = jnp.pad(kvc, ((0, 0), (nc - 8, 8), (0, 0)))
        ks = jnp.concatenate([kv[:, 2], jnp.broadcast_to(key_aug[:, None, :], (s, NSA_GROUPS, HEAD_DIM))],
                             axis=-1).transpose(1, 0, 2).astype(BF16)
        kw = jnp.concatenate([kv[:, 4], kv[:, 5]], axis=-1).transpose(1, 0, 2).astype(BF16)
        vst = jnp.concatenate([kv[:, 3].transpose(1, 2, 0).astype(BF16), v_ones], axis=1)
        vwt = jnp.concatenate([kv[:, 5].transpose(1, 2, 0).astype(BF16), v_ones], axis=1)
        glt = z[:, OFF_GL:OFF_GL + 3 * NSA_HEADS].reshape(s, NSA_GROUPS, 3 * NSA_HPG).transpose(1, 2, 0)
        a = nsa_attention(qt, kvc_pad, ks, vst, kw, vwt, glt, ts, wb, bc, s=s, tk=tl["tk_sel"])
        p = pool_mixer(z, w_pool_b, pool_scale[l], l, tm=tl["tm_pool"])
        xc = merge_mix(a, p, z, xc, wa_b, wp_b, wo_b, ln_mix_post[l], l, tm=tl["tm_merge"])
        memkv = norm_matmul(mem2, ln_mem[l], wxkv_b, l, tm=N_MEM, tn=512, out_dtype=BF16)
        xc = cross_attention(xc, ln_x_pre[l], wxq_b, memkv, wxo_b, ln_x_post[l], l, tm=tl["tm_x"])
        xc = ffn(xc, ln_ffn_pre[l], wg_b, wu_b, wd_b, ln_ffn_post[l], l, tm=tl["tm_ffn"], tf=tl["tf"])
    return xc[None]


def kernel(x, mem, rel_bias, ln_mix_pre, ln_mix_post, ln_x_pre, ln_x_post, ln_mem, ln_ffn_pre,
           ln_ffn_post, w_in, cmp_pe, cmp_w1, cmp_w2, w_pool, pool_scale, w_br_attn, w_br_pool,
           w_mix_out, w_xq, w_xkv, w_xo, w_gate, w_up, w_down):
    return _forward(x, mem, rel_bias, ln_mix_pre, ln_mix_post, ln_x_pre, ln_x_post, ln_mem,
                    ln_ffn_pre, ln_ffn_post, w_in, cmp_pe, cmp_w1, cmp_w2, w_pool, pool_scale,
                    w_br_attn, w_br_pool, w_mix_out, w_xq, w_xkv, w_xo, w_gate, w_up, w_down)
```

```python
import functools
import math

import numpy as np
import jax
import jax.numpy as jnp
from jax import lax
from jax.experimental import pallas as pl
from jax.experimental.pallas import tpu as pltpu

F32 = jnp.float32
BF16 = jnp.bfloat16

D_MODEL = 2048
DEPTH = 4
N_MEM = 256
NSA_HEADS = 16
NSA_GROUPS = 4
NSA_HPG = 4
HEAD_DIM = 64
NSA_WIDTH = 1024
KV_WIDTH = 256
CMP_LEN = 32
CMP_STRIDE = 16
CMP_HID = 128
SEL_BLK = 64
SEL_TOPN = 16
WIN = 512
Q_BLK = 128
FORCE_SCORE = 1e4
POOL_WINDOWS = (2, 4, 8, 16)
POOL_GW = 256
POOL_WIDTH = 1024
REL_BUCKETS = 32
REL_MAX_DIST = 2048
X_HEADS = 4
X_HEAD_DIM = 128
X_WIDTH = 512
D_FF = 5632
NEG = -1e30
LOG2E = 1.4426950408889634

GL_PAD = 512
OFF_ML, OFF_U, OFF_Q, OFF_KV, OFF_GL = 0, 4096, 5120, 6144, 7680
IN_PAD = 8192

VMEM_LIMIT_V7X = 56 * 1024 * 1024
LANES = 128


def _cparams(sem, flags=None):
    return pltpu.CompilerParams(dimension_semantics=sem, vmem_limit_bytes=VMEM_LIMIT_V7X, flags=flags)


def _rms(x, g):
    ms = jnp.mean(x * x, axis=-1, keepdims=True)
    return x * lax.rsqrt(ms + 1e-6) * g


def _sigmoid(x):
    return 1.0 / (1.0 + jnp.exp(-x))


def _dot(a, b):
    return jnp.dot(a, b, preferred_element_type=F32)


def _dot_nt(a, b):
    return lax.dot_general(a, b, (((1,), (1,)), ((), ())), preferred_element_type=F32)


def _norm_matmul_kernel(x_ref, g_ref, w_ref, o_ref, h_ref):
    @pl.when(pl.program_id(1) == 0)
    def _():
        h_ref[...] = _rms(x_ref[...], g_ref[...]).astype(BF16)

    o_ref[...] = _dot(h_ref[...], w_ref[...]).astype(o_ref.dtype)


def norm_matmul(x, g, w, l, *, tm, tn, out_dtype=F32):
    m, k = x.shape
    n = w.shape[2]
    return pl.pallas_call(
        _norm_matmul_kernel,
        grid=(m // tm, n // tn),
        in_specs=[
            pl.BlockSpec((tm, k), lambda i, j: (i, 0)),
            pl.BlockSpec((1, k), lambda i, j: (0, 0)),
            pl.BlockSpec((None, k, tn), lambda i, j: (l, 0, j)),
        ],
        out_specs=pl.BlockSpec((tm, tn), lambda i, j: (i, j)),
        out_shape=jax.ShapeDtypeStruct((m, n), out_dtype),
        scratch_shapes=[pltpu.VMEM((tm, k), BF16)],
        compiler_params=_cparams(("parallel", "arbitrary")),
        name="norm_matmul",
    )(x, g.reshape(1, k), w)


def _bucket(d):
    n = jnp.maximum(d, 0)
    exact = REL_BUCKETS // 2
    nf = jnp.maximum(n, 1).astype(F32)
    large = exact + (jnp.log(nf / exact) / math.log(REL_MAX_DIST / exact)
                     * (REL_BUCKETS - exact)).astype(jnp.int32)
    return jnp.where(n < exact, n, jnp.minimum(large, REL_BUCKETS - 1))


def _lookup(tab_ref, h, bk):
    out = jnp.full(bk.shape, tab_ref[h, 0], F32)
    for b in range(1, REL_BUCKETS):
        out = jnp.where(bk == b, tab_ref[h, b], out)
    return out


def _bias_tables_kernel(tab_ref, ts_ref, wb_ref, bc_ref, *, ls, off, lw, ncw, cmp_shift):
    h = pl.program_id(0)
    r = lax.broadcasted_iota(jnp.int32, (LANES, Q_BLK), 0)
    qi = lax.broadcasted_iota(jnp.int32, (LANES, Q_BLK), 1)

    def chunked(ref, n_rows, dist, valid):
        def body(ch, carry):
            d = dist(ch * LANES + r)
            val = _lookup(tab_ref, h, _bucket(d))
            ref[0, pl.ds(pl.multiple_of(ch * LANES, LANES), LANES), :] = jnp.where(valid(d), val * LOG2E, NEG)
            return carry
        lax.fori_loop(0, n_rows // LANES, body, 0)

    chunked(ts_ref, ls, lambda c: qi - c + off, lambda d: d >= 0)
    chunked(wb_ref, lw, lambda c: qi - c + WIN, lambda d: (d >= 0) & (d < WIN))
    chunked(bc_ref, ncw, lambda m: qi - CMP_STRIDE * m + cmp_shift, lambda d: d >= 0)


def _table_sizes(s, tk):
    off = REL_MAX_DIST + tk
    ls = off + 3 * tk - Q_BLK
    lw = 2 * WIN + Q_BLK
    return off, ls, lw


def bias_tables(rel_bias, *, s, tk):
    off, ls, lw = _table_sizes(s, tk)
    ncw = s // CMP_STRIDE
    cmp_shift = CMP_STRIDE * (ncw - 8) - (CMP_LEN - 1)
    kern = functools.partial(_bias_tables_kernel, ls=ls, off=off, lw=lw, ncw=ncw, cmp_shift=cmp_shift)
    return pl.pallas_call(
        kern,
        grid=(NSA_HEADS,),
        in_specs=[pl.BlockSpec(memory_space=pltpu.SMEM)],
        out_specs=[
            pl.BlockSpec((1, ls, Q_BLK), lambda h: (h, 0, 0)),
            pl.BlockSpec((1, lw, Q_BLK), lambda h: (h, 0, 0)),
            pl.BlockSpec((1, ncw, Q_BLK), lambda h: (h, 0, 0)),
        ],
        out_shape=[
            jax.ShapeDtypeStruct((NSA_HEADS, ls, Q_BLK), F32),
            jax.ShapeDtypeStruct((NSA_HEADS, lw, Q_BLK), F32),
            jax.ShapeDtypeStruct((NSA_HEADS, ncw, Q_BLK), F32),
        ],
        compiler_params=_cparams(("arbitrary",)),
        name="bias_tables",
    )(rel_bias)


def _gelu_tanh(x):
    return 0.5 * x * (1.0 + jnp.tanh(math.sqrt(2.0 / math.pi) * (x + 0.044715 * x * x * x)))


def _compress_kernel(c_ref, pe_ref, w1_ref, w2_ref, o_ref, scr_ref, *, nc):
    c = c_ref[0, 0]
    ca = (c + pe_ref[0, 0]).astype(BF16)
    cb = (c + pe_ref[0, 1]).astype(BF16)
    h1 = _dot(ca, w1_ref[0, 0])
    h2 = _dot(cb, w1_ref[0, 1])
    scr_ref[pl.ds(0, nc), :] = h2
    scr_ref[pl.ds(nc, 8), :] = jnp.zeros((8, CMP_HID), F32)
    hid = h1 + scr_ref[pl.ds(1, nc), :]
    out = _dot(_gelu_tanh(hid).astype(BF16), w2_ref[0])
    row = lax.broadcasted_iota(jnp.int32, out.shape, 0)
    o_ref[0, 0] = jnp.where(row < nc - 1, out, 0.0)


def compress(c, pe, w1, w2, l, *, s):
    nc = s // CMP_STRIDE
    half = CMP_STRIDE * HEAD_DIM
    return pl.pallas_call(
        functools.partial(_compress_kernel, nc=nc),
        grid=(2, NSA_GROUPS),
        in_specs=[
            pl.BlockSpec((1, 1, nc, half), lambda a, g: (a, g, 0, 0)),
            pl.BlockSpec((None, 1, 2, 1, half), lambda a, g: (l, a, 0, 0, 0)),
            pl.BlockSpec((None, 1, 2, half, CMP_HID), lambda a, g: (l, a, 0, 0, 0)),
            pl.BlockSpec((None, 1, CMP_HID, HEAD_DIM), lambda a, g: (l, a, 0, 0)),
        ],
        out_specs=pl.BlockSpec((1, 1, nc, HEAD_DIM), lambda a, g: (a, g, 0, 0)),
        out_shape=jax.ShapeDtypeStruct((2, NSA_GROUPS, nc, HEAD_DIM), F32),
        scratch_shapes=[pltpu.VMEM((nc + 8, CMP_HID), F32)],
        compiler_params=_cparams(("arbitrary", "arbitrary")),
        name="compress_kv",
    )(c, pe, w1, w2)


V_ROWS = HEAD_DIM + 16


def _attend(s, vt_aug):
    m = jnp.max(s, axis=0, keepdims=True)
    e = jnp.exp2(s - m).astype(BF16)
    oa = _dot(vt_aug, e)
    inv = jnp.where(m > 0.5 * NEG, 1.0 / oa[HEAD_DIM:HEAD_DIM + 1, :], 0.0)
    return e, oa[:HEAD_DIM, :] * inv, inv


def _lanes4(fn):
    return jnp.concatenate([fn(hh) for hh in range(NSA_HPG)], axis=-1)


def _nsa_kernel(q_ref, kvc_ref, ks_ref, vs_ref, kw_ref, vw_ref, gl_ref, ts_ref, wb_ref, bc_ref,
                ov_ref, o_ref, mask_ref, sa_ref, sb_ref, *, tk, ncw, nsel, off):
    i = pl.program_id(1)
    q0 = i * Q_BLK
    rows = NSA_HPG * Q_BLK
    qt = q_ref[0, 0]
    qa = jnp.concatenate([qt, jnp.zeros_like(qt)], axis=0)

    win = pl.ds(pl.multiple_of(8 * i, 8), ncw)
    kc = kvc_ref[0, 0, win, :].astype(BF16)
    vc = kvc_ref[0, 1, win, :]
    ones_rows = (lax.broadcasted_iota(jnp.int32, (V_ROWS - HEAD_DIM, ncw), 0) == 0).astype(BF16)
    vct = jnp.concatenate([vc.T[:HEAD_DIM, :].astype(BF16), ones_rows], axis=0)
    neg_row = jnp.where(lax.broadcasted_iota(jnp.int32, (AUG_ROWS, rows), 0) == 0, NEG, 0.0).astype(BF16)
    qa_c = jnp.concatenate([qt, neg_row, jnp.zeros((LANES - HEAD_DIM - AUG_ROWS, rows), BF16)], axis=0)
    s = _dot(kc, qa_c) + _lanes4(lambda hh: bc_ref[hh])
    e, o_c, inv = _attend(s, vct)

    imp4 = _dot(ov_ref[...], e) * inv
    imp = (imp4[:, 0:Q_BLK] + imp4[:, Q_BLK:2 * Q_BLK]) + (imp4[:, 2 * Q_BLK:3 * Q_BLK] + imp4[:, 3 * Q_BLK:])

    jp = lax.broadcasted_iota(jnp.int32, (nsel, Q_BLK), 0)
    jabs = jp + (2 * i - (nsel - 2))
    cur = 2 * i + (lax.broadcasted_iota(jnp.int32, (nsel, Q_BLK), 1) >= SEL_BLK).astype(jnp.int32)
    val = jnp.where(jabs == cur - 1, FORCE_SCORE, imp)
    val = jnp.where(jabs == cur, 2.0 * FORCE_SCORE, val)
    val = jnp.where(jabs == 0, 3.0 * FORCE_SCORE, val)
    val = jnp.where(jabs > cur, -1.0, val)
    val = jnp.where(jabs < 0, -3.0, val)

    work = val
    for _ in range(SEL_TOPN):
        work = jnp.where(work == jnp.max(work, axis=0, keepdims=True), -1e38, work)
    fast = jnp.where((work < -1e37) & (val >= 0.0), 1.0, 0.0)
    tie_seen = jnp.max(jnp.sum(fast, axis=0, keepdims=True)) > SEL_TOPN + 0.5

    def exact_topn():
        wk_, chosen = val, jnp.zeros((nsel, Q_BLK), F32)
        jpf = jp.astype(F32)
        for _ in range(SEL_TOPN):
            mx = jnp.max(wk_, axis=0, keepdims=True)
            jmin = jnp.min(jnp.where(wk_ == mx, jpf, 2.0 * nsel), axis=0, keepdims=True)
            hit = jpf == jmin
            chosen = jnp.where(hit, 1.0, chosen)
            wk_ = jnp.where(hit, -1e38, wk_)
        return jnp.where(val >= 0.0, chosen, 0.0)

    sel = lax.cond(tie_seen, exact_topn, lambda: fast).astype(BF16)
    a_row = lax.broadcasted_iota(jnp.int32, (nsel, nsel), 0)
    s_col = lax.broadcasted_iota(jnp.int32, (nsel, nsel), 1)
    shift = jnp.where(s_col == a_row - (2 * i - (nsel - 2)), 1.0, 0.0).astype(BF16)
    mask_abs = (_dot(shift, sel) - 1.0) * (-NEG)
    mask_ref[...] = jnp.concatenate([mask_abs] * NSA_HPG, axis=-1)

    wk = WIN + Q_BLK
    start = pl.multiple_of(jnp.maximum(q0 - WIN, 0), Q_BLK)
    kvw = kw_ref[0, pl.ds(start, wk), :]
    vwt = vw_ref[0, :, pl.ds(start, wk)]
    wb0 = pl.multiple_of(WIN - (q0 - start), Q_BLK)
    s = _dot(kvw, qa) + _lanes4(lambda hh: wb_ref[hh, pl.ds(wb0, wk), :])
    _, o_w, _ = _attend(s, vwt)

    nblk = tk // SEL_BLK
    far_bias = _lanes4(lambda hh: ts_ref[hh, 0:1, :])
    far_hi = far_bias.astype(BF16).astype(F32)
    far_lo = (far_bias - far_hi).astype(BF16).astype(F32)
    far_lo2 = (far_bias - far_hi - far_lo).astype(BF16).astype(F32)
    far_rows = jnp.concatenate([far_hi, far_lo, far_lo2, jnp.zeros((8 - FAR_TERMS, rows), F32)], axis=0)
    n_pad = AUG_ROWS - nblk - 8
    aug_pad = [jnp.zeros((n_pad, rows), F32)] if n_pad else []
    rhs_tail = jnp.zeros((LANES - HEAD_DIM - AUG_ROWS, rows), BF16)
    n_tiles = (q0 + Q_BLK + tk - 1) // tk
    n_pairs = (n_tiles + 1) // 2
    n_far_pairs = (jnp.maximum(q0 - off + tk, 0) // tk) // (2 * PAIRS_PER_ITER) * PAIRS_PER_ITER
    last_tile = ks_ref.shape[1] // tk - 1

    def scores_into(buf, kt):
        kt = jnp.minimum(kt, last_tile)
        k0 = pl.multiple_of(kt * tk, tk)
        slab = mask_ref[pl.ds(pl.multiple_of(kt * nblk, nblk), nblk), :]
        far = jnp.where(kt < 2 * n_far_pairs, far_rows, 0.0)
        aug = jnp.concatenate([slab, far] + aug_pad, axis=0).astype(BF16)
        rhs = jnp.concatenate([qt, aug, rhs_tail], axis=0)
        buf[...] = _dot(ks_ref[0, pl.ds(k0, tk), :], rhs)

    def sel_tile(buf, kt, carry, near):
        m, acc = carry
        k0 = pl.multiple_of(kt * tk, tk)
        s = buf[...]
        vt = vs_ref[0, :, pl.ds(k0, tk)]
        if near:
            cb = pl.multiple_of(jnp.maximum(off - (q0 - k0), 0), Q_BLK)
            s = s + _lanes4(lambda hh: ts_ref[hh, pl.ds(cb, tk), :])
        m_new = jnp.maximum(m, jnp.max(s, axis=0, keepdims=True))
        p = jnp.exp2(s - m_new)
        acc = jnp.exp2(m - m_new) * acc + _dot(vt, p.astype(BF16))
        return m_new, acc

    def sel_pairs(it, carry, near, pairs, base):
        for j in range(pairs):
            kt = 2 * (base + pairs * it + j)
            scores_into(sb_ref, kt + 1)
            carry = sel_tile(sa_ref, kt, carry, near)
            scores_into(sa_ref, kt + 2)
            carry = sel_tile(sb_ref, kt + 1, carry, near)
        return carry

    scores_into(sa_ref, 0)
    carry = (jnp.full((1, rows), NEG, F32), jnp.zeros((V_ROWS, rows), F32))
    n_near_long = (n_pairs - n_far_pairs) // PAIRS_PER_ITER
    carry = lax.fori_loop(0, n_far_pairs // PAIRS_PER_ITER,
                          functools.partial(sel_pairs, near=False, pairs=PAIRS_PER_ITER, base=0), carry)
    carry = lax.fori_loop(0, n_near_long,
                          functools.partial(sel_pairs, near=True, pairs=PAIRS_PER_ITER, base=n_far_pairs), carry)
    _, acc_s = lax.fori_loop(n_far_pairs + PAIRS_PER_ITER * n_near_long, n_pairs,
                             functools.partial(sel_pairs, near=True, pairs=1, base=0), carry)
    o_s = acc_s[:HEAD_DIM, :] * (1.0 / acc_s[HEAD_DIM:HEAD_DIM + 1, :])

    gates = _sigmoid(gl_ref[0])
    def gate_row(c):
        return _lanes4(lambda hh: gates[3 * hh + c:3 * hh + c + 1, :])
    o = gate_row(0) * o_c + gate_row(1) * o_s + gate_row(2) * o_w
    o = jnp.concatenate([o[:, hh * Q_BLK:(hh + 1) * Q_BLK] for hh in range(NSA_HPG)], axis=0)
    o_ref[...] = o.T.astype(o_ref.dtype)


PAIRS_PER_ITER = 2
FAR_TERMS = 3
AUG_ROWS = 16


def _key_aug_lanes(s, tk):
    nblk = tk // SEL_BLK
    aug = np.zeros((s, HEAD_DIM), np.float32)
    aug[np.arange(s), (np.arange(s) % tk) // SEL_BLK] = 1.0
    aug[:, nblk:nblk + FAR_TERMS] = 1.0
    return aug


def _overlap_matrix_t(ncw, nsel):
    m = np.arange(ncw)[None, :]
    j = np.arange(nsel)[:, None]
    lo = np.maximum(m * CMP_STRIDE, j * SEL_BLK)
    hi = np.minimum(m * CMP_STRIDE + CMP_LEN, (j + 1) * SEL_BLK)
    return (np.maximum(hi - lo, 0).astype(np.float32) / CMP_LEN)


def nsa_attention(qt, kvc_pad, ks, vst, kw, vwt, glt, ts, wb, bc, *, s, tk):
    ncw = s // CMP_STRIDE
    nsel = s // SEL_BLK
    off, ls, lw = _table_sizes(s, tk)
    rows = NSA_HPG * Q_BLK
    ov = jnp.asarray(_overlap_matrix_t(ncw, nsel), BF16)
    kern = functools.partial(_nsa_kernel, tk=tk, ncw=ncw, nsel=nsel, off=off)
    per_group = dict(pipeline_mode=pl.Buffered(1))
    return pl.pallas_call(
        kern,
        grid=(NSA_GROUPS, s // Q_BLK),
        in_specs=[
            pl.BlockSpec((1, 1, HEAD_DIM, rows), lambda g, i: (g, i, 0, 0)),
            pl.BlockSpec((1, 2, 2 * ncw, LANES), lambda g, i: (g, 0, 0, 0), **per_group),
            pl.BlockSpec((1, s, LANES), lambda g, i: (g, 0, 0), **per_group),
            pl.BlockSpec((1, V_ROWS, s), lambda g, i: (g, 0, 0), **per_group),
            pl.BlockSpec((1, s, LANES), lambda g, i: (g, 0, 0), **per_group),
            pl.BlockSpec((1, V_ROWS, s), lambda g, i: (g, 0, 0), **per_group),
            pl.BlockSpec((1, 3 * NSA_HPG, Q_BLK), lambda g, i: (g, 0, i)),
            pl.BlockSpec((NSA_HPG, ls, Q_BLK), lambda g, i: (g, 0, 0), **per_group),
            pl.BlockSpec((NSA_HPG, lw, Q_BLK), lambda g, i: (g, 0, 0), **per_group),
            pl.BlockSpec((NSA_HPG, ncw, Q_BLK), lambda g, i: (g, 0, 0), **per_group),
            pl.BlockSpec((nsel, ncw), lambda g, i: (0, 0), **per_group),
        ],
        out_specs=pl.BlockSpec((Q_BLK, NSA_HPG * HEAD_DIM), lambda g, i: (i, g)),
        out_shape=jax.ShapeDtypeStruct((s, NSA_WIDTH), BF16),
        scratch_shapes=[pltpu.VMEM((nsel, rows), F32), pltpu.VMEM((tk, rows), F32),
                        pltpu.VMEM((tk, rows), F32)],
        compiler_params=_cparams(("arbitrary", "arbitrary")),
        name="nsa_attention",
    )(qt, kvc_pad, ks, vst, kw, vwt, glt, ts, wb, bc, ov)


HALO = 16


def _pool_kernel(u_ref, halo_ref, w_ref, sc_ref, o_ref, ext_ref, *, tm):
    i = pl.program_id(0)
    halo = jnp.where(i > 0, halo_ref[...], 0.0)
    ext_ref[pl.ds(0, HALO), :] = halo
    ext_ref[pl.ds(HALO, tm), :] = u_ref[...]
    t = i * tm + lax.broadcasted_iota(jnp.int32, (tm, 1), 0)
    outs = []
    for gi, w in enumerate(POOL_WINDOWS):
        cols = pl.ds(gi * POOL_GW, POOL_GW)
        x = ext_ref[pl.ds(HALO, tm), cols]
        acc = x
        for k in range(1, w):
            acc = acc + ext_ref[pl.ds(HALO - k, tm), cols]
        cnt = jnp.minimum(t + 1, w).astype(F32)
        y = acc / cnt - x
        outs.append(_dot(y.astype(BF16), w_ref[gi]))
    o_ref[...] = (jnp.concatenate(outs, axis=-1) * sc_ref[...]).astype(o_ref.dtype)


def pool_mixer(z, w_pool, pool_scale, l, *, tm):
    s = z.shape[0]
    ucol = OFF_U // POOL_WIDTH
    return pl.pallas_call(
        functools.partial(_pool_kernel, tm=tm),
        grid=(s // tm,),
        in_specs=[
            pl.BlockSpec((tm, POOL_WIDTH), lambda i: (i, ucol)),
            pl.BlockSpec((HALO, POOL_WIDTH), lambda i: (jnp.maximum(i * (tm // HALO) - 1, 0), ucol)),
            pl.BlockSpec((None, len(POOL_WINDOWS), POOL_GW, POOL_GW), lambda i: (l, 0, 0, 0)),
            pl.BlockSpec((1, POOL_WIDTH), lambda i: (0, 0)),
        ],
        out_specs=pl.BlockSpec((tm, POOL_WIDTH), lambda i: (i, 0)),
        out_shape=jax.ShapeDtypeStruct((s, POOL_WIDTH), BF16),
        scratch_shapes=[pltpu.VMEM((HALO + tm, POOL_WIDTH), F32)],
        compiler_params=_cparams(("parallel",)),
        name="pool_mixer",
    )(z, z, w_pool, pool_scale.reshape(1, POOL_WIDTH))


def _merge_kernel(a_ref, p_ref, ml_ref, x_ref, wa_ref, wp_ref, wo_ref, g_ref, o_ref):
    pa = _dot(a_ref[...], wa_ref[...])
    pp = _dot(p_ref[...], wp_ref[...])
    ml = ml_ref[...]
    y = _sigmoid(ml[:, :D_MODEL]) * pa + _sigmoid(ml[:, D_MODEL:]) * pp
    y = _dot(y.astype(BF16), wo_ref[...])
    o_ref[...] = x_ref[...] + _rms(y, g_ref[...])


def merge_mix(a, p, z, x, wa, wp, wo, g, l, *, tm):
    s = x.shape[0]
    const = lambda i: (0, 0)
    layer = lambda i: (l, 0, 0)
    return pl.pallas_call(
        _merge_kernel,
        grid=(s // tm,),
        in_specs=[
            pl.BlockSpec((tm, NSA_WIDTH), lambda i: (i, 0)),
            pl.BlockSpec((tm, POOL_WIDTH), lambda i: (i, 0)),
            pl.BlockSpec((tm, 2 * D_MODEL), lambda i: (i, OFF_ML // (2 * D_MODEL))),
            pl.BlockSpec((tm, D_MODEL), lambda i: (i, 0)),
            pl.BlockSpec((None, NSA_WIDTH, D_MODEL), layer, pipeline_mode=pl.Buffered(1)),
            pl.BlockSpec((None, POOL_WIDTH, D_MODEL), layer, pipeline_mode=pl.Buffered(1)),
            pl.BlockSpec((None, D_MODEL, D_MODEL), layer, pipeline_mode=pl.Buffered(1)),
            pl.BlockSpec((1, D_MODEL), const),
        ],
        out_specs=pl.BlockSpec((tm, D_MODEL), lambda i: (i, 0)),
        out_shape=jax.ShapeDtypeStruct((s, D_MODEL), F32),
        compiler_params=_cparams(("parallel",)),
        name="merge_mix",
    )(a, p, z, x, wa, wp, wo, g.reshape(1, D_MODEL))


def _xattn_kernel(x_ref, gpre_ref, wq_ref, kv_ref, wo_ref, gpost_ref, o_ref):
    x = x_ref[...]
    h = _rms(x, gpre_ref[...]).astype(BF16)
    q = _dot(h, wq_ref[...]).astype(BF16)
    kv = kv_ref[...]
    outs = []
    for hh in range(X_HEADS):
        lo = hh * X_HEAD_DIM
        sc = _dot_nt(q[:, lo:lo + X_HEAD_DIM], kv[:, lo:lo + X_HEAD_DIM]) * (X_HEAD_DIM ** -0.5)
        m = jnp.max(sc, axis=-1, keepdims=True)
        e = jnp.exp(sc - m)
        p = (e / jnp.sum(e, axis=-1, keepdims=True)).astype(BF16)
        outs.append(_dot(p, kv[:, X_WIDTH + lo:X_WIDTH + lo + X_HEAD_DIM]))
    o = jnp.concatenate(outs, axis=-1).astype(BF16)
    y = _dot(o, wo_ref[...])
    o_ref[...] = x + _rms(y, gpost_ref[...])


def cross_attention(x, gpre, wq, memkv, wo, gpost, l, *, tm):
    s = x.shape[0]
    const = lambda i: (0, 0)
    layer = lambda i: (l, 0, 0)
    return pl.pallas_call(
        _xattn_kernel,
        grid=(s // tm,),
        in_specs=[
            pl.BlockSpec((tm, D_MODEL), lambda i: (i, 0)),
            pl.BlockSpec((1, D_MODEL), const),
            pl.BlockSpec((None, D_MODEL, X_WIDTH), layer),
            pl.BlockSpec((N_MEM, 2 * X_WIDTH), const),
            pl.BlockSpec((None, X_WIDTH, D_MODEL), layer),
            pl.BlockSpec((1, D_MODEL), const),
        ],
        out_specs=pl.BlockSpec((tm, D_MODEL), lambda i: (i, 0)),
        out_shape=jax.ShapeDtypeStruct((s, D_MODEL), F32),
        compiler_params=_cparams(("parallel",)),
        name="cross_attention",
    )(x, gpre.reshape(1, D_MODEL), wq, memkv, wo, gpost.reshape(1, D_MODEL))


def _ffn_kernel(x_ref, gpre_ref, wg_ref, wu_ref, wd_ref, gpost_ref, o_ref, h_ref, acc_ref):
    f = pl.program_id(1)

    @pl.when(f == 0)
    def _():
        h_ref[...] = _rms(x_ref[...], gpre_ref[...]).astype(BF16)
        acc_ref[...] = jnp.zeros_like(acc_ref)

    h = h_ref[...]
    a = _dot(h, wg_ref[...])
    b = _dot(h, wu_ref[...])
    t = (a * _sigmoid(a) * b).astype(BF16)
    acc_ref[...] += _dot(t, wd_ref[...])

    @pl.when(f == pl.num_programs(1) - 1)
    def _():
        o_ref[...] = x_ref[...] + _rms(acc_ref[...], gpost_ref[...])


def ffn(x, gpre, wg, wu, wd, gpost, l, *, tm, tf):
    s = x.shape[0]
    return pl.pallas_call(
        _ffn_kernel,
        grid=(s // tm, D_FF // tf),
        in_specs=[
            pl.BlockSpec((tm, D_MODEL), lambda i, f: (i, 0)),
            pl.BlockSpec((1, D_MODEL), lambda i, f: (0, 0)),
            pl.BlockSpec((None, D_MODEL, tf), lambda i, f: (l, 0, f)),
            pl.BlockSpec((None, D_MODEL, tf), lambda i, f: (l, 0, f)),
            pl.BlockSpec((None, tf, D_MODEL), lambda i, f: (l, f, 0)),
            pl.BlockSpec((1, D_MODEL), lambda i, f: (0, 0)),
        ],
        out_specs=pl.BlockSpec((tm, D_MODEL), lambda i, f: (i, 0)),
        out_shape=jax.ShapeDtypeStruct((s, D_MODEL), F32),
        scratch_shapes=[pltpu.VMEM((tm, D_MODEL), BF16), pltpu.VMEM((tm, D_MODEL), F32)],
        compiler_params=_cparams(("parallel", "arbitrary")),
        name="ffn_swiglu",
    )(x, gpre.reshape(1, D_MODEL), wg, wu, wd, gpost.reshape(1, D_MODEL))


def _tiles(s):
    big = s >= 4096
    return dict(
        tm_in=1024 if big else 256, tn_in=512,
        tk_sel=512 if big else 256,
        tm_pool=512 if big else 256,
        tm_merge=256,
        tm_x=512 if big else 256,
        tm_ffn=512 if big else 256, tf=512,
    )


def _pad_w_in(w_in):
    w = w_in.astype(BF16)
    n_gl = 3 * NSA_HEADS
    o_gl = NSA_WIDTH + 6 * KV_WIDTH
    o_u = o_gl + n_gl
    o_ml = o_u + POOL_WIDTH
    pad = jnp.zeros(w.shape[:2] + (GL_PAD - n_gl,), BF16)
    return jnp.concatenate(
        [w[..., o_ml:], w[..., o_u:o_ml], w[..., :o_gl], w[..., o_gl:o_u], pad], axis=-1)


def _forward(x, mem, rel_bias, ln_mix_pre, ln_mix_post, ln_x_pre, ln_x_post, ln_mem,
             ln_ffn_pre, ln_ffn_post, w_in, cmp_pe, cmp_w1, cmp_w2, w_pool, pool_scale,
             w_br_attn, w_br_pool, w_mix_out, w_xq, w_xkv, w_xo, w_gate, w_up, w_down):
    b, s, _ = x.shape
    assert b == 1 and s % 1024 == 0
    depth = w_in.shape[0]
    tl = _tiles(s)
    nc = s // CMP_STRIDE
    nq = s // Q_BLK
    half = CMP_STRIDE * HEAD_DIM
    ts, wb, bc = bias_tables(rel_bias, s=s, tk=tl["tk_sel"])
    key_aug = jnp.asarray(_key_aug_lanes(s, tl["tk_sel"]))
    v_ones = jnp.zeros((NSA_GROUPS, V_ROWS - HEAD_DIM, s), BF16).at[:, 0, :].set(1.0)
    w_in_b = _pad_w_in(w_in)
    pe_b = cmp_pe.reshape(depth, 2, 2, 1, half)
    w1_b = cmp_w1.reshape(depth, 2, 2, half, CMP_HID).astype(BF16)
    w2_b, w_pool_b = cmp_w2.astype(BF16), w_pool.astype(BF16)
    wa_b, wp_b, wo_b = w_br_attn.astype(BF16), w_br_pool.astype(BF16), w_mix_out.astype(BF16)
    wxq_b, wxkv_b, wxo_b = w_xq.astype(BF16), w_xkv.astype(BF16), w_xo.astype(BF16)
    wg_b, wu_b, wd_b = w_gate.astype(BF16), w_up.astype(BF16), w_down.astype(BF16)
    xc = x[0]
    mem2 = mem[0]
    for l in range(depth):
        z = norm_matmul(xc, ln_mix_pre[l], w_in_b, l, tm=tl["tm_in"], tn=tl["tn_in"])
        q = z[:, OFF_Q:OFF_Q + NSA_WIDTH] * ((HEAD_DIM ** -0.5) * LOG2E)
        qt = q.reshape(nq, Q_BLK, NSA_GROUPS, NSA_HPG, HEAD_DIM).transpose(2, 0, 4, 3, 1)
        qt = qt.reshape(NSA_GROUPS, nq, HEAD_DIM, NSA_HPG * Q_BLK).astype(BF16)
        kv = z[:, OFF_KV:OFF_KV + 6 * KV_WIDTH].reshape(s, 6, NSA_GROUPS, HEAD_DIM)
        c = kv[:, 0:2].reshape(nc, CMP_STRIDE, 2, NSA_GROUPS, HEAD_DIM)
        c = c.transpose(2, 3, 0, 1, 4).reshape(2, NSA_GROUPS, nc, half)
        kvc = compress(c, pe_b, w1_b, w2_b, l, s=s)
        kvc = jnp.transpose(kvc, (1, 0, 2, 3))
        kvc_pad = jnp.pad(kvc, ((0, 0), (0, 0), (nc - 8, 8), (0, LANES - HEAD_DIM)))
        kvc_pad = kvc_pad.at[:, 0, :nc - 8, HEAD_DIM].set(1.0)
        ks = jnp.concatenate([kv[:, 2], jnp.broadcast_to(key_aug[:, None, :], (s, NSA_GROUPS, HEAD_DIM))],
                             axis=-1).transpose(1, 0, 2).astype(BF16)
        kw = jnp.concatenate([kv[:, 4], kv[:, 5]], axis=-1).transpose(1, 0, 2).astype(BF16)
        vst = jnp.concatenate([kv[:, 3].transpose(1, 2, 0).astype(BF16), v_ones], axis=1)
        vwt = jnp.concatenate([kv[:, 5].transpose(1, 2, 0).astype(BF16), v_ones], axis=1)
        glt = z[:, OFF_GL:OFF_GL + 3 * NSA_HEADS].reshape(s, NSA_GROUPS, 3 * NSA_HPG).transpose(1, 2, 0)
        a = nsa_attention(qt, kvc_pad, ks, vst, kw, vwt, glt, ts, wb, bc, s=s, tk=tl["tk_sel"])
        p = pool_mixer(z, w_pool_b, pool_scale[l], l, tm=tl["tm_pool"])
        xc = merge_mix(a, p, z, xc, wa_b, wp_b, wo_b, ln_mix_post[l], l, tm=tl["tm_merge"])
        memkv = norm_matmul(mem2, ln_mem[l], wxkv_b, l, tm=N_MEM, tn=512, out_dtype=BF16)
        xc = cross_attention(xc, ln_x_pre[l], wxq_b, memkv, wxo_b, ln_x_post[l], l, tm=tl["tm_x"])
        xc = ffn(xc, ln_ffn_pre[l], wg_b, wu_b, wd_b, ln_ffn_post[l], l, tm=tl["tm_ffn"], tf=tl["tf"])
    return xc[None]


def kernel(x, mem, rel_bias, ln_mix_pre, ln_mix_post, ln_x_pre, ln_x_post, ln_mem, ln_ffn_pre,
           ln_ffn_post, w_in, cmp_pe, cmp_w1, cmp_w2, w_pool, pool_scale, w_br_attn, w_br_pool,
           w_mix_out, w_xq, w_xkv, w_xo, w_gate, w_up, w_down):
    return _forward(x, mem, rel_bias, ln_mix_pre, ln_mix_post, ln_x_pre, ln_x_post, ln_mem,
                    ln_ffn_pre, ln_ffn_post, w_in, cmp_pe, cmp_w1, cmp_w2, w_pool, pool_scale,
                    w_br_attn, w_br_pool, w_mix_out, w_xq, w_xkv, w_xo, w_gate, w_up, w_down)
```

```python
import functools
import math

import numpy as np
import jax
import jax.numpy as jnp
from jax import lax
from jax.experimental import pallas as pl
from jax.experimental.pallas import tpu as pltpu

F32 = jnp.float32
BF16 = jnp.bfloat16

D_MODEL = 2048
DEPTH = 4
N_MEM = 256
NSA_HEADS = 16
NSA_GROUPS = 4
NSA_HPG = 4
HEAD_DIM = 64
NSA_WIDTH = 1024
KV_WIDTH = 256
CMP_LEN = 32
CMP_STRIDE = 16
CMP_HID = 128
SEL_BLK = 64
SEL_TOPN = 16
WIN = 512
Q_BLK = 128
FORCE_SCORE = 1e4
POOL_WINDOWS = (2, 4, 8, 16)
POOL_GW = 256
POOL_WIDTH = 1024
REL_BUCKETS = 32
REL_MAX_DIST = 2048
X_HEADS = 4
X_HEAD_DIM = 128
X_WIDTH = 512
D_FF = 5632
NEG = -1e30
LOG2E = 1.4426950408889634

GL_PAD = 512
OFF_ML, OFF_U, OFF_Q, OFF_KV, OFF_GL = 0, 4096, 5120, 6144, 7680
IN_PAD = 8192

VMEM_LIMIT_V7X = 56 * 1024 * 1024
LANES = 128


def _cparams(sem, flags=None):
    return pltpu.CompilerParams(dimension_semantics=sem, vmem_limit_bytes=VMEM_LIMIT_V7X, flags=flags)


def _rms(x, g):
    ms = jnp.mean(x * x, axis=-1, keepdims=True)
    return x * lax.rsqrt(ms + 1e-6) * g


def _sigmoid(x):
    return 1.0 / (1.0 + jnp.exp(-x))


def _dot(a, b):
    return jnp.dot(a, b, preferred_element_type=F32)


def _dot_nt(a, b):
    return lax.dot_general(a, b, (((1,), (1,)), ((), ())), preferred_element_type=F32)


def _norm_matmul_kernel(x_ref, g_ref, w_ref, o_ref, h_ref):
    @pl.when(pl.program_id(1) == 0)
    def _():
        h_ref[...] = _rms(x_ref[...], g_ref[...]).astype(BF16)

    o_ref[...] = _dot(h_ref[...], w_ref[...]).astype(o_ref.dtype)


def norm_matmul(x, g, w, l, *, tm, tn, out_dtype=F32):
    m, k = x.shape
    n = w.shape[2]
    return pl.pallas_call(
        _norm_matmul_kernel,
        grid=(m // tm, n // tn),
        in_specs=[
            pl.BlockSpec((tm, k), lambda i, j: (i, 0)),
            pl.BlockSpec((1, k), lambda i, j: (0, 0)),
            pl.BlockSpec((None, k, tn), lambda i, j: (l, 0, j)),
        ],
        out_specs=pl.BlockSpec((tm, tn), lambda i, j: (i, j)),
        out_shape=jax.ShapeDtypeStruct((m, n), out_dtype),
        scratch_shapes=[pltpu.VMEM((tm, k), BF16)],
        compiler_params=_cparams(("parallel", "arbitrary")),
        name="norm_matmul",
    )(x, g.reshape(1, k), w)


def _bucket(d):
    n = jnp.maximum(d, 0)
    exact = REL_BUCKETS // 2
    nf = jnp.maximum(n, 1).astype(F32)
    large = exact + (jnp.log(nf / exact) / math.log(REL_MAX_DIST / exact)
                     * (REL_BUCKETS - exact)).astype(jnp.int32)
    return jnp.where(n < exact, n, jnp.minimum(large, REL_BUCKETS - 1))


def _lookup(tab_ref, h, bk):
    out = jnp.full(bk.shape, tab_ref[h, 0], F32)
    for b in range(1, REL_BUCKETS):
        out = jnp.where(bk == b, tab_ref[h, b], out)
    return out


def _bias_tables_kernel(tab_ref, ts_ref, wb_ref, bc_ref, *, ls, off, lw, ncw, cmp_shift):
    h = pl.program_id(0)
    r = lax.broadcasted_iota(jnp.int32, (LANES, Q_BLK), 0)
    qi = lax.broadcasted_iota(jnp.int32, (LANES, Q_BLK), 1)

    def chunked(ref, n_rows, dist, valid):
        def body(ch, carry):
            d = dist(ch * LANES + r)
            val = _lookup(tab_ref, h, _bucket(d))
            ref[0, pl.ds(pl.multiple_of(ch * LANES, LANES), LANES), :] = jnp.where(valid(d), val * LOG2E, NEG)
            return carry
        lax.fori_loop(0, n_rows // LANES, body, 0)

    chunked(ts_ref, ls, lambda c: qi - c + off, lambda d: d >= 0)
    chunked(wb_ref, lw, lambda c: qi - c + WIN, lambda d: (d >= 0) & (d < WIN))
    chunked(bc_ref, ncw, lambda m: qi - CMP_STRIDE * m + cmp_shift, lambda d: d >= 0)


def _table_sizes(s, tk):
    off = REL_MAX_DIST + tk
    ls = off + 3 * tk - Q_BLK
    lw = 2 * WIN + Q_BLK
    return off, ls, lw


def bias_tables(rel_bias, *, s, tk):
    off, ls, lw = _table_sizes(s, tk)
    ncw = s // CMP_STRIDE
    cmp_shift = CMP_STRIDE * (ncw - 8) - (CMP_LEN - 1)
    kern = functools.partial(_bias_tables_kernel, ls=ls, off=off, lw=lw, ncw=ncw, cmp_shift=cmp_shift)
    return pl.pallas_call(
        kern,
        grid=(NSA_HEADS,),
        in_specs=[pl.BlockSpec(memory_space=pltpu.SMEM)],
        out_specs=[
            pl.BlockSpec((1, ls, Q_BLK), lambda h: (h, 0, 0)),
            pl.BlockSpec((1, lw, Q_BLK), lambda h: (h, 0, 0)),
            pl.BlockSpec((1, ncw, Q_BLK), lambda h: (h, 0, 0)),
        ],
        out_shape=[
            jax.ShapeDtypeStruct((NSA_HEADS, ls, Q_BLK), F32),
            jax.ShapeDtypeStruct((NSA_HEADS, lw, Q_BLK), F32),
            jax.ShapeDtypeStruct((NSA_HEADS, ncw, Q_BLK), F32),
        ],
        compiler_params=_cparams(("arbitrary",)),
        name="bias_tables",
    )(rel_bias)


def _gelu_tanh(x):
    return 0.5 * x * (1.0 + jnp.tanh(math.sqrt(2.0 / math.pi) * (x + 0.044715 * x * x * x)))


def _compress_kernel(c_ref, pe_ref, w1_ref, w2_ref, o_ref, scr_ref, *, nc):
    c = c_ref[0, 0]
    ca = (c + pe_ref[0, 0]).astype(BF16)
    cb = (c + pe_ref[0, 1]).astype(BF16)
    h1 = _dot(ca, w1_ref[0, 0])
    h2 = _dot(cb, w1_ref[0, 1])
    scr_ref[pl.ds(0, nc), :] = h2
    scr_ref[pl.ds(nc, 8), :] = jnp.zeros((8, CMP_HID), F32)
    hid = h1 + scr_ref[pl.ds(1, nc), :]
    out = _dot(_gelu_tanh(hid).astype(BF16), w2_ref[0])
    row = lax.broadcasted_iota(jnp.int32, out.shape, 0)
    o_ref[0, 0] = jnp.where(row < nc - 1, out, 0.0)


def compress(c, pe, w1, w2, l, *, s):
    nc = s // CMP_STRIDE
    half = CMP_STRIDE * HEAD_DIM
    return pl.pallas_call(
        functools.partial(_compress_kernel, nc=nc),
        grid=(2, NSA_GROUPS),
        in_specs=[
            pl.BlockSpec((1, 1, nc, half), lambda a, g: (a, g, 0, 0)),
            pl.BlockSpec((None, 1, 2, 1, half), lambda a, g: (l, a, 0, 0, 0)),
            pl.BlockSpec((None, 1, 2, half, CMP_HID), lambda a, g: (l, a, 0, 0, 0)),
            pl.BlockSpec((None, 1, CMP_HID, HEAD_DIM), lambda a, g: (l, a, 0, 0)),
        ],
        out_specs=pl.BlockSpec((1, 1, nc, HEAD_DIM), lambda a, g: (a, g, 0, 0)),
        out_shape=jax.ShapeDtypeStruct((2, NSA_GROUPS, nc, HEAD_DIM), F32),
        scratch_shapes=[pltpu.VMEM((nc + 8, CMP_HID), F32)],
        compiler_params=_cparams(("arbitrary", "arbitrary")),
        name="compress_kv",
    )(c, pe, w1, w2)


V_ROWS = HEAD_DIM + 16


def _attend(s, vt_aug):
    m = jnp.max(s, axis=0, keepdims=True)
    e = jnp.exp2(s - m).astype(BF16)
    oa = _dot(vt_aug, e)
    inv = jnp.where(m > 0.5 * NEG, 1.0 / oa[HEAD_DIM:HEAD_DIM + 1, :], 0.0)
    return e, oa[:HEAD_DIM, :] * inv, inv


def _lanes4(fn):
    return jnp.concatenate([fn(hh) for hh in range(NSA_HPG)], axis=-1)


def _nsa_kernel(q_ref, kvc_ref, ks_ref, vs_ref, kw_ref, vw_ref, gl_ref, ts_ref, wb_ref, bc_ref,
                ov_ref, o_ref, mask_ref, sa_ref, sb_ref, sw_ref, *, tk, ncw, nsel, off):
    i = pl.program_id(1)
    q0 = i * Q_BLK
    rows = NSA_HPG * Q_BLK
    qt = q_ref[0, 0]
    qa = jnp.concatenate([qt, jnp.zeros_like(qt)], axis=0)

    wk = WIN + Q_BLK
    start = pl.multiple_of(jnp.maximum(q0 - WIN, 0), Q_BLK)
    sw_ref[...] = _dot(kw_ref[0, pl.ds(start, wk), :], qa)

    tc = min(tk, ncw)
    neg_row = jnp.where(lax.broadcasted_iota(jnp.int32, (AUG_ROWS, rows), 0) == 0, NEG, 0.0).astype(BF16)
    qa_c = jnp.concatenate([qt, neg_row, jnp.zeros((LANES - HEAD_DIM - AUG_ROWS, rows), BF16)], axis=0)
    ones_rows = (lax.broadcasted_iota(jnp.int32, (V_ROWS - HEAD_DIM, tc), 0) == 0).astype(BF16)

    def cmp_rows(t):
        return pl.ds(pl.multiple_of(8 * i + t * tc, 8), tc)

    def cmp_scores(buf, t):
        buf[pl.ds(0, tc), :] = _dot(kvc_ref[0, 0, cmp_rows(t), :].astype(BF16), qa_c)

    def cmp_tile(buf, t, carry):
        m, acc, impa = carry
        s = buf[pl.ds(0, tc), :] + _lanes4(lambda hh: bc_ref[hh, t * tc:(t + 1) * tc, :])
        vc = kvc_ref[0, 1, cmp_rows(t), :]
        vct = jnp.concatenate([vc.T[:HEAD_DIM, :].astype(BF16), ones_rows], axis=0)
        m_new = jnp.maximum(m, jnp.max(s, axis=0, keepdims=True))
        alpha = jnp.exp2(m - m_new)
        p = jnp.exp2(s - m_new).astype(BF16)
        return (m_new, alpha * acc + _dot(vct, p),
                alpha * impa + _dot(ov_ref[:, t * tc:(t + 1) * tc], p))

    cmp_bufs = (sa_ref, sb_ref)
    n_ct = ncw // tc
    cmp_scores(cmp_bufs[0], 0)

    carry_c = (jnp.full((1, rows), NEG, F32), jnp.zeros((V_ROWS, rows), F32), jnp.zeros((nsel, rows), F32))
    for t in range(n_ct):
        if t + 1 < n_ct:
            cmp_scores(cmp_bufs[(t + 1) % 2], t + 1)
        carry_c = cmp_tile(cmp_bufs[t % 2], t, carry_c)
    m_c, acc_c, imp_c = carry_c
    inv = jnp.where(m_c > 0.5 * NEG, 1.0 / acc_c[HEAD_DIM:HEAD_DIM + 1, :], 0.0)
    o_c = acc_c[:HEAD_DIM, :] * inv
    imp4 = imp_c * inv
    imp = (imp4[:, 0:Q_BLK] + imp4[:, Q_BLK:2 * Q_BLK]) + (imp4[:, 2 * Q_BLK:3 * Q_BLK] + imp4[:, 3 * Q_BLK:])

    vwt = vw_ref[0, :, pl.ds(start, wk)]
    wb0 = pl.multiple_of(WIN - (q0 - start), Q_BLK)
    _, o_w, _ = _attend(sw_ref[...] + _lanes4(lambda hh: wb_ref[hh, pl.ds(wb0, wk), :]), vwt)

    jp = lax.broadcasted_iota(jnp.int32, (nsel, Q_BLK), 0)
    jabs = jp + (2 * i - (nsel - 2))
    cur = 2 * i + (lax.broadcasted_iota(jnp.int32, (nsel, Q_BLK), 1) >= SEL_BLK).astype(jnp.int32)
    val = jnp.where(jabs == cur - 1, FORCE_SCORE, imp)
    val = jnp.where(jabs == cur, 2.0 * FORCE_SCORE, val)
    val = jnp.where(jabs == 0, 3.0 * FORCE_SCORE, val)
    val = jnp.where(jabs > cur, -1.0, val)
    val = jnp.where(jabs < 0, -3.0, val)

    work = val
    for _ in range(SEL_TOPN):
        work = jnp.where(work == jnp.max(work, axis=0, keepdims=True), -1e38, work)
    fast = jnp.where((work < -1e37) & (val >= 0.0), 1.0, 0.0)
    tie_seen = jnp.max(jnp.sum(fast, axis=0, keepdims=True)) > SEL_TOPN + 0.5

    def exact_topn():
        wk_, chosen = val, jnp.zeros((nsel, Q_BLK), F32)
        jpf = jp.astype(F32)
        for _ in range(SEL_TOPN):
            mx = jnp.max(wk_, axis=0, keepdims=True)
            jmin = jnp.min(jnp.where(wk_ == mx, jpf, 2.0 * nsel), axis=0, keepdims=True)
            hit = jpf == jmin
            chosen = jnp.where(hit, 1.0, chosen)
            wk_ = jnp.where(hit, -1e38, wk_)
        return jnp.where(val >= 0.0, chosen, 0.0)

    sel = lax.cond(tie_seen, exact_topn, lambda: fast).astype(BF16)
    a_row = lax.broadcasted_iota(jnp.int32, (nsel, nsel), 0)
    s_col = lax.broadcasted_iota(jnp.int32, (nsel, nsel), 1)
    shift = jnp.where(s_col == a_row - (2 * i - (nsel - 2)), 1.0, 0.0).astype(BF16)
    mask_abs = (_dot(shift, sel) - 1.0) * (-NEG)
    mask_ref[...] = jnp.concatenate([mask_abs] * NSA_HPG, axis=-1)

    nblk = tk // SEL_BLK
    far_bias = _lanes4(lambda hh: ts_ref[hh, 0:1, :])
    far_hi = far_bias.astype(BF16).astype(F32)
    far_lo = (far_bias - far_hi).astype(BF16).astype(F32)
    far_lo2 = (far_bias - far_hi - far_lo).astype(BF16).astype(F32)
    far_rows = jnp.concatenate([far_hi, far_lo, far_lo2, jnp.zeros((8 - FAR_TERMS, rows), F32)], axis=0)
    n_pad = AUG_ROWS - nblk - 8
    aug_pad = [jnp.zeros((n_pad, rows), F32)] if n_pad else []
    rhs_tail = jnp.zeros((LANES - HEAD_DIM - AUG_ROWS, rows), BF16)
    n_tiles = (q0 + Q_BLK + tk - 1) // tk
    n_pairs = (n_tiles + 1) // 2
    n_far_pairs = (jnp.maximum(q0 - off + tk, 0) // tk) // (2 * PAIRS_PER_ITER) * PAIRS_PER_ITER
    last_tile = ks_ref.shape[1] // tk - 1

    def scores_into(buf, kt):
        kt = jnp.minimum(kt, last_tile)
        k0 = pl.multiple_of(kt * tk, tk)
        slab = mask_ref[pl.ds(pl.multiple_of(kt * nblk, nblk), nblk), :]
        far = jnp.where(kt < 2 * n_far_pairs, far_rows, 0.0)
        aug = jnp.concatenate([slab, far] + aug_pad, axis=0).astype(BF16)
        rhs = jnp.concatenate([qt, aug, rhs_tail], axis=0)
        buf[...] = _dot(ks_ref[0, pl.ds(k0, tk), :], rhs)

    def sel_tile(buf, kt, carry, near):
        m, acc = carry
        k0 = pl.multiple_of(kt * tk, tk)
        s = buf[...]
        vt = vs_ref[0, :, pl.ds(k0, tk)]
        if near:
            cb = pl.multiple_of(jnp.maximum(off - (q0 - k0), 0), Q_BLK)
            s = s + _lanes4(lambda hh: ts_ref[hh, pl.ds(cb, tk), :])
        m_new = jnp.maximum(m, jnp.max(s, axis=0, keepdims=True))
        p = jnp.exp2(s - m_new)
        acc = jnp.exp2(m - m_new) * acc + _dot(vt, p.astype(BF16))
        return m_new, acc

    def sel_pairs(it, carry, near, pairs, base):
        for j in range(pairs):
            kt = 2 * (base + pairs * it + j)
            scores_into(sb_ref, kt + 1)
            carry = sel_tile(sa_ref, kt, carry, near)
            scores_into(sa_ref, kt + 2)
            carry = sel_tile(sb_ref, kt + 1, carry, near)
        return carry

    scores_into(sa_ref, 0)
    carry = (jnp.full((1, rows), NEG, F32), jnp.zeros((V_ROWS, rows), F32))
    n_near_long = (n_pairs - n_far_pairs) // PAIRS_PER_ITER
    carry = lax.fori_loop(0, n_far_pairs // PAIRS_PER_ITER,
                          functools.partial(sel_pairs, near=False, pairs=PAIRS_PER_ITER, base=0), carry)
    carry = lax.fori_loop(0, n_near_long,
                          functools.partial(sel_pairs, near=True, pairs=PAIRS_PER_ITER, base=n_far_pairs), carry)
    _, acc_s = lax.fori_loop(n_far_pairs + PAIRS_PER_ITER * n_near_long, n_pairs,
                             functools.partial(sel_pairs, near=True, pairs=1, base=0), carry)
    o_s = acc_s[:HEAD_DIM, :] * (1.0 / acc_s[HEAD_DIM:HEAD_DIM + 1, :])

    gates = _sigmoid(gl_ref[0])
    def gate_row(c):
        return _lanes4(lambda hh: gates[3 * hh + c:3 * hh + c + 1, :])
    o = gate_row(0) * o_c + gate_row(1) * o_s + gate_row(2) * o_w
    o = jnp.concatenate([o[:, hh * Q_BLK:(hh + 1) * Q_BLK] for hh in range(NSA_HPG)], axis=0)
    o_ref[...] = o.T.astype(o_ref.dtype)


PAIRS_PER_ITER = 2
FAR_TERMS = 3
AUG_ROWS = 16


def _key_aug_lanes(s, tk):
    nblk = tk // SEL_BLK
    aug = np.zeros((s, HEAD_DIM), np.float32)
    aug[np.arange(s), (np.arange(s) % tk) // SEL_BLK] = 1.0
    aug[:, nblk:nblk + FAR_TERMS] = 1.0
    return aug


def _overlap_matrix_t(ncw, nsel):
    m = np.arange(ncw)[None, :]
    j = np.arange(nsel)[:, None]
    lo = np.maximum(m * CMP_STRIDE, j * SEL_BLK)
    hi = np.minimum(m * CMP_STRIDE + CMP_LEN, (j + 1) * SEL_BLK)
    return (np.maximum(hi - lo, 0).astype(np.float32) / CMP_LEN)


def nsa_attention(qt, kvc_pad, ks, vst, kw, vwt, glt, ts, wb, bc, *, s, tk):
    ncw = s // CMP_STRIDE
    nsel = s // SEL_BLK
    off, ls, lw = _table_sizes(s, tk)
    rows = NSA_HPG * Q_BLK
    ov = jnp.asarray(_overlap_matrix_t(ncw, nsel), BF16)
    kern = functools.partial(_nsa_kernel, tk=tk, ncw=ncw, nsel=nsel, off=off)
    per_group = dict(pipeline_mode=pl.Buffered(1))
    return pl.pallas_call(
        kern,
        grid=(NSA_GROUPS, s // Q_BLK),
        in_specs=[
            pl.BlockSpec((1, 1, HEAD_DIM, rows), lambda g, i: (g, i, 0, 0)),
            pl.BlockSpec((1, 2, 2 * ncw, LANES), lambda g, i: (g, 0, 0, 0), **per_group),
            pl.BlockSpec((1, s, LANES), lambda g, i: (g, 0, 0), **per_group),
            pl.BlockSpec((1, V_ROWS, s), lambda g, i: (g, 0, 0), **per_group),
            pl.BlockSpec((1, s, LANES), lambda g, i: (g, 0, 0), **per_group),
            pl.BlockSpec((1, V_ROWS, s), lambda g, i: (g, 0, 0), **per_group),
            pl.BlockSpec((1, 3 * NSA_HPG, Q_BLK), lambda g, i: (g, 0, i)),
            pl.BlockSpec((NSA_HPG, ls, Q_BLK), lambda g, i: (g, 0, 0), **per_group),
            pl.BlockSpec((NSA_HPG, lw, Q_BLK), lambda g, i: (g, 0, 0), **per_group),
            pl.BlockSpec((NSA_HPG, ncw, Q_BLK), lambda g, i: (g, 0, 0), **per_group),
            pl.BlockSpec((nsel, ncw), lambda g, i: (0, 0), **per_group),
        ],
        out_specs=pl.BlockSpec((Q_BLK, NSA_HPG * HEAD_DIM), lambda g, i: (i, g)),
        out_shape=jax.ShapeDtypeStruct((s, NSA_WIDTH), BF16),
        scratch_shapes=[pltpu.VMEM((nsel, rows), F32), pltpu.VMEM((tk, rows), F32),
                        pltpu.VMEM((tk, rows), F32), pltpu.VMEM((WIN + Q_BLK, rows), F32)],
        compiler_params=_cparams(("arbitrary", "arbitrary")),
        name="nsa_attention",
    )(qt, kvc_pad, ks, vst, kw, vwt, glt, ts, wb, bc, ov)


HALO = 16


def _pool_kernel(u_ref, halo_ref, w_ref, sc_ref, o_ref, ext_ref, *, tm):
    i = pl.program_id(0)
    halo = jnp.where(i > 0, halo_ref[...], 0.0)
    ext_ref[pl.ds(0, HALO), :] = halo
    ext_ref[pl.ds(HALO, tm), :] = u_ref[...]
    t = i * tm + lax.broadcasted_iota(jnp.int32, (tm, 1), 0)
    outs = []
    for gi, w in enumerate(POOL_WINDOWS):
        cols = pl.ds(gi * POOL_GW, POOL_GW)
        x = ext_ref[pl.ds(HALO, tm), cols]
        acc = x
        for k in range(1, w):
            acc = acc + ext_ref[pl.ds(HALO - k, tm), cols]
        cnt = jnp.minimum(t + 1, w).astype(F32)
        y = acc / cnt - x
        outs.append(_dot(y.astype(BF16), w_ref[gi]))
    o_ref[...] = (jnp.concatenate(outs, axis=-1) * sc_ref[...]).astype(o_ref.dtype)


def pool_mixer(z, w_pool, pool_scale, l, *, tm):
    s = z.shape[0]
    ucol = OFF_U // POOL_WIDTH
    return pl.pallas_call(
        functools.partial(_pool_kernel, tm=tm),
        grid=(s // tm,),
        in_specs=[
            pl.BlockSpec((tm, POOL_WIDTH), lambda i: (i, ucol)),
            pl.BlockSpec((HALO, POOL_WIDTH), lambda i: (jnp.maximum(i * (tm // HALO) - 1, 0), ucol)),
            pl.BlockSpec((None, len(POOL_WINDOWS), POOL_GW, POOL_GW), lambda i: (l, 0, 0, 0)),
            pl.BlockSpec((1, POOL_WIDTH), lambda i: (0, 0)),
        ],
        out_specs=pl.BlockSpec((tm, POOL_WIDTH), lambda i: (i, 0)),
        out_shape=jax.ShapeDtypeStruct((s, POOL_WIDTH), BF16),
        scratch_shapes=[pltpu.VMEM((HALO + tm, POOL_WIDTH), F32)],
        compiler_params=_cparams(("parallel",)),
        name="pool_mixer",
    )(z, z, w_pool, pool_scale.reshape(1, POOL_WIDTH))


def _merge_kernel(a_ref, p_ref, ml_ref, x_ref, wa_ref, wp_ref, wo_ref, g_ref, o_ref):
    pa = _dot(a_ref[...], wa_ref[...])
    pp = _dot(p_ref[...], wp_ref[...])
    ml = ml_ref[...]
    y = _sigmoid(ml[:, :D_MODEL]) * pa + _sigmoid(ml[:, D_MODEL:]) * pp
    y = _dot(y.astype(BF16), wo_ref[...])
    o_ref[...] = x_ref[...] + _rms(y, g_ref[...])


def merge_mix(a, p, z, x, wa, wp, wo, g, l, *, tm):
    s = x.shape[0]
    const = lambda i: (0, 0)
    layer = lambda i: (l, 0, 0)
    return pl.pallas_call(
        _merge_kernel,
        grid=(s // tm,),
        in_specs=[
            pl.BlockSpec((tm, NSA_WIDTH), lambda i: (i, 0)),
            pl.BlockSpec((tm, POOL_WIDTH), lambda i: (i, 0)),
            pl.BlockSpec((tm, 2 * D_MODEL), lambda i: (i, OFF_ML // (2 * D_MODEL))),
            pl.BlockSpec((tm, D_MODEL), lambda i: (i, 0)),
            pl.BlockSpec((None, NSA_WIDTH, D_MODEL), layer, pipeline_mode=pl.Buffered(1)),
            pl.BlockSpec((None, POOL_WIDTH, D_MODEL), layer, pipeline_mode=pl.Buffered(1)),
            pl.BlockSpec((None, D_MODEL, D_MODEL), layer, pipeline_mode=pl.Buffered(1)),
            pl.BlockSpec((1, D_MODEL), const),
        ],
        out_specs=pl.BlockSpec((tm, D_MODEL), lambda i: (i, 0)),
        out_shape=jax.ShapeDtypeStruct((s, D_MODEL), F32),
        compiler_params=_cparams(("parallel",)),
        name="merge_mix",
    )(a, p, z, x, wa, wp, wo, g.reshape(1, D_MODEL))


def _xattn_kernel(x_ref, gpre_ref, wq_ref, kv_ref, wo_ref, gpost_ref, o_ref):
    x = x_ref[...]
    h = _rms(x, gpre_ref[...]).astype(BF16)
    q = _dot(h, wq_ref[...]).astype(BF16)
    kv = kv_ref[...]
    outs = []
    for hh in range(X_HEADS):
        lo = hh * X_HEAD_DIM
        sc = _dot_nt(q[:, lo:lo + X_HEAD_DIM], kv[:, lo:lo + X_HEAD_DIM]) * (X_HEAD_DIM ** -0.5)
        m = jnp.max(sc, axis=-1, keepdims=True)
        e = jnp.exp(sc - m)
        p = (e / jnp.sum(e, axis=-1, keepdims=True)).astype(BF16)
        outs.append(_dot(p, kv[:, X_WIDTH + lo:X_WIDTH + lo + X_HEAD_DIM]))
    o = jnp.concatenate(outs, axis=-1).astype(BF16)
    y = _dot(o, wo_ref[...])
    o_ref[...] = x + _rms(y, gpost_ref[...])


def cross_attention(x, gpre, wq, memkv, wo, gpost, l, *, tm):
    s = x.shape[0]
    const = lambda i: (0, 0)
    layer = lambda i: (l, 0, 0)
    return pl.pallas_call(
        _xattn_kernel,
        grid=(s // tm,),
        in_specs=[
            pl.BlockSpec((tm, D_MODEL), lambda i: (i, 0)),
            pl.BlockSpec((1, D_MODEL), const),
            pl.BlockSpec((None, D_MODEL, X_WIDTH), layer),
            pl.BlockSpec((N_MEM, 2 * X_WIDTH), const),
            pl.BlockSpec((None, X_WIDTH, D_MODEL), layer),
            pl.BlockSpec((1, D_MODEL), const),
        ],
        out_specs=pl.BlockSpec((tm, D_MODEL), lambda i: (i, 0)),
        out_shape=jax.ShapeDtypeStruct((s, D_MODEL), F32),
        compiler_params=_cparams(("parallel",)),
        name="cross_attention",
    )(x, gpre.reshape(1, D_MODEL), wq, memkv, wo, gpost.reshape(1, D_MODEL))


def _ffn_kernel(x_ref, gpre_ref, wg_ref, wu_ref, wd_ref, gpost_ref, o_ref, h_ref, acc_ref):
    f = pl.program_id(1)

    @pl.when(f == 0)
    def _():
        h_ref[...] = _rms(x_ref[...], gpre_ref[...]).astype(BF16)
        acc_ref[...] = jnp.zeros_like(acc_ref)

    h = h_ref[...]
    a = _dot(h, wg_ref[...])
    b = _dot(h, wu_ref[...])
    t = (a * _sigmoid(a) * b).astype(BF16)
    acc_ref[...] += _dot(t, wd_ref[...])

    @pl.when(f == pl.num_programs(1) - 1)
    def _():
        o_ref[...] = x_ref[...] + _rms(acc_ref[...], gpost_ref[...])


def ffn(x, gpre, wg, wu, wd, gpost, l, *, tm, tf):
    s = x.shape[0]
    return pl.pallas_call(
        _ffn_kernel,
        grid=(s // tm, D_FF // tf),
        in_specs=[
            pl.BlockSpec((tm, D_MODEL), lambda i, f: (i, 0)),
            pl.BlockSpec((1, D_MODEL), lambda i, f: (0, 0)),
            pl.BlockSpec((None, D_MODEL, tf), lambda i, f: (l, 0, f)),
            pl.BlockSpec((None, D_MODEL, tf), lambda i, f: (l, 0, f)),
            pl.BlockSpec((None, tf, D_MODEL), lambda i, f: (l, f, 0)),
            pl.BlockSpec((1, D_MODEL), lambda i, f: (0, 0)),
        ],
        out_specs=pl.BlockSpec((tm, D_MODEL), lambda i, f: (i, 0)),
        out_shape=jax.ShapeDtypeStruct((s, D_MODEL), F32),
        scratch_shapes=[pltpu.VMEM((tm, D_MODEL), BF16), pltpu.VMEM((tm, D_MODEL), F32)],
        compiler_params=_cparams(("parallel", "arbitrary")),
        name="ffn_swiglu",
    )(x, gpre.reshape(1, D_MODEL), wg, wu, wd, gpost.reshape(1, D_MODEL))


def _tiles(s):
    big = s >= 4096
    return dict(
        tm_in=1024 if big else 256, tn_in=512,
        tk_sel=512 if big else 256,
        tm_pool=512 if big else 256,
        tm_merge=256,
        tm_x=512 if big else 256,
        tm_ffn=512 if big else 256, tf=512,
    )


def _pad_w_in(w_in):
    w = w_in.astype(BF16)
    n_gl = 3 * NSA_HEADS
    o_gl = NSA_WIDTH + 6 * KV_WIDTH
    o_u = o_gl + n_gl
    o_ml = o_u + POOL_WIDTH
    pad = jnp.zeros(w.shape[:2] + (GL_PAD - n_gl,), BF16)
    return jnp.concatenate(
        [w[..., o_ml:], w[..., o_u:o_ml], w[..., :o_gl], w[..., o_gl:o_u], pad], axis=-1)


def _forward(x, mem, rel_bias, ln_mix_pre, ln_mix_post, ln_x_pre, ln_x_post, ln_mem,
             ln_ffn_pre, ln_ffn_post, w_in, cmp_pe, cmp_w1, cmp_w2, w_pool, pool_scale,
             w_br_attn, w_br_pool, w_mix_out, w_xq, w_xkv, w_xo, w_gate, w_up, w_down):
    b, s, _ = x.shape
    assert b == 1 and s % 1024 == 0
    depth = w_in.shape[0]
    tl = _tiles(s)
    nc = s // CMP_STRIDE
    nq = s // Q_BLK
    half = CMP_STRIDE * HEAD_DIM
    ts, wb, bc = bias_tables(rel_bias, s=s, tk=tl["tk_sel"])
    key_aug = jnp.asarray(_key_aug_lanes(s, tl["tk_sel"]))
    v_ones = jnp.zeros((NSA_GROUPS, V_ROWS - HEAD_DIM, s), BF16).at[:, 0, :].set(1.0)
    w_in_b = _pad_w_in(w_in)
    pe_b = cmp_pe.reshape(depth, 2, 2, 1, half)
    w1_b = cmp_w1.reshape(depth, 2, 2, half, CMP_HID).astype(BF16)
    w2_b, w_pool_b = cmp_w2.astype(BF16), w_pool.astype(BF16)
    wa_b, wp_b, wo_b = w_br_attn.astype(BF16), w_br_pool.astype(BF16), w_mix_out.astype(BF16)
    wxq_b, wxkv_b, wxo_b = w_xq.astype(BF16), w_xkv.astype(BF16), w_xo.astype(BF16)
    wg_b, wu_b, wd_b = w_gate.astype(BF16), w_up.astype(BF16), w_down.astype(BF16)
    xc = x[0]
    mem2 = mem[0]
    for l in range(depth):
        z = norm_matmul(xc, ln_mix_pre[l], w_in_b, l, tm=tl["tm_in"], tn=tl["tn_in"])
        q = z[:, OFF_Q:OFF_Q + NSA_WIDTH] * ((HEAD_DIM ** -0.5) * LOG2E)
        qt = q.reshape(nq, Q_BLK, NSA_GROUPS, NSA_HPG, HEAD_DIM).transpose(2, 0, 4, 3, 1)
        qt = qt.reshape(NSA_GROUPS, nq, HEAD_DIM, NSA_HPG * Q_BLK).astype(BF16)
        kv = z[:, OFF_KV:OFF_KV + 6 * KV_WIDTH].reshape(s, 6, NSA_GROUPS, HEAD_DIM)
        c = kv[:, 0:2].reshape(nc, CMP_STRIDE, 2, NSA_GROUPS, HEAD_DIM)
        c = c.transpose(2, 3, 0, 1, 4).reshape(2, NSA_GROUPS, nc, half)
        kvc = compress(c, pe_b, w1_b, w2_b, l, s=s)
        kvc = jnp.transpose(kvc, (1, 0, 2, 3))
        kvc_pad = jnp.pad(kvc, ((0, 0), (0, 0), (nc - 8, 8), (0, LANES - HEAD_DIM)))
        kvc_pad = kvc_pad.at[:, 0, :nc - 8, HEAD_DIM].set(1.0)
        ks = jnp.concatenate([kv[:, 2], jnp.broadcast_to(key_aug[:, None, :], (s, NSA_GROUPS, HEAD_DIM))],
                             axis=-1).transpose(1, 0, 2).astype(BF16)
        kw = jnp.concatenate([kv[:, 4], kv[:, 5]], axis=-1).transpose(1, 0, 2).astype(BF16)
        vst = jnp.concatenate([kv[:, 3].transpose(1, 2, 0).astype(BF16), v_ones], axis=1)
        vwt = jnp.concatenate([kv[:, 5].transpose(1, 2, 0).astype(BF16), v_ones], axis=1)
        glt = z[:, OFF_GL:OFF_GL + 3 * NSA_HEADS].reshape(s, NSA_GROUPS, 3 * NSA_HPG).transpose(1, 2, 0)
        a = nsa_attention(qt, kvc_pad, ks, vst, kw, vwt, glt, ts, wb, bc, s=s, tk=tl["tk_sel"])
        p = pool_mixer(z, w_pool_b, pool_scale[l], l, tm=tl["tm_pool"])
        xc = merge_mix(a, p, z, xc, wa_b, wp_b, wo_b, ln_mix_post[l], l, tm=tl["tm_merge"])
        memkv = norm_matmul(mem2, ln_mem[l], wxkv_b, l, tm=N_MEM, tn=512, out_dtype=BF16)
        xc = cross_attention(xc, ln_x_pre[l], wxq_b, memkv, wxo_b, ln_x_post[l], l, tm=tl["tm_x"])
        xc = ffn(xc, ln_ffn_pre[l], wg_b, wu_b, wd_b, ln_ffn_post[l], l, tm=tl["tm_ffn"], tf=tl["tf"])
    return xc[None]


def kernel(x, mem, rel_bias, ln_mix_pre, ln_mix_post, ln_x_pre, ln_x_post, ln_mem, ln_ffn_pre,
           ln_ffn_post, w_in, cmp_pe, cmp_w1, cmp_w2, w_pool, pool_scale, w_br_attn, w_br_pool,
           w_mix_out, w_xq, w_xkv, w_xo, w_gate, w_up, w_down):
    return _forward(x, mem, rel_bias, ln_mix_pre, ln_mix_post, ln_x_pre, ln_x_post, ln_mem,
                    ln_ffn_pre, ln_ffn_post, w_in, cmp_pe, cmp_w1, cmp_w2, w_pool, pool_scale,
                    w_br_attn, w_br_pool, w_mix_out, w_xq, w_xkv, w_xo, w_gate, w_up, w_down)
```

```python
import functools
import math

import numpy as np
import jax
import jax.numpy as jnp
from jax import lax
from jax.experimental import pallas as pl
from jax.experimental.pallas import tpu as pltpu

F32 = jnp.float32
BF16 = jnp.bfloat16

D_MODEL = 2048
DEPTH = 4
N_MEM = 256
NSA_HEADS = 16
NSA_GROUPS = 4
NSA_HPG = 4
HEAD_DIM = 64
NSA_WIDTH = 1024
KV_WIDTH = 256
CMP_LEN = 32
CMP_STRIDE = 16
CMP_HID = 128
SEL_BLK = 64
SEL_TOPN = 16
WIN = 512
Q_BLK = 128
FORCE_SCORE = 1e4
POOL_WINDOWS = (2, 4, 8, 16)
POOL_GW = 256
POOL_WIDTH = 1024
REL_BUCKETS = 32
REL_MAX_DIST = 2048
X_HEADS = 4
X_HEAD_DIM = 128
X_WIDTH = 512
D_FF = 5632
NEG = -1e30
LOG2E = 1.4426950408889634

GL_PAD = 512
OFF_ML, OFF_U, OFF_Q, OFF_KV, OFF_GL = 0, 4096, 5120, 6144, 7680
IN_PAD = 8192

VMEM_LIMIT_V7X = 56 * 1024 * 1024
LANES = 128


def _cparams(sem, flags=None):
    return pltpu.CompilerParams(dimension_semantics=sem, vmem_limit_bytes=VMEM_LIMIT_V7X, flags=flags)


def _rms(x, g):
    ms = jnp.mean(x * x, axis=-1, keepdims=True)
    return x * lax.rsqrt(ms + 1e-6) * g


def _sigmoid(x):
    return 1.0 / (1.0 + jnp.exp(-x))


def _dot(a, b):
    return jnp.dot(a, b, preferred_element_type=F32)


def _dot_nt(a, b):
    return lax.dot_general(a, b, (((1,), (1,)), ((), ())), preferred_element_type=F32)


def _norm_matmul_kernel(x_ref, g_ref, w_ref, o_ref, h_ref):
    @pl.when(pl.program_id(1) == 0)
    def _():
        h_ref[...] = _rms(x_ref[...], g_ref[...]).astype(BF16)

    o_ref[...] = _dot(h_ref[...], w_ref[...]).astype(o_ref.dtype)


def norm_matmul(x, g, w, l, *, tm, tn, out_dtype=F32):
    m, k = x.shape
    n = w.shape[2]
    return pl.pallas_call(
        _norm_matmul_kernel,
        grid=(m // tm, n // tn),
        in_specs=[
            pl.BlockSpec((tm, k), lambda i, j: (i, 0)),
            pl.BlockSpec((1, k), lambda i, j: (0, 0)),
            pl.BlockSpec((None, k, tn), lambda i, j: (l, 0, j)),
        ],
        out_specs=pl.BlockSpec((tm, tn), lambda i, j: (i, j)),
        out_shape=jax.ShapeDtypeStruct((m, n), out_dtype),
        scratch_shapes=[pltpu.VMEM((tm, k), BF16)],
        compiler_params=_cparams(("parallel", "arbitrary")),
        name="norm_matmul",
    )(x, g.reshape(1, k), w)


def _bucket(d):
    n = jnp.maximum(d, 0)
    exact = REL_BUCKETS // 2
    nf = jnp.maximum(n, 1).astype(F32)
    large = exact + (jnp.log(nf / exact) / math.log(REL_MAX_DIST / exact)
                     * (REL_BUCKETS - exact)).astype(jnp.int32)
    return jnp.where(n < exact, n, jnp.minimum(large, REL_BUCKETS - 1))


def _lookup(tab_ref, h, bk):
    out = jnp.full(bk.shape, tab_ref[h, 0], F32)
    for b in range(1, REL_BUCKETS):
        out = jnp.where(bk == b, tab_ref[h, b], out)
    return out


def _bias_tables_kernel(tab_ref, ts_ref, wb_ref, bc_ref, *, ls, off, lw, ncw, cmp_shift):
    h = pl.program_id(0)
    r = lax.broadcasted_iota(jnp.int32, (LANES, Q_BLK), 0)
    qi = lax.broadcasted_iota(jnp.int32, (LANES, Q_BLK), 1)

    def chunked(ref, n_rows, dist, valid):
        def body(ch, carry):
            d = dist(ch * LANES + r)
            val = _lookup(tab_ref, h, _bucket(d))
            ref[0, pl.ds(pl.multiple_of(ch * LANES, LANES), LANES), :] = jnp.where(valid(d), val * LOG2E, NEG)
            return carry
        lax.fori_loop(0, n_rows // LANES, body, 0)

    chunked(ts_ref, ls, lambda c: qi - c + off, lambda d: d >= 0)
    chunked(wb_ref, lw, lambda c: qi - c + WIN, lambda d: (d >= 0) & (d < WIN))
    chunked(bc_ref, ncw, lambda m: qi - CMP_STRIDE * m + cmp_shift, lambda d: d >= 0)


def _table_sizes(s, tk):
    off = REL_MAX_DIST + tk
    ls = off + 3 * tk
    lw = 2 * WIN + Q_BLK
    return off, ls, lw


def bias_tables(rel_bias, *, s, tk):
    off, ls, lw = _table_sizes(s, tk)
    ncw = s // CMP_STRIDE
    cmp_shift = CMP_STRIDE * (ncw - 8) - (CMP_LEN - 1)
    kern = functools.partial(_bias_tables_kernel, ls=ls, off=off, lw=lw, ncw=ncw, cmp_shift=cmp_shift)
    return pl.pallas_call(
        kern,
        grid=(NSA_HEADS,),
        in_specs=[pl.BlockSpec(memory_space=pltpu.SMEM)],
        out_specs=[
            pl.BlockSpec((1, ls, Q_BLK), lambda h: (h, 0, 0)),
            pl.BlockSpec((1, lw, Q_BLK), lambda h: (h, 0, 0)),
            pl.BlockSpec((1, ncw, Q_BLK), lambda h: (h, 0, 0)),
        ],
        out_shape=[
            jax.ShapeDtypeStruct((NSA_HEADS, ls, Q_BLK), F32),
            jax.ShapeDtypeStruct((NSA_HEADS, lw, Q_BLK), F32),
            jax.ShapeDtypeStruct((NSA_HEADS, ncw, Q_BLK), F32),
        ],
        compiler_params=_cparams(("arbitrary",)),
        name="bias_tables",
    )(rel_bias)


def _gelu_tanh(x):
    return 0.5 * x * (1.0 + jnp.tanh(math.sqrt(2.0 / math.pi) * (x + 0.044715 * x * x * x)))


def _compress_kernel(c_ref, pe_ref, w1_ref, w2_ref, o_ref, scr_ref, *, nc):
    c = c_ref[0, 0]
    ca = (c + pe_ref[0, 0]).astype(BF16)
    cb = (c + pe_ref[0, 1]).astype(BF16)
    h1 = _dot(ca, w1_ref[0, 0])
    h2 = _dot(cb, w1_ref[0, 1])
    scr_ref[pl.ds(0, nc), :] = h2
    scr_ref[pl.ds(nc, 8), :] = jnp.zeros((8, CMP_HID), F32)
    hid = h1 + scr_ref[pl.ds(1, nc), :]
    out = _dot(_gelu_tanh(hid).astype(BF16), w2_ref[0])
    row = lax.broadcasted_iota(jnp.int32, out.shape, 0)
    o_ref[0, 0] = jnp.where(row < nc - 1, out, 0.0)


def compress(c, pe, w1, w2, l, *, s):
    nc = s // CMP_STRIDE
    half = CMP_STRIDE * HEAD_DIM
    return pl.pallas_call(
        functools.partial(_compress_kernel, nc=nc),
        grid=(2, NSA_GROUPS),
        in_specs=[
            pl.BlockSpec((1, 1, nc, half), lambda a, g: (a, g, 0, 0)),
            pl.BlockSpec((None, 1, 2, 1, half), lambda a, g: (l, a, 0, 0, 0)),
            pl.BlockSpec((None, 1, 2, half, CMP_HID), lambda a, g: (l, a, 0, 0, 0)),
            pl.BlockSpec((None, 1, CMP_HID, HEAD_DIM), lambda a, g: (l, a, 0, 0)),
        ],
        out_specs=pl.BlockSpec((1, 1, nc, HEAD_DIM), lambda a, g: (a, g, 0, 0)),
        out_shape=jax.ShapeDtypeStruct((2, NSA_GROUPS, nc, HEAD_DIM), F32),
        scratch_shapes=[pltpu.VMEM((nc + 8, CMP_HID), F32)],
        compiler_params=_cparams(("arbitrary", "arbitrary")),
        name="compress_kv",
    )(c, pe, w1, w2)


V_ROWS = HEAD_DIM + 16


def _lanes4(fn):
    return jnp.concatenate([fn(hh) for hh in range(NSA_HPG)], axis=-1)


def _nsa_kernel(q_ref, kvc_ref, ks_ref, vs_ref, kw_ref, vw_ref, gl_ref, ts_ref, wb_ref, bc_ref,
                ov_ref, o_ref, mask_ref, sa_ref, sb_ref, sw_ref, *, tk, ncw, nsel, off, nb):
    rows1 = NSA_HPG * Q_BLK
    rows = nb * rows1
    ib = [nb * pl.program_id(1) + b for b in range(nb)]
    q0b = [i * Q_BLK for i in ib]

    def blk(b):
        return slice(b * rows1, (b + 1) * rows1)

    def per_block(fn):
        return jnp.concatenate([fn(b) for b in range(nb)], axis=-1)

    def table_rows(ref, row0, n):
        return per_block(lambda b: _lanes4(lambda hh: ref[hh, pl.ds(row0(b), n), :]))

    qt = per_block(lambda b: q_ref[0, b])
    qa = jnp.concatenate([qt, jnp.zeros_like(qt)], axis=0)

    wk = WIN + Q_BLK
    startb = [pl.multiple_of(jnp.maximum(q0 - WIN, 0), Q_BLK) for q0 in q0b]
    for b in range(nb):
        sw_ref[:, blk(b)] = _dot(kw_ref[0, pl.ds(startb[b], wk), :], qa[:, blk(b)])

    tc = min(tk, ncw)
    neg_row = jnp.where(lax.broadcasted_iota(jnp.int32, (AUG_ROWS, rows), 0) == 0, NEG, 0.0).astype(BF16)
    qa_c = jnp.concatenate([qt, neg_row, jnp.zeros((LANES - HEAD_DIM - AUG_ROWS, rows), BF16)], axis=0)
    ones_rows = (lax.broadcasted_iota(jnp.int32, (V_ROWS - HEAD_DIM, tc), 0) == 0).astype(BF16)

    def cmp_rows(b, t):
        return pl.ds(pl.multiple_of(8 * ib[b] + t * tc, 8), tc)

    def cmp_scores(buf, t):
        for b in range(nb):
            buf[pl.ds(0, tc), blk(b)] = _dot(kvc_ref[0, 0, cmp_rows(b, t), :].astype(BF16), qa_c[:, blk(b)])

    def cmp_tile(buf, t, carry):
        m, acc, impa = carry
        s = buf[pl.ds(0, tc), :] + table_rows(bc_ref, lambda b: t * tc, tc)
        m_new = jnp.maximum(m, jnp.max(s, axis=0, keepdims=True))
        alpha = jnp.exp2(m - m_new)
        p = jnp.exp2(s - m_new).astype(BF16)

        def pv(b):
            vc = kvc_ref[0, 1, cmp_rows(b, t), :]
            vct = jnp.concatenate([vc.T[:HEAD_DIM, :].astype(BF16), ones_rows], axis=0)
            return _dot(vct, p[:, blk(b)])

        return (m_new, alpha * acc + per_block(pv),
                alpha * impa + _dot(ov_ref[:, t * tc:(t + 1) * tc], p))

    cmp_bufs = (sa_ref, sb_ref)
    n_ct = ncw // tc
    cmp_scores(cmp_bufs[0], 0)

    carry_c = (jnp.full((1, rows), NEG, F32), jnp.zeros((V_ROWS, rows), F32), jnp.zeros((nsel, rows), F32))
    for t in range(n_ct):
        if t + 1 < n_ct:
            cmp_scores(cmp_bufs[(t + 1) % 2], t + 1)
        carry_c = cmp_tile(cmp_bufs[t % 2], t, carry_c)
    m_c, acc_c, imp_c = carry_c
    inv = jnp.where(m_c > 0.5 * NEG, 1.0 / acc_c[HEAD_DIM:HEAD_DIM + 1, :], 0.0)
    o_c = acc_c[:HEAD_DIM, :] * inv
    imp4 = imp_c * inv

    def head_sum(b):
        c = [imp4[:, b * rows1 + hh * Q_BLK:b * rows1 + (hh + 1) * Q_BLK] for hh in range(NSA_HPG)]
        return (c[0] + c[1]) + (c[2] + c[3])
    imp = per_block(head_sum)

    s_w = sw_ref[...] + table_rows(wb_ref, lambda b: pl.multiple_of(WIN - (q0b[b] - startb[b]), Q_BLK), wk)
    m_w = jnp.max(s_w, axis=0, keepdims=True)
    e_w = jnp.exp2(s_w - m_w).astype(BF16)
    oa_w = per_block(lambda b: _dot(vw_ref[0, :, pl.ds(startb[b], wk)], e_w[:, blk(b)]))
    o_w = oa_w[:HEAD_DIM, :] * (1.0 / oa_w[HEAD_DIM:HEAD_DIM + 1, :])

    nq = nb * Q_BLK
    jp = lax.broadcasted_iota(jnp.int32, (nsel, nq), 0)
    lane = lax.broadcasted_iota(jnp.int32, (nsel, nq), 1)
    i_lane = nb * pl.program_id(1) + (lane >> 7)
    jabs = jp + (2 * i_lane - (nsel - 2))
    cur = 2 * i_lane + ((lane & (Q_BLK - 1)) >= SEL_BLK).astype(jnp.int32)
    val = jnp.where(jabs == cur - 1, FORCE_SCORE, imp)
    val = jnp.where(jabs == cur, 2.0 * FORCE_SCORE, val)
    val = jnp.where(jabs == 0, 3.0 * FORCE_SCORE, val)
    val = jnp.where(jabs > cur, -1.0, val)
    val = jnp.where(jabs < 0, -3.0, val)

    work = val
    for _ in range(SEL_TOPN):
        work = jnp.where(work == jnp.max(work, axis=0, keepdims=True), -1e38, work)
    fast = jnp.where((work < -1e37) & (val >= 0.0), 1.0, 0.0)
    tie_seen = jnp.max(jnp.sum(fast, axis=0, keepdims=True)) > SEL_TOPN + 0.5

    def exact_topn():
        wk_, chosen = val, jnp.zeros((nsel, nq), F32)
        jpf = jp.astype(F32)
        for _ in range(SEL_TOPN):
            mx = jnp.max(wk_, axis=0, keepdims=True)
            jmin = jnp.min(jnp.where(wk_ == mx, jpf, 2.0 * nsel), axis=0, keepdims=True)
            hit = jpf == jmin
            chosen = jnp.where(hit, 1.0, chosen)
            wk_ = jnp.where(hit, -1e38, wk_)
        return jnp.where(val >= 0.0, chosen, 0.0)

    sel = lax.cond(tie_seen, exact_topn, lambda: fast).astype(BF16)
    a_row = lax.broadcasted_iota(jnp.int32, (nsel, nsel), 0)
    s_col = lax.broadcasted_iota(jnp.int32, (nsel, nsel), 1)
    for b in range(nb):
        shift = jnp.where(s_col == a_row - (2 * ib[b] - (nsel - 2)), 1.0, 0.0).astype(BF16)
        mask_abs = (_dot(shift, sel[:, b * Q_BLK:(b + 1) * Q_BLK]) - 1.0) * (-NEG)
        mask_ref[:, blk(b)] = jnp.concatenate([mask_abs] * NSA_HPG, axis=-1)

    nblk = tk // SEL_BLK
    far_bias = table_rows(ts_ref, lambda b: 0, 1)
    far_hi = far_bias.astype(BF16).astype(F32)
    far_lo = (far_bias - far_hi).astype(BF16).astype(F32)
    far_lo2 = (far_bias - far_hi - far_lo).astype(BF16).astype(F32)
    far_rows = jnp.concatenate([far_hi, far_lo, far_lo2, jnp.zeros((8 - FAR_TERMS, rows), F32)], axis=0)
    n_pad = AUG_ROWS - nblk - 8
    aug_pad = [jnp.zeros((n_pad, rows), F32)] if n_pad else []
    rhs_tail = jnp.zeros((LANES - HEAD_DIM - AUG_ROWS, rows), BF16)
    n_tiles = (q0b[-1] + Q_BLK + tk - 1) // tk
    n_pairs = (n_tiles + 1) // 2
    n_far_pairs = (jnp.maximum(q0b[0] - off + tk, 0) // tk) // (2 * PAIRS_PER_ITER) * PAIRS_PER_ITER
    last_tile = ks_ref.shape[1] // tk - 1

    def scores_into(buf, kt):
        kt = jnp.minimum(kt, last_tile)
        k0 = pl.multiple_of(kt * tk, tk)
        slab = mask_ref[pl.ds(pl.multiple_of(kt * nblk, nblk), nblk), :]
        far = jnp.where(kt < 2 * n_far_pairs, far_rows, 0.0)
        aug = jnp.concatenate([slab, far] + aug_pad, axis=0).astype(BF16)
        rhs = jnp.concatenate([qt, aug, rhs_tail], axis=0)
        buf[...] = _dot(ks_ref[0, pl.ds(k0, tk), :], rhs)

    def sel_tile(buf, kt, carry, near):
        m, acc = carry
        k0 = pl.multiple_of(kt * tk, tk)
        s = buf[...]
        vt = vs_ref[0, :, pl.ds(k0, tk)]
        if near:
            s = s + table_rows(
                ts_ref, lambda b: pl.multiple_of(jnp.maximum(off - (q0b[b] - k0), 0), Q_BLK), tk)
        m_new = jnp.maximum(m, jnp.max(s, axis=0, keepdims=True))
        p = jnp.exp2(s - m_new)
        acc = jnp.exp2(m - m_new) * acc + _dot(vt, p.astype(BF16))
        return m_new, acc

    def sel_pairs(it, carry, near, pairs, base):
        for j in range(pairs):
            kt = 2 * (base + pairs * it + j)
            scores_into(sb_ref, kt + 1)
            carry = sel_tile(sa_ref, kt, carry, near)
            scores_into(sa_ref, kt + 2)
            carry = sel_tile(sb_ref, kt + 1, carry, near)
        return carry

    scores_into(sa_ref, 0)
    carry = (jnp.full((1, rows), NEG, F32), jnp.zeros((V_ROWS, rows), F32))
    n_near_long = (n_pairs - n_far_pairs) // PAIRS_PER_ITER
    carry = lax.fori_loop(0, n_far_pairs // PAIRS_PER_ITER,
                          functools.partial(sel_pairs, near=False, pairs=PAIRS_PER_ITER, base=0), carry)
    carry = lax.fori_loop(0, n_near_long,
                          functools.partial(sel_pairs, near=True, pairs=PAIRS_PER_ITER, base=n_far_pairs), carry)
    _, acc_s = lax.fori_loop(n_far_pairs + PAIRS_PER_ITER * n_near_long, n_pairs,
                             functools.partial(sel_pairs, near=True, pairs=1, base=0), carry)
    o_s = acc_s[:HEAD_DIM, :] * (1.0 / acc_s[HEAD_DIM:HEAD_DIM + 1, :])

    gates = _sigmoid(gl_ref[0])
    def gate_row(c):
        return per_block(lambda b: _lanes4(
            lambda hh: gates[3 * hh + c:3 * hh + c + 1, b * Q_BLK:(b + 1) * Q_BLK]))
    o = gate_row(0) * o_c + gate_row(1) * o_s + gate_row(2) * o_w
    for b in range(nb):
        ob = jnp.concatenate([o[:, b * rows1 + hh * Q_BLK:b * rows1 + (hh + 1) * Q_BLK]
                              for hh in range(NSA_HPG)], axis=0)
        o_ref[b * Q_BLK:(b + 1) * Q_BLK, :] = ob.T.astype(o_ref.dtype)


PAIRS_PER_ITER = 2
FAR_TERMS = 3
AUG_ROWS = 16


def _key_aug_lanes(s, tk):
    nblk = tk // SEL_BLK
    aug = np.zeros((s, HEAD_DIM), np.float32)
    aug[np.arange(s), (np.arange(s) % tk) // SEL_BLK] = 1.0
    aug[:, nblk:nblk + FAR_TERMS] = 1.0
    return aug


def _overlap_matrix_t(ncw, nsel):
    m = np.arange(ncw)[None, :]
    j = np.arange(nsel)[:, None]
    lo = np.maximum(m * CMP_STRIDE, j * SEL_BLK)
    hi = np.minimum(m * CMP_STRIDE + CMP_LEN, (j + 1) * SEL_BLK)
    return (np.maximum(hi - lo, 0).astype(np.float32) / CMP_LEN)


def nsa_attention(qt, kvc_pad, ks, vst, kw, vwt, glt, ts, wb, bc, *, s, tk, nb):
    ncw = s // CMP_STRIDE
    nsel = s // SEL_BLK
    off, ls, lw = _table_sizes(s, tk)
    rows1 = NSA_HPG * Q_BLK
    rows = nb * rows1
    ov = jnp.asarray(_overlap_matrix_t(ncw, nsel), BF16)
    kern = functools.partial(_nsa_kernel, tk=tk, ncw=ncw, nsel=nsel, off=off, nb=nb)
    per_group = dict(pipeline_mode=pl.Buffered(1))
    return pl.pallas_call(
        kern,
        grid=(NSA_GROUPS, s // (nb * Q_BLK)),
        in_specs=[
            pl.BlockSpec((1, nb, HEAD_DIM, rows1), lambda g, i: (g, i, 0, 0)),
            pl.BlockSpec((1, 2, 2 * ncw, LANES), lambda g, i: (g, 0, 0, 0), **per_group),
            pl.BlockSpec((1, s, LANES), lambda g, i: (g, 0, 0), **per_group),
            pl.BlockSpec((1, V_ROWS, s), lambda g, i: (g, 0, 0), **per_group),
            pl.BlockSpec((1, s, LANES), lambda g, i: (g, 0, 0), **per_group),
            pl.BlockSpec((1, V_ROWS, s), lambda g, i: (g, 0, 0), **per_group),
            pl.BlockSpec((1, 3 * NSA_HPG, nb * Q_BLK), lambda g, i: (g, 0, i)),
            pl.BlockSpec((NSA_HPG, ls, Q_BLK), lambda g, i: (g, 0, 0), **per_group),
            pl.BlockSpec((NSA_HPG, lw, Q_BLK), lambda g, i: (g, 0, 0), **per_group),
            pl.BlockSpec((NSA_HPG, ncw, Q_BLK), lambda g, i: (g, 0, 0), **per_group),
            pl.BlockSpec((nsel, ncw), lambda g, i: (0, 0), **per_group),
        ],
        out_specs=pl.BlockSpec((nb * Q_BLK, NSA_HPG * HEAD_DIM), lambda g, i: (i, g)),
        out_shape=jax.ShapeDtypeStruct((s, NSA_WIDTH), BF16),
        scratch_shapes=[pltpu.VMEM((nsel, rows), F32), pltpu.VMEM((tk, rows), F32),
                        pltpu.VMEM((tk, rows), F32), pltpu.VMEM((WIN + Q_BLK, rows), F32)],
        compiler_params=_cparams(("arbitrary", "arbitrary")),
        name="nsa_attention",
    )(qt, kvc_pad, ks, vst, kw, vwt, glt, ts, wb, bc, ov)


HALO = 16


def _pool_kernel(u_ref, halo_ref, w_ref, sc_ref, o_ref, ext_ref, *, tm):
    i = pl.program_id(0)
    halo = jnp.where(i > 0, halo_ref[...], 0.0)
    ext_ref[pl.ds(0, HALO), :] = halo
    ext_ref[pl.ds(HALO, tm), :] = u_ref[...]
    t = i * tm + lax.broadcasted_iota(jnp.int32, (tm, 1), 0)
    outs = []
    for gi, w in enumerate(POOL_WINDOWS):
        cols = pl.ds(gi * POOL_GW, POOL_GW)
        x = ext_ref[pl.ds(HALO, tm), cols]
        acc = x
        for k in range(1, w):
            acc = acc + ext_ref[pl.ds(HALO - k, tm), cols]
        cnt = jnp.minimum(t + 1, w).astype(F32)
        y = acc / cnt - x
        outs.append(_dot(y.astype(BF16), w_ref[gi]))
    o_ref[...] = (jnp.concatenate(outs, axis=-1) * sc_ref[...]).astype(o_ref.dtype)


def pool_mixer(z, w_pool, pool_scale, l, *, tm):
    s = z.shape[0]
    ucol = OFF_U // POOL_WIDTH
    return pl.pallas_call(
        functools.partial(_pool_kernel, tm=tm),
        grid=(s // tm,),
        in_specs=[
            pl.BlockSpec((tm, POOL_WIDTH), lambda i: (i, ucol)),
            pl.BlockSpec((HALO, POOL_WIDTH), lambda i: (jnp.maximum(i * (tm // HALO) - 1, 0), ucol)),
            pl.BlockSpec((None, len(POOL_WINDOWS), POOL_GW, POOL_GW), lambda i: (l, 0, 0, 0)),
            pl.BlockSpec((1, POOL_WIDTH), lambda i: (0, 0)),
        ],
        out_specs=pl.BlockSpec((tm, POOL_WIDTH), lambda i: (i, 0)),
        out_shape=jax.ShapeDtypeStruct((s, POOL_WIDTH), BF16),
        scratch_shapes=[pltpu.VMEM((HALO + tm, POOL_WIDTH), F32)],
        compiler_params=_cparams(("parallel",)),
        name="pool_mixer",
    )(z, z, w_pool, pool_scale.reshape(1, POOL_WIDTH))


def _merge_kernel(a_ref, p_ref, ml_ref, x_ref, wa_ref, wp_ref, wo_ref, g_ref, o_ref):
    pa = _dot(a_ref[...], wa_ref[...])
    pp = _dot(p_ref[...], wp_ref[...])
    ml = ml_ref[...]
    y = _sigmoid(ml[:, :D_MODEL]) * pa + _sigmoid(ml[:, D_MODEL:]) * pp
    y = _dot(y.astype(BF16), wo_ref[...])
    o_ref[...] = x_ref[...] + _rms(y, g_ref[...])


def merge_mix(a, p, z, x, wa, wp, wo, g, l, *, tm):
    s = x.shape[0]
    const = lambda i: (0, 0)
    layer = lambda i: (l, 0, 0)
    return pl.pallas_call(
        _merge_kernel,
        grid=(s // tm,),
        in_specs=[
            pl.BlockSpec((tm, NSA_WIDTH), lambda i: (i, 0)),
            pl.BlockSpec((tm, POOL_WIDTH), lambda i: (i, 0)),
            pl.BlockSpec((tm, 2 * D_MODEL), lambda i: (i, OFF_ML // (2 * D_MODEL))),
            pl.BlockSpec((tm, D_MODEL), lambda i: (i, 0)),
            pl.BlockSpec((None, NSA_WIDTH, D_MODEL), layer, pipeline_mode=pl.Buffered(1)),
            pl.BlockSpec((None, POOL_WIDTH, D_MODEL), layer, pipeline_mode=pl.Buffered(1)),
            pl.BlockSpec((None, D_MODEL, D_MODEL), layer, pipeline_mode=pl.Buffered(1)),
            pl.BlockSpec((1, D_MODEL), const),
        ],
        out_specs=pl.BlockSpec((tm, D_MODEL), lambda i: (i, 0)),
        out_shape=jax.ShapeDtypeStruct((s, D_MODEL), F32),
        compiler_params=_cparams(("parallel",)),
        name="merge_mix",
    )(a, p, z, x, wa, wp, wo, g.reshape(1, D_MODEL))


def _xattn_kernel(x_ref, gpre_ref, wq_ref, kv_ref, wo_ref, gpost_ref, o_ref):
    x = x_ref[...]
    h = _rms(x, gpre_ref[...]).astype(BF16)
    q = _dot(h, wq_ref[...]).astype(BF16)
    kv = kv_ref[...]
    outs = []
    for hh in range(X_HEADS):
        lo = hh * X_HEAD_DIM
        sc = _dot_nt(q[:, lo:lo + X_HEAD_DIM], kv[:, lo:lo + X_HEAD_DIM]) * (X_HEAD_DIM ** -0.5)
        m = jnp.max(sc, axis=-1, keepdims=True)
        e = jnp.exp(sc - m)
        p = (e / jnp.sum(e, axis=-1, keepdims=True)).astype(BF16)
        outs.append(_dot(p, kv[:, X_WIDTH + lo:X_WIDTH + lo + X_HEAD_DIM]))
    o = jnp.concatenate(outs, axis=-1).astype(BF16)
    y = _dot(o, wo_ref[...])
    o_ref[...] = x + _rms(y, gpost_ref[...])


def cross_attention(x, gpre, wq, memkv, wo, gpost, l, *, tm):
    s = x.shape[0]
    const = lambda i: (0, 0)
    layer = lambda i: (l, 0, 0)
    return pl.pallas_call(
        _xattn_kernel,
        grid=(s // tm,),
        in_specs=[
            pl.BlockSpec((tm, D_MODEL), lambda i: (i, 0)),
            pl.BlockSpec((1, D_MODEL), const),
            pl.BlockSpec((None, D_MODEL, X_WIDTH), layer),
            pl.BlockSpec((N_MEM, 2 * X_WIDTH), const),
            pl.BlockSpec((None, X_WIDTH, D_MODEL), layer),
            pl.BlockSpec((1, D_MODEL), const),
        ],
        out_specs=pl.BlockSpec((tm, D_MODEL), lambda i: (i, 0)),
        out_shape=jax.ShapeDtypeStruct((s, D_MODEL), F32),
        compiler_params=_cparams(("parallel",)),
        name="cross_attention",
    )(x, gpre.reshape(1, D_MODEL), wq, memkv, wo, gpost.reshape(1, D_MODEL))


def _ffn_kernel(x_ref, gpre_ref, wg_ref, wu_ref, wd_ref, gpost_ref, o_ref, h_ref, acc_ref):
    f = pl.program_id(1)

    @pl.when(f == 0)
    def _():
        h_ref[...] = _rms(x_ref[...], gpre_ref[...]).astype(BF16)
        acc_ref[...] = jnp.zeros_like(acc_ref)

    h = h_ref[...]
    a = _dot(h, wg_ref[...])
    b = _dot(h, wu_ref[...])
    t = (a * _sigmoid(a) * b).astype(BF16)
    acc_ref[...] += _dot(t, wd_ref[...])

    @pl.when(f == pl.num_programs(1) - 1)
    def _():
        o_ref[...] = x_ref[...] + _rms(acc_ref[...], gpost_ref[...])


def ffn(x, gpre, wg, wu, wd, gpost, l, *, tm, tf):
    s = x.shape[0]
    return pl.pallas_call(
        _ffn_kernel,
        grid=(s // tm, D_FF // tf),
        in_specs=[
            pl.BlockSpec((tm, D_MODEL), lambda i, f: (i, 0)),
            pl.BlockSpec((1, D_MODEL), lambda i, f: (0, 0)),
            pl.BlockSpec((None, D_MODEL, tf), lambda i, f: (l, 0, f)),
            pl.BlockSpec((None, D_MODEL, tf), lambda i, f: (l, 0, f)),
            pl.BlockSpec((None, tf, D_MODEL), lambda i, f: (l, f, 0)),
            pl.BlockSpec((1, D_MODEL), lambda i, f: (0, 0)),
        ],
        out_specs=pl.BlockSpec((tm, D_MODEL), lambda i, f: (i, 0)),
        out_shape=jax.ShapeDtypeStruct((s, D_MODEL), F32),
        scratch_shapes=[pltpu.VMEM((tm, D_MODEL), BF16), pltpu.VMEM((tm, D_MODEL), F32)],
        compiler_params=_cparams(("parallel", "arbitrary")),
        name="ffn_swiglu",
    )(x, gpre.reshape(1, D_MODEL), wg, wu, wd, gpost.reshape(1, D_MODEL))


def _tiles(s):
    big = s >= 4096
    return dict(
        tm_in=1024 if big else 256, tn_in=512,
        tk_sel=512 if big else 256, nb_nsa=2,
        tm_pool=512 if big else 256,
        tm_merge=256,
        tm_x=512 if big else 256,
        tm_ffn=512 if big else 256, tf=512,
    )


def _pad_w_in(w_in):
    w = w_in.astype(BF16)
    n_gl = 3 * NSA_HEADS
    o_gl = NSA_WIDTH + 6 * KV_WIDTH
    o_u = o_gl + n_gl
    o_ml = o_u + POOL_WIDTH
    pad = jnp.zeros(w.shape[:2] + (GL_PAD - n_gl,), BF16)
    return jnp.concatenate(
        [w[..., o_ml:], w[..., o_u:o_ml], w[..., :o_gl], w[..., o_gl:o_u], pad], axis=-1)


def _forward(x, mem, rel_bias, ln_mix_pre, ln_mix_post, ln_x_pre, ln_x_post, ln_mem,
             ln_ffn_pre, ln_ffn_post, w_in, cmp_pe, cmp_w1, cmp_w2, w_pool, pool_scale,
             w_br_attn, w_br_pool, w_mix_out, w_xq, w_xkv, w_xo, w_gate, w_up, w_down):
    b, s, _ = x.shape
    assert b == 1 and s % 1024 == 0
    depth = w_in.shape[0]
    tl = _tiles(s)
    nc = s // CMP_STRIDE
    nq = s // Q_BLK
    half = CMP_STRIDE * HEAD_DIM
    ts, wb, bc = bias_tables(rel_bias, s=s, tk=tl["tk_sel"])
    key_aug = jnp.asarray(_key_aug_lanes(s, tl["tk_sel"]))
    v_ones = jnp.zeros((NSA_GROUPS, V_ROWS - HEAD_DIM, s), BF16).at[:, 0, :].set(1.0)
    w_in_b = _pad_w_in(w_in)
    pe_b = cmp_pe.reshape(depth, 2, 2, 1, half)
    w1_b = cmp_w1.reshape(depth, 2, 2, half, CMP_HID).astype(BF16)
    w2_b, w_pool_b = cmp_w2.astype(BF16), w_pool.astype(BF16)
    wa_b, wp_b, wo_b = w_br_attn.astype(BF16), w_br_pool.astype(BF16), w_mix_out.astype(BF16)
    wxq_b, wxkv_b, wxo_b = w_xq.astype(BF16), w_xkv.astype(BF16), w_xo.astype(BF16)
    wg_b, wu_b, wd_b = w_gate.astype(BF16), w_up.astype(BF16), w_down.astype(BF16)
    xc = x[0]
    mem2 = mem[0]
    for l in range(depth):
        z = norm_matmul(xc, ln_mix_pre[l], w_in_b, l, tm=tl["tm_in"], tn=tl["tn_in"])
        q = z[:, OFF_Q:OFF_Q + NSA_WIDTH] * ((HEAD_DIM ** -0.5) * LOG2E)
        qt = q.reshape(nq, Q_BLK, NSA_GROUPS, NSA_HPG, HEAD_DIM).transpose(2, 0, 4, 3, 1)
        qt = qt.reshape(NSA_GROUPS, nq, HEAD_DIM, NSA_HPG * Q_BLK).astype(BF16)
        kv = z[:, OFF_KV:OFF_KV + 6 * KV_WIDTH].reshape(s, 6, NSA_GROUPS, HEAD_DIM)
        c = kv[:, 0:2].reshape(nc, CMP_STRIDE, 2, NSA_GROUPS, HEAD_DIM)
        c = c.transpose(2, 3, 0, 1, 4).reshape(2, NSA_GROUPS, nc, half)
        kvc = compress(c, pe_b, w1_b, w2_b, l, s=s)
        kvc = jnp.transpose(kvc, (1, 0, 2, 3))
        kvc_pad = jnp.pad(kvc, ((0, 0), (0, 0), (nc - 8, 8), (0, LANES - HEAD_DIM)))
        kvc_pad = kvc_pad.at[:, 0, :nc - 8, HEAD_DIM].set(1.0)
        ks = jnp.concatenate([kv[:, 2], jnp.broadcast_to(key_aug[:, None, :], (s, NSA_GROUPS, HEAD_DIM))],
                             axis=-1).transpose(1, 0, 2).astype(BF16)
        kw = jnp.concatenate([kv[:, 4], kv[:, 5]], axis=-1).transpose(1, 0, 2).astype(BF16)
        vst = jnp.concatenate([kv[:, 3].transpose(1, 2, 0).astype(BF16), v_ones], axis=1)
        vwt = jnp.concatenate([kv[:, 5].transpose(1, 2, 0).astype(BF16), v_ones], axis=1)
        glt = z[:, OFF_GL:OFF_GL + 3 * NSA_HEADS].reshape(s, NSA_GROUPS, 3 * NSA_HPG).transpose(1, 2, 0)
        a = nsa_attention(qt, kvc_pad, ks, vst, kw, vwt, glt, ts, wb, bc, s=s, tk=tl["tk_sel"],
                          nb=tl["nb_nsa"])
        p = pool_mixer(z, w_pool_b, pool_scale[l], l, tm=tl["tm_pool"])
        xc = merge_mix(a, p, z, xc, wa_b, wp_b, wo_b, ln_mix_post[l], l, tm=tl["tm_merge"])
        memkv = norm_matmul(mem2, ln_mem[l], wxkv_b, l, tm=N_MEM, tn=512, out_dtype=BF16)
        xc = cross_attention(xc, ln_x_pre[l], wxq_b, memkv, wxo_b, ln_x_post[l], l, tm=tl["tm_x"])
        xc = ffn(xc, ln_ffn_pre[l], wg_b, wu_b, wd_b, ln_ffn_post[l], l, tm=tl["tm_ffn"], tf=tl["tf"])
    return xc[None]


def kernel(x, mem, rel_bias, ln_mix_pre, ln_mix_post, ln_x_pre, ln_x_post, ln_mem, ln_ffn_pre,
           ln_ffn_post, w_in, cmp_pe, cmp_w1, cmp_w2, w_pool, pool_scale, w_br_attn, w_br_pool,
           w_mix_out, w_xq, w_xkv, w_xo, w_gate, w_up, w_down):
    return _forward(x, mem, rel_bias, ln_mix_pre, ln_mix_post, ln_x_pre, ln_x_post, ln_mem,
                    ln_ffn_pre, ln_ffn_post, w_in, cmp_pe, cmp_w1, cmp_w2, w_pool, pool_scale,
                    w_br_attn, w_br_pool, w_mix_out, w_xq, w_xkv, w_xo, w_gate, w_up, w_down)
```

```python
import functools
import math

import numpy as np
import jax
import jax.numpy as jnp
from jax import lax
from jax.experimental import pallas as pl
from jax.experimental.pallas import tpu as pltpu

F32 = jnp.float32
BF16 = jnp.bfloat16

D_MODEL = 2048
DEPTH = 4
N_MEM = 256
NSA_HEADS = 16
NSA_GROUPS = 4
NSA_HPG = 4
HEAD_DIM = 64
NSA_WIDTH = 1024
KV_WIDTH = 256
CMP_LEN = 32
CMP_STRIDE = 16
CMP_HID = 128
SEL_BLK = 64
SEL_TOPN = 16
WIN = 512
Q_BLK = 128
FORCE_SCORE = 1e4
POOL_WINDOWS = (2, 4, 8, 16)
POOL_GW = 256
POOL_WIDTH = 1024
REL_BUCKETS = 32
REL_MAX_DIST = 2048
X_HEADS = 4
X_HEAD_DIM = 128
X_WIDTH = 512
D_FF = 5632
NEG = -1e30
LOG2E = 1.4426950408889634

GL_PAD = 512
OFF_ML, OFF_U, OFF_GL, OFF_CMP = 0, 4096, 5120, 5632
IN_F32 = 6144
IN_KV4 = 4 * KV_WIDTH
IN_PAD = IN_F32 + NSA_WIDTH + IN_KV4
IN_TN = 512

VMEM_LIMIT_V7X = 56 * 1024 * 1024
LANES = 128


def _cparams(sem, flags=None):
    return pltpu.CompilerParams(dimension_semantics=sem, vmem_limit_bytes=VMEM_LIMIT_V7X, flags=flags)


def _rms(x, g):
    ms = jnp.mean(x * x, axis=-1, keepdims=True)
    return x * lax.rsqrt(ms + 1e-6) * g


def _sigmoid(x):
    return 1.0 / (1.0 + jnp.exp(-x))


def _dot(a, b):
    return jnp.dot(a, b, preferred_element_type=F32)


def _dot_nt(a, b):
    return lax.dot_general(a, b, (((1,), (1,)), ((), ())), preferred_element_type=F32)


def _norm_matmul_kernel(x_ref, g_ref, w_ref, o_ref, h_ref):
    @pl.when(pl.program_id(1) == 0)
    def _():
        h_ref[...] = _rms(x_ref[...], g_ref[...]).astype(BF16)

    o_ref[...] = _dot(h_ref[...], w_ref[...]).astype(o_ref.dtype)


def norm_matmul(x, g, w, l, *, tm, tn, out_dtype=F32):
    m, k = x.shape
    n = w.shape[2]
    return pl.pallas_call(
        _norm_matmul_kernel,
        grid=(m // tm, n // tn),
        in_specs=[
            pl.BlockSpec((tm, k), lambda i, j: (i, 0)),
            pl.BlockSpec((1, k), lambda i, j: (0, 0)),
            pl.BlockSpec((None, k, tn), lambda i, j: (l, 0, j)),
        ],
        out_specs=pl.BlockSpec((tm, tn), lambda i, j: (i, j)),
        out_shape=jax.ShapeDtypeStruct((m, n), out_dtype),
        scratch_shapes=[pltpu.VMEM((tm, k), BF16)],
        compiler_params=_cparams(("parallel", "arbitrary")),
        name="norm_matmul",
    )(x, g.reshape(1, k), w)


Q_SCALE = (HEAD_DIM ** -0.5) * LOG2E


def _in_proj_kernel(x_ref, g_ref, w_ref, of_ref, oq_ref, okv_ref, h_ref, *, nf, nq):
    j = pl.program_id(1)

    @pl.when(j == 0)
    def _():
        h_ref[...] = _rms(x_ref[...], g_ref[...]).astype(BF16)

    res = _dot(h_ref[...], w_ref[...])

    @pl.when(j < nf)
    def _():
        of_ref[...] = res

    @pl.when((j >= nf) & (j < nf + nq))
    def _():
        oq_ref[...] = (res * Q_SCALE).astype(BF16)

    @pl.when(j >= nf + nq)
    def _():
        okv_ref[...] = res.astype(BF16)


def in_proj(x, g, w, l, *, tm):
    m, k = x.shape
    tn = IN_TN
    nf, nq, nkv = IN_F32 // tn, NSA_WIDTH // tn, IN_KV4 // tn
    return pl.pallas_call(
        functools.partial(_in_proj_kernel, nf=nf, nq=nq),
        grid=(m // tm, nf + nq + nkv),
        in_specs=[
            pl.BlockSpec((tm, k), lambda i, j: (i, 0)),
            pl.BlockSpec((1, k), lambda i, j: (0, 0)),
            pl.BlockSpec((None, k, tn), lambda i, j: (l, 0, j)),
        ],
        out_specs=[
            pl.BlockSpec((tm, tn), lambda i, j: (i, jnp.minimum(j, nf - 1))),
            pl.BlockSpec((tm, tn), lambda i, j: (i, jnp.clip(j - nf, 0, nq - 1))),
            pl.BlockSpec((tm, tn), lambda i, j: (i, jnp.clip(j - nf - nq, 0, nkv - 1))),
        ],
        out_shape=[
            jax.ShapeDtypeStruct((m, IN_F32), F32),
            jax.ShapeDtypeStruct((m, NSA_WIDTH), BF16),
            jax.ShapeDtypeStruct((m, IN_KV4), BF16),
        ],
        scratch_shapes=[pltpu.VMEM((tm, k), BF16)],
        compiler_params=_cparams(("parallel", "arbitrary")),
        name="in_proj",
    )(x, g.reshape(1, k), w)


def _bucket(d):
    n = jnp.maximum(d, 0)
    exact = REL_BUCKETS // 2
    nf = jnp.maximum(n, 1).astype(F32)
    large = exact + (jnp.log(nf / exact) / math.log(REL_MAX_DIST / exact)
                     * (REL_BUCKETS - exact)).astype(jnp.int32)
    return jnp.where(n < exact, n, jnp.minimum(large, REL_BUCKETS - 1))


def _lookup(tab_ref, h, bk):
    out = jnp.full(bk.shape, tab_ref[h, 0], F32)
    for b in range(1, REL_BUCKETS):
        out = jnp.where(bk == b, tab_ref[h, b], out)
    return out


def _bias_tables_kernel(tab_ref, ts_ref, wb_ref, bc_ref, *, ls, off, lw, ncw, cmp_shift):
    h = pl.program_id(0)
    r = lax.broadcasted_iota(jnp.int32, (LANES, Q_BLK), 0)
    qi = lax.broadcasted_iota(jnp.int32, (LANES, Q_BLK), 1)

    def chunked(ref, n_rows, dist, valid):
        def body(ch, carry):
            d = dist(ch * LANES + r)
            val = _lookup(tab_ref, h, _bucket(d))
            ref[0, pl.ds(pl.multiple_of(ch * LANES, LANES), LANES), :] = jnp.where(valid(d), val * LOG2E, NEG)
            return carry
        lax.fori_loop(0, n_rows // LANES, body, 0)

    chunked(ts_ref, ls, lambda c: qi - c + off, lambda d: d >= 0)
    chunked(wb_ref, lw, lambda c: qi - c + WIN, lambda d: (d >= 0) & (d < WIN))
    chunked(bc_ref, ncw, lambda m: qi - CMP_STRIDE * m + cmp_shift, lambda d: d >= 0)


def _table_sizes(s, tk):
    off = REL_MAX_DIST + tk
    ls = off + 3 * tk
    lw = 2 * WIN + Q_BLK
    return off, ls, lw


def bias_tables(rel_bias, *, s, tk):
    off, ls, lw = _table_sizes(s, tk)
    ncw = s // CMP_STRIDE
    cmp_shift = CMP_STRIDE * (ncw - 8) - (CMP_LEN - 1)
    kern = functools.partial(_bias_tables_kernel, ls=ls, off=off, lw=lw, ncw=ncw, cmp_shift=cmp_shift)
    return pl.pallas_call(
        kern,
        grid=(NSA_HEADS,),
        in_specs=[pl.BlockSpec(memory_space=pltpu.SMEM)],
        out_specs=[
            pl.BlockSpec((1, ls, Q_BLK), lambda h: (h, 0, 0)),
            pl.BlockSpec((1, lw, Q_BLK), lambda h: (h, 0, 0)),
            pl.BlockSpec((1, ncw, Q_BLK), lambda h: (h, 0, 0)),
        ],
        out_shape=[
            jax.ShapeDtypeStruct((NSA_HEADS, ls, Q_BLK), F32),
            jax.ShapeDtypeStruct((NSA_HEADS, lw, Q_BLK), F32),
            jax.ShapeDtypeStruct((NSA_HEADS, ncw, Q_BLK), F32),
        ],
        compiler_params=_cparams(("arbitrary",)),
        name="bias_tables",
    )(rel_bias)


def _gelu_tanh(x):
    return 0.5 * x * (1.0 + jnp.tanh(math.sqrt(2.0 / math.pi) * (x + 0.044715 * x * x * x)))


def _compress_kernel(c_ref, pe_ref, w1_ref, w2_ref, o_ref, scr_ref, *, nc):
    c = c_ref[0, 0]
    ca = (c + pe_ref[0, 0]).astype(BF16)
    cb = (c + pe_ref[0, 1]).astype(BF16)
    h1 = _dot(ca, w1_ref[0, 0])
    h2 = _dot(cb, w1_ref[0, 1])
    scr_ref[pl.ds(0, nc), :] = h2
    scr_ref[pl.ds(nc, 8), :] = jnp.zeros((8, CMP_HID), F32)
    hid = h1 + scr_ref[pl.ds(1, nc), :]
    out = _dot(_gelu_tanh(hid).astype(BF16), w2_ref[0])
    row = lax.broadcasted_iota(jnp.int32, out.shape, 0)
    o_ref[0, 0] = jnp.where(row < nc - 1, out, 0.0)


def compress(c, pe, w1, w2, l, *, s):
    nc = s // CMP_STRIDE
    half = CMP_STRIDE * HEAD_DIM
    return pl.pallas_call(
        functools.partial(_compress_kernel, nc=nc),
        grid=(2, NSA_GROUPS),
        in_specs=[
            pl.BlockSpec((1, 1, nc, half), lambda a, g: (a, g, 0, 0)),
            pl.BlockSpec((None, 1, 2, 1, half), lambda a, g: (l, a, 0, 0, 0)),
            pl.BlockSpec((None, 1, 2, half, CMP_HID), lambda a, g: (l, a, 0, 0, 0)),
            pl.BlockSpec((None, 1, CMP_HID, HEAD_DIM), lambda a, g: (l, a, 0, 0)),
        ],
        out_specs=pl.BlockSpec((1, 1, nc, HEAD_DIM), lambda a, g: (a, g, 0, 0)),
        out_shape=jax.ShapeDtypeStruct((2, NSA_GROUPS, nc, HEAD_DIM), F32),
        scratch_shapes=[pltpu.VMEM((nc + 8, CMP_HID), F32)],
        compiler_params=_cparams(("arbitrary", "arbitrary")),
        name="compress_kv",
    )(c, pe, w1, w2)


V_ROWS = HEAD_DIM + 16


def _lanes4(fn):
    return jnp.concatenate([fn(hh) for hh in range(NSA_HPG)], axis=-1)


def _nsa_kernel(q_ref, kvc_ref, ks_ref, vs_ref, kw_ref, vw_ref, gl_ref, ts_ref, wb_ref, bc_ref,
                ov_ref, o_ref, mask_ref, sa_ref, sb_ref, sw_ref, *, tk, ncw, nsel, off, nb):
    rows1 = NSA_HPG * Q_BLK
    rows = nb * rows1
    ib = [nb * pl.program_id(1) + b for b in range(nb)]
    q0b = [i * Q_BLK for i in ib]

    def blk(b):
        return slice(b * rows1, (b + 1) * rows1)

    def per_block(fn):
        return jnp.concatenate([fn(b) for b in range(nb)], axis=-1)

    def table_rows(ref, row0, n):
        return per_block(lambda b: _lanes4(lambda hh: ref[hh, pl.ds(row0(b), n), :]))

    qt = per_block(lambda b: q_ref[0, b])
    qa = jnp.concatenate([qt, jnp.zeros_like(qt)], axis=0)

    wk = WIN + Q_BLK
    startb = [pl.multiple_of(jnp.maximum(q0 - WIN, 0), Q_BLK) for q0 in q0b]
    for b in range(nb):
        sw_ref[:, blk(b)] = _dot(kw_ref[0, pl.ds(startb[b], wk), :], qa[:, blk(b)])

    tc = min(tk, ncw)
    neg_row = jnp.where(lax.broadcasted_iota(jnp.int32, (AUG_ROWS, rows), 0) == 0, NEG, 0.0).astype(BF16)
    qa_c = jnp.concatenate([qt, neg_row, jnp.zeros((LANES - HEAD_DIM - AUG_ROWS, rows), BF16)], axis=0)
    ones_rows = (lax.broadcasted_iota(jnp.int32, (V_ROWS - HEAD_DIM, tc), 0) == 0).astype(BF16)

    def cmp_rows(b, t):
        return pl.ds(pl.multiple_of(8 * ib[b] + t * tc, 8), tc)

    def cmp_scores(buf, t):
        for b in range(nb):
            buf[pl.ds(0, tc), blk(b)] = _dot(kvc_ref[0, 0, cmp_rows(b, t), :].astype(BF16), qa_c[:, blk(b)])

    def cmp_tile(buf, t, carry):
        m, acc, impa = carry
        s = buf[pl.ds(0, tc), :] + table_rows(bc_ref, lambda b: t * tc, tc)
        m_new = jnp.maximum(m, jnp.max(s, axis=0, keepdims=True))
        alpha = jnp.exp2(m - m_new)
        p = jnp.exp2(s - m_new).astype(BF16)

        def pv(b):
            vc = kvc_ref[0, 1, cmp_rows(b, t), :]
            vct = jnp.concatenate([vc.T[:HEAD_DIM, :].astype(BF16), ones_rows], axis=0)
            return _dot(vct, p[:, blk(b)])

        return (m_new, alpha * acc + per_block(pv),
                alpha * impa + _dot(ov_ref[:, t * tc:(t + 1) * tc], p))

    cmp_bufs = (sa_ref, sb_ref)
    n_ct = ncw // tc
    cmp_scores(cmp_bufs[0], 0)

    carry_c = (jnp.full((1, rows), NEG, F32), jnp.zeros((V_ROWS, rows), F32), jnp.zeros((nsel, rows), F32))
    for t in range(n_ct):
        if t + 1 < n_ct:
            cmp_scores(cmp_bufs[(t + 1) % 2], t + 1)
        carry_c = cmp_tile(cmp_bufs[t % 2], t, carry_c)
    m_c, acc_c, imp_c = carry_c
    inv = jnp.where(m_c > 0.5 * NEG, 1.0 / acc_c[HEAD_DIM:HEAD_DIM + 1, :], 0.0)
    o_c = acc_c[:HEAD_DIM, :] * inv
    imp4 = imp_c * inv

    def head_sum(b):
        c = [imp4[:, b * rows1 + hh * Q_BLK:b * rows1 + (hh + 1) * Q_BLK] for hh in range(NSA_HPG)]
        return (c[0] + c[1]) + (c[2] + c[3])
    imp = per_block(head_sum)

    s_w = sw_ref[...] + table_rows(wb_ref, lambda b: pl.multiple_of(WIN - (q0b[b] - startb[b]), Q_BLK), wk)
    m_w = jnp.max(s_w, axis=0, keepdims=True)
    e_w = jnp.exp2(s_w - m_w).astype(BF16)
    oa_w = per_block(lambda b: _dot(vw_ref[0, :, pl.ds(startb[b], wk)], e_w[:, blk(b)]))
    o_w = oa_w[:HEAD_DIM, :] * (1.0 / oa_w[HEAD_DIM:HEAD_DIM + 1, :])

    nq = nb * Q_BLK
    jp = lax.broadcasted_iota(jnp.int32, (nsel, nq), 0)
    lane = lax.broadcasted_iota(jnp.int32, (nsel, nq), 1)
    i_lane = nb * pl.program_id(1) + (lane >> 7)
    jabs = jp + (2 * i_lane - (nsel - 2))
    cur = 2 * i_lane + ((lane & (Q_BLK - 1)) >= SEL_BLK).astype(jnp.int32)
    val = jnp.where(jabs == cur - 1, FORCE_SCORE, imp)
    val = jnp.where(jabs == cur, 2.0 * FORCE_SCORE, val)
    val = jnp.where(jabs == 0, 3.0 * FORCE_SCORE, val)
    val = jnp.where(jabs > cur, -1.0, val)
    val = jnp.where(jabs < 0, -3.0, val)

    work = val
    for _ in range(SEL_TOPN):
        work = jnp.where(work == jnp.max(work, axis=0, keepdims=True), -1e38, work)
    fast = jnp.where((work < -1e37) & (val >= 0.0), 1.0, 0.0)
    tie_seen = jnp.max(jnp.sum(fast, axis=0, keepdims=True)) > SEL_TOPN + 0.5

    def exact_topn():
        wk_, chosen = val, jnp.zeros((nsel, nq), F32)
        jpf = jp.astype(F32)
        for _ in range(SEL_TOPN):
            mx = jnp.max(wk_, axis=0, keepdims=True)
            jmin = jnp.min(jnp.where(wk_ == mx, jpf, 2.0 * nsel), axis=0, keepdims=True)
            hit = jpf == jmin
            chosen = jnp.where(hit, 1.0, chosen)
            wk_ = jnp.where(hit, -1e38, wk_)
        return jnp.where(val >= 0.0, chosen, 0.0)

    sel = lax.cond(tie_seen, exact_topn, lambda: fast).astype(BF16)
    a_row = lax.broadcasted_iota(jnp.int32, (nsel, nsel), 0)
    s_col = lax.broadcasted_iota(jnp.int32, (nsel, nsel), 1)
    for b in range(nb):
        shift = jnp.where(s_col == a_row - (2 * ib[b] - (nsel - 2)), 1.0, 0.0).astype(BF16)
        mask_abs = (_dot(shift, sel[:, b * Q_BLK:(b + 1) * Q_BLK]) - 1.0) * (-NEG)
        mask_ref[:, blk(b)] = jnp.concatenate([mask_abs] * NSA_HPG, axis=-1)

    nblk = tk // SEL_BLK
    far_bias = table_rows(ts_ref, lambda b: 0, 1)
    far_hi = far_bias.astype(BF16).astype(F32)
    far_lo = (far_bias - far_hi).astype(BF16).astype(F32)
    far_lo2 = (far_bias - far_hi - far_lo).astype(BF16).astype(F32)
    far_rows = jnp.concatenate([far_hi, far_lo, far_lo2, jnp.zeros((8 - FAR_TERMS, rows), F32)], axis=0)
    n_pad = AUG_ROWS - nblk - 8
    aug_pad = [jnp.zeros((n_pad, rows), F32)] if n_pad else []
    rhs_tail = jnp.zeros((LANES - HEAD_DIM - AUG_ROWS, rows), BF16)
    n_tiles = (q0b[-1] + Q_BLK + tk - 1) // tk
    n_pairs = (n_tiles + 1) // 2
    n_far_pairs = (jnp.maximum(q0b[0] - off + tk, 0) // tk) // (2 * PAIRS_PER_ITER) * PAIRS_PER_ITER
    last_tile = ks_ref.shape[1] // tk - 1

    def scores_into(buf, kt):
        kt = jnp.minimum(kt, last_tile)
        k0 = pl.multiple_of(kt * tk, tk)
        slab = mask_ref[pl.ds(pl.multiple_of(kt * nblk, nblk), nblk), :]
        far = jnp.where(kt < 2 * n_far_pairs, far_rows, 0.0)
        aug = jnp.concatenate([slab, far] + aug_pad, axis=0).astype(BF16)
        rhs = jnp.concatenate([qt, aug, rhs_tail], axis=0)
        buf[...] = _dot(ks_ref[0, pl.ds(k0, tk), :], rhs)

    def sel_tile(buf, kt, carry, near):
        m, acc = carry
        k0 = pl.multiple_of(kt * tk, tk)
        s = buf[...]
        vt = vs_ref[0, :, pl.ds(k0, tk)]
        if near:
            s = s + table_rows(
                ts_ref, lambda b: pl.multiple_of(jnp.maximum(off - (q0b[b] - k0), 0), Q_BLK), tk)
        m_new = jnp.maximum(m, jnp.max(s, axis=0, keepdims=True))
        p = jnp.exp2(s - m_new)
        acc = jnp.exp2(m - m_new) * acc + _dot(vt, p.astype(BF16))
        return m_new, acc

    def sel_pairs(it, carry, near, pairs, base):
        for j in range(pairs):
            kt = 2 * (base + pairs * it + j)
            scores_into(sb_ref, kt + 1)
            carry = sel_tile(sa_ref, kt, carry, near)
            scores_into(sa_ref, kt + 2)
            carry = sel_tile(sb_ref, kt + 1, carry, near)
        return carry

    scores_into(sa_ref, 0)
    carry = (jnp.full((1, rows), NEG, F32), jnp.zeros((V_ROWS, rows), F32))
    n_near_long = (n_pairs - n_far_pairs) // PAIRS_PER_ITER
    carry = lax.fori_loop(0, n_far_pairs // PAIRS_PER_ITER,
                          functools.partial(sel_pairs, near=False, pairs=PAIRS_PER_ITER, base=0), carry)
    carry = lax.fori_loop(0, n_near_long,
                          functools.partial(sel_pairs, near=True, pairs=PAIRS_PER_ITER, base=n_far_pairs), carry)
    _, acc_s = lax.fori_loop(n_far_pairs + PAIRS_PER_ITER * n_near_long, n_pairs,
                             functools.partial(sel_pairs, near=True, pairs=1, base=0), carry)
    o_s = acc_s[:HEAD_DIM, :] * (1.0 / acc_s[HEAD_DIM:HEAD_DIM + 1, :])

    gates = _sigmoid(gl_ref[0])
    def gate_row(c):
        return per_block(lambda b: _lanes4(
            lambda hh: gates[3 * hh + c:3 * hh + c + 1, b * Q_BLK:(b + 1) * Q_BLK]))
    o = gate_row(0) * o_c + gate_row(1) * o_s + gate_row(2) * o_w
    for b in range(nb):
        ob = jnp.concatenate([o[:, b * rows1 + hh * Q_BLK:b * rows1 + (hh + 1) * Q_BLK]
                              for hh in range(NSA_HPG)], axis=0)
        o_ref[b * Q_BLK:(b + 1) * Q_BLK, :] = ob.T.astype(o_ref.dtype)


PAIRS_PER_ITER = 2
FAR_TERMS = 3
AUG_ROWS = 16


def _key_aug_lanes(s, tk):
    nblk = tk // SEL_BLK
    aug = np.zeros((s, HEAD_DIM), np.float32)
    aug[np.arange(s), (np.arange(s) % tk) // SEL_BLK] = 1.0
    aug[:, nblk:nblk + FAR_TERMS] = 1.0
    return aug


def _overlap_matrix_t(ncw, nsel):
    m = np.arange(ncw)[None, :]
    j = np.arange(nsel)[:, None]
    lo = np.maximum(m * CMP_STRIDE, j * SEL_BLK)
    hi = np.minimum(m * CMP_STRIDE + CMP_LEN, (j + 1) * SEL_BLK)
    return (np.maximum(hi - lo, 0).astype(np.float32) / CMP_LEN)


def nsa_attention(qt, kvc_pad, ks, vst, kw, vwt, glt, ts, wb, bc, *, s, tk, nb):
    ncw = s // CMP_STRIDE
    nsel = s // SEL_BLK
    off, ls, lw = _table_sizes(s, tk)
    rows1 = NSA_HPG * Q_BLK
    rows = nb * rows1
    ov = jnp.asarray(_overlap_matrix_t(ncw, nsel), BF16)
    kern = functools.partial(_nsa_kernel, tk=tk, ncw=ncw, nsel=nsel, off=off, nb=nb)
    per_group = dict(pipeline_mode=pl.Buffered(1))
    return pl.pallas_call(
        kern,
        grid=(NSA_GROUPS, s // (nb * Q_BLK)),
        in_specs=[
            pl.BlockSpec((1, nb, HEAD_DIM, rows1), lambda g, i: (g, i, 0, 0)),
            pl.BlockSpec((1, 2, 2 * ncw, LANES), lambda g, i: (g, 0, 0, 0), **per_group),
            pl.BlockSpec((1, s, LANES), lambda g, i: (g, 0, 0), **per_group),
            pl.BlockSpec((1, V_ROWS, s), lambda g, i: (g, 0, 0), **per_group),
            pl.BlockSpec((1, s, LANES), lambda g, i: (g, 0, 0), **per_group),
            pl.BlockSpec((1, V_ROWS, s), lambda g, i: (g, 0, 0), **per_group),
            pl.BlockSpec((1, 3 * NSA_HPG, nb * Q_BLK), lambda g, i: (g, 0, i)),
            pl.BlockSpec((NSA_HPG, ls, Q_BLK), lambda g, i: (g, 0, 0), **per_group),
            pl.BlockSpec((NSA_HPG, lw, Q_BLK), lambda g, i: (g, 0, 0), **per_group),
            pl.BlockSpec((NSA_HPG, ncw, Q_BLK), lambda g, i: (g, 0, 0), **per_group),
            pl.BlockSpec((nsel, ncw), lambda g, i: (0, 0), **per_group),
        ],
        out_specs=pl.BlockSpec((nb * Q_BLK, NSA_HPG * HEAD_DIM), lambda g, i: (i, g)),
        out_shape=jax.ShapeDtypeStruct((s, NSA_WIDTH), BF16),
        scratch_shapes=[pltpu.VMEM((nsel, rows), F32), pltpu.VMEM((tk, rows), F32),
                        pltpu.VMEM((tk, rows), F32), pltpu.VMEM((WIN + Q_BLK, rows), F32)],
        compiler_params=_cparams(("arbitrary", "arbitrary")),
        name="nsa_attention",
    )(qt, kvc_pad, ks, vst, kw, vwt, glt, ts, wb, bc, ov)


HALO = 16


def _pool_kernel(u_ref, halo_ref, w_ref, sc_ref, o_ref, ext_ref, *, tm):
    i = pl.program_id(0)
    halo = jnp.where(i > 0, halo_ref[...], 0.0)
    ext_ref[pl.ds(0, HALO), :] = halo
    ext_ref[pl.ds(HALO, tm), :] = u_ref[...]
    t = i * tm + lax.broadcasted_iota(jnp.int32, (tm, 1), 0)
    outs = []
    for gi, w in enumerate(POOL_WINDOWS):
        cols = pl.ds(gi * POOL_GW, POOL_GW)
        x = ext_ref[pl.ds(HALO, tm), cols]
        acc = x
        for k in range(1, w):
            acc = acc + ext_ref[pl.ds(HALO - k, tm), cols]
        cnt = jnp.minimum(t + 1, w).astype(F32)
        y = acc / cnt - x
        outs.append(_dot(y.astype(BF16), w_ref[gi]))
    o_ref[...] = (jnp.concatenate(outs, axis=-1) * sc_ref[...]).astype(o_ref.dtype)


def pool_mixer(z, w_pool, pool_scale, l, *, tm):
    s = z.shape[0]
    ucol = OFF_U // POOL_WIDTH
    return pl.pallas_call(
        functools.partial(_pool_kernel, tm=tm),
        grid=(s // tm,),
        in_specs=[
            pl.BlockSpec((tm, POOL_WIDTH), lambda i: (i, ucol)),
            pl.BlockSpec((HALO, POOL_WIDTH), lambda i: (jnp.maximum(i * (tm // HALO) - 1, 0), ucol)),
            pl.BlockSpec((None, len(POOL_WINDOWS), POOL_GW, POOL_GW), lambda i: (l, 0, 0, 0)),
            pl.BlockSpec((1, POOL_WIDTH), lambda i: (0, 0)),
        ],
        out_specs=pl.BlockSpec((tm, POOL_WIDTH), lambda i: (i, 0)),
        out_shape=jax.ShapeDtypeStruct((s, POOL_WIDTH), BF16),
        scratch_shapes=[pltpu.VMEM((HALO + tm, POOL_WIDTH), F32)],
        compiler_params=_cparams(("parallel",)),
        name="pool_mixer",
    )(z, z, w_pool, pool_scale.reshape(1, POOL_WIDTH))


def _merge_kernel(a_ref, p_ref, ml_ref, x_ref, wa_ref, wp_ref, wo_ref, g_ref, o_ref):
    pa = _dot(a_ref[...], wa_ref[...])
    pp = _dot(p_ref[...], wp_ref[...])
    ml = ml_ref[...]
    y = _sigmoid(ml[:, :D_MODEL]) * pa + _sigmoid(ml[:, D_MODEL:]) * pp
    y = _dot(y.astype(BF16), wo_ref[...])
    o_ref[...] = x_ref[...] + _rms(y, g_ref[...])


def merge_mix(a, p, z, x, wa, wp, wo, g, l, *, tm):
    s = x.shape[0]
    const = lambda i: (0, 0)
    layer = lambda i: (l, 0, 0)
    return pl.pallas_call(
        _merge_kernel,
        grid=(s // tm,),
        in_specs=[
            pl.BlockSpec((tm, NSA_WIDTH), lambda i: (i, 0)),
            pl.BlockSpec((tm, POOL_WIDTH), lambda i: (i, 0)),
            pl.BlockSpec((tm, 2 * D_MODEL), lambda i: (i, OFF_ML // (2 * D_MODEL))),
            pl.BlockSpec((tm, D_MODEL), lambda i: (i, 0)),
            pl.BlockSpec((None, NSA_WIDTH, D_MODEL), layer, pipeline_mode=pl.Buffered(1)),
            pl.BlockSpec((None, POOL_WIDTH, D_MODEL), layer, pipeline_mode=pl.Buffered(1)),
            pl.BlockSpec((None, D_MODEL, D_MODEL), layer, pipeline_mode=pl.Buffered(1)),
            pl.BlockSpec((1, D_MODEL), const),
        ],
        out_specs=pl.BlockSpec((tm, D_MODEL), lambda i: (i, 0)),
        out_shape=jax.ShapeDtypeStruct((s, D_MODEL), F32),
        compiler_params=_cparams(("parallel",)),
        name="merge_mix",
    )(a, p, z, x, wa, wp, wo, g.reshape(1, D_MODEL))


def _xattn_kernel(x_ref, gpre_ref, wq_ref, kv_ref, wo_ref, gpost_ref, o_ref):
    x = x_ref[...]
    h = _rms(x, gpre_ref[...]).astype(BF16)
    q = _dot(h, wq_ref[...]).astype(BF16)
    kv = kv_ref[...]
    outs = []
    for hh in range(X_HEADS):
        lo = hh * X_HEAD_DIM
        sc = _dot_nt(q[:, lo:lo + X_HEAD_DIM], kv[:, lo:lo + X_HEAD_DIM]) * (X_HEAD_DIM ** -0.5)
        m = jnp.max(sc, axis=-1, keepdims=True)
        e = jnp.exp(sc - m)
        p = (e / jnp.sum(e, axis=-1, keepdims=True)).astype(BF16)
        outs.append(_dot(p, kv[:, X_WIDTH + lo:X_WIDTH + lo + X_HEAD_DIM]))
    o = jnp.concatenate(outs, axis=-1).astype(BF16)
    y = _dot(o, wo_ref[...])
    o_ref[...] = x + _rms(y, gpost_ref[...])


def cross_attention(x, gpre, wq, memkv, wo, gpost, l, *, tm):
    s = x.shape[0]
    const = lambda i: (0, 0)
    layer = lambda i: (l, 0, 0)
    return pl.pallas_call(
        _xattn_kernel,
        grid=(s // tm,),
        in_specs=[
            pl.BlockSpec((tm, D_MODEL), lambda i: (i, 0)),
            pl.BlockSpec((1, D_MODEL), const),
            pl.BlockSpec((None, D_MODEL, X_WIDTH), layer),
            pl.BlockSpec((N_MEM, 2 * X_WIDTH), const),
            pl.BlockSpec((None, X_WIDTH, D_MODEL), layer),
            pl.BlockSpec((1, D_MODEL), const),
        ],
        out_specs=pl.BlockSpec((tm, D_MODEL), lambda i: (i, 0)),
        out_shape=jax.ShapeDtypeStruct((s, D_MODEL), F32),
        compiler_params=_cparams(("parallel",)),
        name="cross_attention",
    )(x, gpre.reshape(1, D_MODEL), wq, memkv, wo, gpost.reshape(1, D_MODEL))


def _ffn_kernel(x_ref, gpre_ref, wg_ref, wu_ref, wd_ref, gpost_ref, o_ref, h_ref, acc_ref):
    f = pl.program_id(1)

    @pl.when(f == 0)
    def _():
        h_ref[...] = _rms(x_ref[...], gpre_ref[...]).astype(BF16)
        acc_ref[...] = jnp.zeros_like(acc_ref)

    h = h_ref[...]
    a = _dot(h, wg_ref[...])
    b = _dot(h, wu_ref[...])
    t = (a * _sigmoid(a) * b).astype(BF16)
    acc_ref[...] += _dot(t, wd_ref[...])

    @pl.when(f == pl.num_programs(1) - 1)
    def _():
        o_ref[...] = x_ref[...] + _rms(acc_ref[...], gpost_ref[...])


def ffn(x, gpre, wg, wu, wd, gpost, l, *, tm, tf):
    s = x.shape[0]
    return pl.pallas_call(
        _ffn_kernel,
        grid=(s // tm, D_FF // tf),
        in_specs=[
            pl.BlockSpec((tm, D_MODEL), lambda i, f: (i, 0)),
            pl.BlockSpec((1, D_MODEL), lambda i, f: (0, 0)),
            pl.BlockSpec((None, D_MODEL, tf), lambda i, f: (l, 0, f)),
            pl.BlockSpec((None, D_MODEL, tf), lambda i, f: (l, 0, f)),
            pl.BlockSpec((None, tf, D_MODEL), lambda i, f: (l, f, 0)),
            pl.BlockSpec((1, D_MODEL), lambda i, f: (0, 0)),
        ],
        out_specs=pl.BlockSpec((tm, D_MODEL), lambda i, f: (i, 0)),
        out_shape=jax.ShapeDtypeStruct((s, D_MODEL), F32),
        scratch_shapes=[pltpu.VMEM((tm, D_MODEL), BF16), pltpu.VMEM((tm, D_MODEL), F32)],
        compiler_params=_cparams(("parallel", "arbitrary")),
        name="ffn_swiglu",
    )(x, gpre.reshape(1, D_MODEL), wg, wu, wd, gpost.reshape(1, D_MODEL))


def _tiles(s):
    big = s >= 4096
    return dict(
        tm_in=1024 if big else 256,
        tk_sel=512 if big else 256, nb_nsa=2,
        tm_pool=512 if big else 256,
        tm_merge=256,
        tm_x=512 if big else 256,
        tm_ffn=512 if big else 256, tf=512,
    )


def _pad_w_in(w_in):
    w = w_in.astype(BF16)
    n_gl = 3 * NSA_HEADS
    o_kv = NSA_WIDTH
    o_kv4 = o_kv + 2 * KV_WIDTH
    o_gl = NSA_WIDTH + 6 * KV_WIDTH
    o_u = o_gl + n_gl
    o_ml = o_u + POOL_WIDTH
    pad = jnp.zeros(w.shape[:2] + (GL_PAD - n_gl,), BF16)
    return jnp.concatenate(
        [w[..., o_ml:], w[..., o_u:o_ml], w[..., o_gl:o_u], pad, w[..., o_kv:o_kv4], w[..., :o_kv],
         w[..., o_kv4:o_gl]], axis=-1)


def _forward(x, mem, rel_bias, ln_mix_pre, ln_mix_post, ln_x_pre, ln_x_post, ln_mem,
             ln_ffn_pre, ln_ffn_post, w_in, cmp_pe, cmp_w1, cmp_w2, w_pool, pool_scale,
             w_br_attn, w_br_pool, w_mix_out, w_xq, w_xkv, w_xo, w_gate, w_up, w_down):
    b, s, _ = x.shape
    assert b == 1 and s % 1024 == 0
    depth = w_in.shape[0]
    tl = _tiles(s)
    nc = s // CMP_STRIDE
    nq = s // Q_BLK
    half = CMP_STRIDE * HEAD_DIM
    ts, wb, bc = bias_tables(rel_bias, s=s, tk=tl["tk_sel"])
    key_aug = jnp.asarray(_key_aug_lanes(s, tl["tk_sel"]), BF16)
    v_ones = jnp.zeros((NSA_GROUPS, V_ROWS - HEAD_DIM, s), BF16).at[:, 0, :].set(1.0)
    w_in_b = _pad_w_in(w_in)
    pe_b = cmp_pe.reshape(depth, 2, 2, 1, half)
    w1_b = cmp_w1.reshape(depth, 2, 2, half, CMP_HID).astype(BF16)
    w2_b, w_pool_b = cmp_w2.astype(BF16), w_pool.astype(BF16)
    wa_b, wp_b, wo_b = w_br_attn.astype(BF16), w_br_pool.astype(BF16), w_mix_out.astype(BF16)
    wxq_b, wxkv_b, wxo_b = w_xq.astype(BF16), w_xkv.astype(BF16), w_xo.astype(BF16)
    wg_b, wu_b, wd_b = w_gate.astype(BF16), w_up.astype(BF16), w_down.astype(BF16)
    xc = x[0]
    mem2 = mem[0]
    for l in range(depth):
        z, zq, zkv = in_proj(xc, ln_mix_pre[l], w_in_b, l, tm=tl["tm_in"])
        qt = zq.reshape(nq, Q_BLK, NSA_GROUPS, NSA_HPG, HEAD_DIM).transpose(2, 0, 4, 3, 1)
        qt = qt.reshape(NSA_GROUPS, nq, HEAD_DIM, NSA_HPG * Q_BLK)
        kv = zkv.reshape(s, 4, NSA_GROUPS, HEAD_DIM)
        c = z[:, OFF_CMP:OFF_CMP + 2 * KV_WIDTH].reshape(nc, CMP_STRIDE, 2, NSA_GROUPS, HEAD_DIM)
        c = c.transpose(2, 3, 0, 1, 4).reshape(2, NSA_GROUPS, nc, half)
        kvc = compress(c, pe_b, w1_b, w2_b, l, s=s)
        kvc = jnp.transpose(kvc, (1, 0, 2, 3))
        kvc_pad = jnp.pad(kvc, ((0, 0), (0, 0), (nc - 8, 8), (0, LANES - HEAD_DIM)))
        kvc_pad = kvc_pad.at[:, 0, :nc - 8, HEAD_DIM].set(1.0)
        ks = jnp.concatenate([kv[:, 0], jnp.broadcast_to(key_aug[:, None, :], (s, NSA_GROUPS, HEAD_DIM))],
                             axis=-1).transpose(1, 0, 2)
        kw = jnp.concatenate([kv[:, 2], kv[:, 3]], axis=-1).transpose(1, 0, 2)
        vst = jnp.concatenate([kv[:, 1].transpose(1, 2, 0), v_ones], axis=1)
        vwt = jnp.concatenate([kv[:, 3].transpose(1, 2, 0), v_ones], axis=1)
        glt = z[:, OFF_GL:OFF_GL + 3 * NSA_HEADS].reshape(s, NSA_GROUPS, 3 * NSA_HPG).transpose(1, 2, 0)
        a = nsa_attention(qt, kvc_pad, ks, vst, kw, vwt, glt, ts, wb, bc, s=s, tk=tl["tk_sel"],
                          nb=tl["nb_nsa"])
        p = pool_mixer(z, w_pool_b, pool_scale[l], l, tm=tl["tm_pool"])
        xc = merge_mix(a, p, z, xc, wa_b, wp_b, wo_b, ln_mix_post[l], l, tm=tl["tm_merge"])
        memkv = norm_matmul(mem2, ln_mem[l], wxkv_b, l, tm=N_MEM, tn=512, out_dtype=BF16)
        xc = cross_attention(xc, ln_x_pre[l], wxq_b, memkv, wxo_b, ln_x_post[l], l, tm=tl["tm_x"])
        xc = ffn(xc, ln_ffn_pre[l], wg_b, wu_b, wd_b, ln_ffn_post[l], l, tm=tl["tm_ffn"], tf=tl["tf"])
    return xc[None]


def kernel(x, mem, rel_bias, ln_mix_pre, ln_mix_post, ln_x_pre, ln_x_post, ln_mem, ln_ffn_pre,
           ln_ffn_post, w_in, cmp_pe, cmp_w1, cmp_w2, w_pool, pool_scale, w_br_attn, w_br_pool,
           w_mix_out, w_xq, w_xkv, w_xo, w_gate, w_up, w_down):
    return _forward(x, mem, rel_bias, ln_mix_pre, ln_mix_post, ln_x_pre, ln_x_post, ln_mem,
                    ln_ffn_pre, ln_ffn_post, w_in, cmp_pe, cmp_w1, cmp_w2, w_pool, pool_scale,
                    w_br_attn, w_br_pool, w_mix_out, w_xq, w_xkv, w_xo, w_gate, w_up, w_down)
```

```python
import functools
import math

import numpy as np
import jax
import jax.numpy as jnp
from jax import lax
from jax.experimental import pallas as pl
from jax.experimental.pallas import tpu as pltpu

F32 = jnp.float32
BF16 = jnp.bfloat16

D_MODEL = 2048
DEPTH = 4
N_MEM = 256
NSA_HEADS = 16
NSA_GROUPS = 4
NSA_HPG = 4
HEAD_DIM = 64
NSA_WIDTH = 1024
KV_WIDTH = 256
CMP_LEN = 32
CMP_STRIDE = 16
CMP_HID = 128
SEL_BLK = 64
SEL_TOPN = 16
WIN = 512
Q_BLK = 128
FORCE_SCORE = 1e4
POOL_WINDOWS = (2, 4, 8, 16)
POOL_GW = 256
POOL_WIDTH = 1024
REL_BUCKETS = 32
REL_MAX_DIST = 2048
X_HEADS = 4
X_HEAD_DIM = 128
X_WIDTH = 512
D_FF = 5632
NEG = -1e30
LOG2E = 1.4426950408889634

GL_PAD = 512
OFF_ML, OFF_U, OFF_GL, OFF_CMP = 0, 4096, 5120, 5632
IN_F32 = 6144
IN_KV4 = 4 * KV_WIDTH
IN_PAD = IN_F32 + NSA_WIDTH + IN_KV4
IN_TN = 512

VMEM_LIMIT_V7X = 56 * 1024 * 1024
LANES = 128


def _cparams(sem, flags=None):
    return pltpu.CompilerParams(dimension_semantics=sem, vmem_limit_bytes=VMEM_LIMIT_V7X, flags=flags)


def _rms(x, g):
    ms = jnp.mean(x * x, axis=-1, keepdims=True)
    return x * lax.rsqrt(ms + 1e-6) * g


def _sigmoid(x):
    return 1.0 / (1.0 + jnp.exp(-x))


def _dot(a, b):
    return jnp.dot(a, b, preferred_element_type=F32)


def _dot_nt(a, b):
    return lax.dot_general(a, b, (((1,), (1,)), ((), ())), preferred_element_type=F32)


def _norm_matmul_kernel(x_ref, g_ref, w_ref, o_ref, h_ref):
    @pl.when(pl.program_id(1) == 0)
    def _():
        h_ref[...] = _rms(x_ref[...], g_ref[...]).astype(BF16)

    o_ref[...] = _dot(h_ref[...], w_ref[...]).astype(o_ref.dtype)


def norm_matmul(x, g, w, l, *, tm, tn, out_dtype=F32):
    m, k = x.shape
    n = w.shape[2]
    return pl.pallas_call(
        _norm_matmul_kernel,
        grid=(m // tm, n // tn),
        in_specs=[
            pl.BlockSpec((tm, k), lambda i, j: (i, 0)),
            pl.BlockSpec((1, k), lambda i, j: (0, 0)),
            pl.BlockSpec((None, k, tn), lambda i, j: (l, 0, j)),
        ],
        out_specs=pl.BlockSpec((tm, tn), lambda i, j: (i, j)),
        out_shape=jax.ShapeDtypeStruct((m, n), out_dtype),
        scratch_shapes=[pltpu.VMEM((tm, k), BF16)],
        compiler_params=_cparams(("parallel", "arbitrary")),
        name="norm_matmul",
    )(x, g.reshape(1, k), w)


Q_SCALE = (HEAD_DIM ** -0.5) * LOG2E


def _in_proj_kernel(x_ref, g_ref, w_ref, of_ref, oq_ref, okv_ref, h_ref, *, nf, nq):
    j = pl.program_id(1)

    @pl.when(j == 0)
    def _():
        h_ref[...] = _rms(x_ref[...], g_ref[...]).astype(BF16)

    res = _dot(h_ref[...], w_ref[...])

    @pl.when(j < nf)
    def _():
        of_ref[...] = res

    @pl.when((j >= nf) & (j < nf + nq))
    def _():
        oq_ref[...] = (res * Q_SCALE).astype(BF16)

    @pl.when(j >= nf + nq)
    def _():
        okv_ref[...] = res.astype(BF16)


def in_proj(x, g, w, l, *, tm):
    m, k = x.shape
    tn = IN_TN
    nf, nq, nkv = IN_F32 // tn, NSA_WIDTH // tn, IN_KV4 // tn
    return pl.pallas_call(
        functools.partial(_in_proj_kernel, nf=nf, nq=nq),
        grid=(m // tm, nf + nq + nkv),
        in_specs=[
            pl.BlockSpec((tm, k), lambda i, j: (i, 0)),
            pl.BlockSpec((1, k), lambda i, j: (0, 0)),
            pl.BlockSpec((None, k, tn), lambda i, j: (l, 0, j)),
        ],
        out_specs=[
            pl.BlockSpec((tm, tn), lambda i, j: (i, jnp.minimum(j, nf - 1))),
            pl.BlockSpec((tm, tn), lambda i, j: (i, jnp.clip(j - nf, 0, nq - 1))),
            pl.BlockSpec((tm, tn), lambda i, j: (i, jnp.clip(j - nf - nq, 0, nkv - 1))),
        ],
        out_shape=[
            jax.ShapeDtypeStruct((m, IN_F32), F32),
            jax.ShapeDtypeStruct((m, NSA_WIDTH), BF16),
            jax.ShapeDtypeStruct((m, IN_KV4), BF16),
        ],
        scratch_shapes=[pltpu.VMEM((tm, k), BF16)],
        compiler_params=_cparams(("parallel", "arbitrary")),
        name="in_proj",
    )(x, g.reshape(1, k), w)


def _bucket(d):
    n = jnp.maximum(d, 0)
    exact = REL_BUCKETS // 2
    nf = jnp.maximum(n, 1).astype(F32)
    large = exact + (jnp.log(nf / exact) / math.log(REL_MAX_DIST / exact)
                     * (REL_BUCKETS - exact)).astype(jnp.int32)
    return jnp.where(n < exact, n, jnp.minimum(large, REL_BUCKETS - 1))


def _lookup(tab_ref, h, bk):
    out = jnp.full(bk.shape, tab_ref[h, 0], F32)
    for b in range(1, REL_BUCKETS):
        out = jnp.where(bk == b, tab_ref[h, b], out)
    return out


def _bias_tables_kernel(tab_ref, ts_ref, wb_ref, bc_ref, *, ls, off, lw, ncw, cmp_shift):
    h = pl.program_id(0)
    r = lax.broadcasted_iota(jnp.int32, (LANES, Q_BLK), 0)
    qi = lax.broadcasted_iota(jnp.int32, (LANES, Q_BLK), 1)

    def chunked(ref, n_rows, dist, valid):
        def body(ch, carry):
            d = dist(ch * LANES + r)
            val = _lookup(tab_ref, h, _bucket(d))
            ref[0, pl.ds(pl.multiple_of(ch * LANES, LANES), LANES), :] = jnp.where(valid(d), val * LOG2E, NEG)
            return carry
        lax.fori_loop(0, n_rows // LANES, body, 0)

    chunked(ts_ref, ls, lambda c: qi - c + off, lambda d: d >= 0)
    chunked(wb_ref, lw, lambda c: qi - c + WIN, lambda d: (d >= 0) & (d < WIN))
    chunked(bc_ref, ncw, lambda m: qi - CMP_STRIDE * m + cmp_shift, lambda d: d >= 0)


def _table_sizes(s, tk):
    off = REL_MAX_DIST + tk
    ls = off + 3 * tk
    lw = 2 * WIN + Q_BLK
    return off, ls, lw


def bias_tables(rel_bias, *, s, tk):
    off, ls, lw = _table_sizes(s, tk)
    ncw = s // CMP_STRIDE
    cmp_shift = CMP_STRIDE * (ncw - 8) - (CMP_LEN - 1)
    kern = functools.partial(_bias_tables_kernel, ls=ls, off=off, lw=lw, ncw=ncw, cmp_shift=cmp_shift)
    return pl.pallas_call(
        kern,
        grid=(NSA_HEADS,),
        in_specs=[pl.BlockSpec(memory_space=pltpu.SMEM)],
        out_specs=[
            pl.BlockSpec((1, ls, Q_BLK), lambda h: (h, 0, 0)),
            pl.BlockSpec((1, lw, Q_BLK), lambda h: (h, 0, 0)),
            pl.BlockSpec((1, ncw, Q_BLK), lambda h: (h, 0, 0)),
        ],
        out_shape=[
            jax.ShapeDtypeStruct((NSA_HEADS, ls, Q_BLK), F32),
            jax.ShapeDtypeStruct((NSA_HEADS, lw, Q_BLK), F32),
            jax.ShapeDtypeStruct((NSA_HEADS, ncw, Q_BLK), F32),
        ],
        compiler_params=_cparams(("arbitrary",)),
        name="bias_tables",
    )(rel_bias)


def _gelu_tanh(x):
    return 0.5 * x * (1.0 + jnp.tanh(math.sqrt(2.0 / math.pi) * (x + 0.044715 * x * x * x)))


def _compress_kernel(c_ref, pe_ref, w1_ref, w2_ref, o_ref, scr_ref, *, nc):
    c = c_ref[0, 0]
    ca = (c + pe_ref[0, 0]).astype(BF16)
    cb = (c + pe_ref[0, 1]).astype(BF16)
    h1 = _dot(ca, w1_ref[0, 0])
    h2 = _dot(cb, w1_ref[0, 1])
    scr_ref[pl.ds(0, nc), :] = h2
    scr_ref[pl.ds(nc, 8), :] = jnp.zeros((8, CMP_HID), F32)
    hid = h1 + scr_ref[pl.ds(1, nc), :]
    out = _dot(_gelu_tanh(hid).astype(BF16), w2_ref[0])
    row = lax.broadcasted_iota(jnp.int32, out.shape, 0)
    o_ref[0, 0] = jnp.where(row < nc - 1, out, 0.0)


def compress(c, pe, w1, w2, l, *, s):
    nc = s // CMP_STRIDE
    half = CMP_STRIDE * HEAD_DIM
    return pl.pallas_call(
        functools.partial(_compress_kernel, nc=nc),
        grid=(2, NSA_GROUPS),
        in_specs=[
            pl.BlockSpec((1, 1, nc, half), lambda a, g: (a, g, 0, 0)),
            pl.BlockSpec((None, 1, 2, 1, half), lambda a, g: (l, a, 0, 0, 0)),
            pl.BlockSpec((None, 1, 2, half, CMP_HID), lambda a, g: (l, a, 0, 0, 0)),
            pl.BlockSpec((None, 1, CMP_HID, HEAD_DIM), lambda a, g: (l, a, 0, 0)),
        ],
        out_specs=pl.BlockSpec((1, 1, nc, HEAD_DIM), lambda a, g: (a, g, 0, 0)),
        out_shape=jax.ShapeDtypeStruct((2, NSA_GROUPS, nc, HEAD_DIM), F32),
        scratch_shapes=[pltpu.VMEM((nc + 8, CMP_HID), F32)],
        compiler_params=_cparams(("arbitrary", "arbitrary")),
        name="compress_kv",
    )(c, pe, w1, w2)


V_ROWS = HEAD_DIM + 16


def _lanes4(fn):
    return jnp.concatenate([fn(hh) for hh in range(NSA_HPG)], axis=-1)


def _nsa_kernel(q_ref, kvc_ref, ks_ref, vs_ref, kw_ref, vw_ref, gl_ref, ts_ref, wb_ref, bc_ref,
                ov_ref, o_ref, mask_ref, sa_ref, sb_ref, sw_ref, *, tk, ncw, nsel, off, nb):
    rows1 = NSA_HPG * Q_BLK
    rows = nb * rows1
    ib = [nb * pl.program_id(1) + b for b in range(nb)]
    q0b = [i * Q_BLK for i in ib]

    def blk(b):
        return slice(b * rows1, (b + 1) * rows1)

    def per_block(fn):
        return jnp.concatenate([fn(b) for b in range(nb)], axis=-1)

    def table_rows(ref, row0, n):
        return per_block(lambda b: _lanes4(lambda hh: ref[hh, pl.ds(row0(b), n), :]))

    qt = per_block(lambda b: q_ref[0, b])
    qa = jnp.concatenate([qt, jnp.zeros_like(qt)], axis=0)

    wk = WIN + Q_BLK
    startb = [pl.multiple_of(jnp.maximum(q0 - WIN, 0), Q_BLK) for q0 in q0b]
    for b in range(nb):
        sw_ref[:, blk(b)] = _dot(kw_ref[0, pl.ds(startb[b], wk), :], qa[:, blk(b)])

    tc = min(tk, ncw)
    neg_row = jnp.where(lax.broadcasted_iota(jnp.int32, (AUG_ROWS, rows), 0) == 0, NEG, 0.0).astype(BF16)
    qa_c = jnp.concatenate([qt, neg_row, jnp.zeros((LANES - HEAD_DIM - AUG_ROWS, rows), BF16)], axis=0)
    ones_rows = (lax.broadcasted_iota(jnp.int32, (V_ROWS - HEAD_DIM, tc), 0) == 0).astype(BF16)

    def cmp_rows(b, t):
        return pl.ds(pl.multiple_of(8 * ib[b] + t * tc, 8), tc)

    def cmp_scores(buf, t):
        for b in range(nb):
            buf[pl.ds(0, tc), blk(b)] = _dot(kvc_ref[0, 0, cmp_rows(b, t), :].astype(BF16), qa_c[:, blk(b)])

    def cmp_tile(buf, t, carry):
        m, acc, impa = carry
        s = buf[pl.ds(0, tc), :] + table_rows(bc_ref, lambda b: t * tc, tc)
        m_new = jnp.maximum(m, jnp.max(s, axis=0, keepdims=True))
        alpha = jnp.exp2(m - m_new)
        p = jnp.exp2(s - m_new).astype(BF16)

        def pv(b):
            vc = kvc_ref[0, 1, cmp_rows(b, t), :]
            vct = jnp.concatenate([vc.T[:HEAD_DIM, :].astype(BF16), ones_rows], axis=0)
            return _dot(vct, p[:, blk(b)])

        return (m_new, alpha * acc + per_block(pv),
                alpha * impa + _dot(ov_ref[:, t * tc:(t + 1) * tc], p))

    cmp_bufs = (sa_ref, sb_ref)
    n_ct = ncw // tc
    cmp_scores(cmp_bufs[0], 0)

    carry_c = (jnp.full((1, rows), NEG, F32), jnp.zeros((V_ROWS, rows), F32), jnp.zeros((nsel, rows), F32))
    for t in range(n_ct):
        if t + 1 < n_ct:
            cmp_scores(cmp_bufs[(t + 1) % 2], t + 1)
        carry_c = cmp_tile(cmp_bufs[t % 2], t, carry_c)
    m_c, acc_c, imp_c = carry_c
    inv = jnp.where(m_c > 0.5 * NEG, 1.0 / acc_c[HEAD_DIM:HEAD_DIM + 1, :], 0.0)
    o_c = acc_c[:HEAD_DIM, :] * inv
    imp4 = imp_c * inv

    def head_sum(b):
        c = [imp4[:, b * rows1 + hh * Q_BLK:b * rows1 + (hh + 1) * Q_BLK] for hh in range(NSA_HPG)]
        return (c[0] + c[1]) + (c[2] + c[3])
    imp = per_block(head_sum)

    s_w = sw_ref[...] + table_rows(wb_ref, lambda b: pl.multiple_of(WIN - (q0b[b] - startb[b]), Q_BLK), wk)
    m_w = jnp.max(s_w, axis=0, keepdims=True)
    e_w = jnp.exp2(s_w - m_w).astype(BF16)
    oa_w = per_block(lambda b: _dot(vw_ref[0, :, pl.ds(startb[b], wk)], e_w[:, blk(b)]))
    o_w = oa_w[:HEAD_DIM, :] * (1.0 / oa_w[HEAD_DIM:HEAD_DIM + 1, :])

    nq = nb * Q_BLK
    jp = lax.broadcasted_iota(jnp.int32, (nsel, nq), 0)
    lane = lax.broadcasted_iota(jnp.int32, (nsel, nq), 1)
    i_lane = nb * pl.program_id(1) + (lane >> 7)
    jabs = jp + (2 * i_lane - (nsel - 2))
    cur = 2 * i_lane + ((lane & (Q_BLK - 1)) >= SEL_BLK).astype(jnp.int32)
    val = jnp.where(jabs == cur - 1, FORCE_SCORE, imp)
    val = jnp.where(jabs == cur, 2.0 * FORCE_SCORE, val)
    val = jnp.where(jabs == 0, 3.0 * FORCE_SCORE, val)
    val = jnp.where(jabs > cur, -1.0, val)
    val = jnp.where(jabs < 0, -3.0, val)

    work = val
    for _ in range(SEL_TOPN):
        work = jnp.where(work == jnp.max(work, axis=0, keepdims=True), -1e38, work)
    fast = jnp.where((work < -1e37) & (val >= 0.0), 1.0, 0.0)
    tie_seen = jnp.max(jnp.sum(fast, axis=0, keepdims=True)) > SEL_TOPN + 0.5

    def exact_topn():
        wk_, chosen = val, jnp.zeros((nsel, nq), F32)
        jpf = jp.astype(F32)
        for _ in range(SEL_TOPN):
            mx = jnp.max(wk_, axis=0, keepdims=True)
            jmin = jnp.min(jnp.where(wk_ == mx, jpf, 2.0 * nsel), axis=0, keepdims=True)
            hit = jpf == jmin
            chosen = jnp.where(hit, 1.0, chosen)
            wk_ = jnp.where(hit, -1e38, wk_)
        return jnp.where(val >= 0.0, chosen, 0.0)

    sel = lax.cond(tie_seen, exact_topn, lambda: fast)
    for b in range(nb):
        shift = (2 * ib[b] + 2) & (nsel - 1)
        sel_abs = pltpu.roll(sel[:, b * Q_BLK:(b + 1) * Q_BLK], shift, 0)
        mask_abs = (sel_abs - 1.0) * (-NEG)
        mask_ref[:, blk(b)] = jnp.concatenate([mask_abs] * NSA_HPG, axis=-1)

    nblk = tk // SEL_BLK
    far_bias = table_rows(ts_ref, lambda b: 0, 1)
    far_hi = far_bias.astype(BF16).astype(F32)
    far_lo = (far_bias - far_hi).astype(BF16).astype(F32)
    far_lo2 = (far_bias - far_hi - far_lo).astype(BF16).astype(F32)
    far_rows = jnp.concatenate([far_hi, far_lo, far_lo2, jnp.zeros((8 - FAR_TERMS, rows), F32)], axis=0)
    n_pad = AUG_ROWS - nblk - 8
    aug_pad = [jnp.zeros((n_pad, rows), F32)] if n_pad else []
    rhs_tail = jnp.zeros((LANES - HEAD_DIM - AUG_ROWS, rows), BF16)
    n_tiles = (q0b[-1] + Q_BLK + tk - 1) // tk
    n_pairs = (n_tiles + 1) // 2
    n_far_pairs = (jnp.maximum(q0b[0] - off + tk, 0) // tk) // (2 * PAIRS_PER_ITER) * PAIRS_PER_ITER
    last_tile = ks_ref.shape[1] // tk - 1

    def scores_into(buf, kt):
        kt = jnp.minimum(kt, last_tile)
        k0 = pl.multiple_of(kt * tk, tk)
        slab = mask_ref[pl.ds(pl.multiple_of(kt * nblk, nblk), nblk), :]
        far = jnp.where(kt < 2 * n_far_pairs, far_rows, 0.0)
        aug = jnp.concatenate([slab, far] + aug_pad, axis=0).astype(BF16)
        rhs = jnp.concatenate([qt, aug, rhs_tail], axis=0)
        buf[...] = _dot(ks_ref[0, pl.ds(k0, tk), :], rhs)

    def sel_tile(buf, kt, carry, near):
        m, acc = carry
        k0 = pl.multiple_of(kt * tk, tk)
        s = buf[...]
        vt = vs_ref[0, :, pl.ds(k0, tk)]
        if near:
            s = s + table_rows(
                ts_ref, lambda b: pl.multiple_of(jnp.maximum(off - (q0b[b] - k0), 0), Q_BLK), tk)
        m_new = jnp.maximum(m, jnp.max(s, axis=0, keepdims=True))
        p = jnp.exp2(s - m_new)
        acc = jnp.exp2(m - m_new) * acc + _dot(vt, p.astype(BF16))
        return m_new, acc

    def sel_pairs(it, carry, near, pairs, base):
        for j in range(pairs):
            kt = 2 * (base + pairs * it + j)
            scores_into(sb_ref, kt + 1)
            carry = sel_tile(sa_ref, kt, carry, near)
            scores_into(sa_ref, kt + 2)
            carry = sel_tile(sb_ref, kt + 1, carry, near)
        return carry

    scores_into(sa_ref, 0)
    carry = (jnp.full((1, rows), NEG, F32), jnp.zeros((V_ROWS, rows), F32))
    n_near_long = (n_pairs - n_far_pairs) // PAIRS_PER_ITER
    carry = lax.fori_loop(0, n_far_pairs // PAIRS_PER_ITER,
                          functools.partial(sel_pairs, near=False, pairs=PAIRS_PER_ITER, base=0), carry)
    carry = lax.fori_loop(0, n_near_long,
                          functools.partial(sel_pairs, near=True, pairs=PAIRS_PER_ITER, base=n_far_pairs), carry)
    _, acc_s = lax.fori_loop(n_far_pairs + PAIRS_PER_ITER * n_near_long, n_pairs,
                             functools.partial(sel_pairs, near=True, pairs=1, base=0), carry)
    o_s = acc_s[:HEAD_DIM, :] * (1.0 / acc_s[HEAD_DIM:HEAD_DIM + 1, :])

    gates = _sigmoid(gl_ref[0])
    def gate_row(c):
        return per_block(lambda b: _lanes4(
            lambda hh: gates[3 * hh + c:3 * hh + c + 1, b * Q_BLK:(b + 1) * Q_BLK]))
    o = gate_row(0) * o_c + gate_row(1) * o_s + gate_row(2) * o_w
    for b in range(nb):
        ob = jnp.concatenate([o[:, b * rows1 + hh * Q_BLK:b * rows1 + (hh + 1) * Q_BLK]
                              for hh in range(NSA_HPG)], axis=0)
        o_ref[b * Q_BLK:(b + 1) * Q_BLK, :] = ob.T.astype(o_ref.dtype)


PAIRS_PER_ITER = 3
FAR_TERMS = 3
AUG_ROWS = 16


def _key_aug_lanes(s, tk):
    nblk = tk // SEL_BLK
    aug = np.zeros((s, HEAD_DIM), np.float32)
    aug[np.arange(s), (np.arange(s) % tk) // SEL_BLK] = 1.0
    aug[:, nblk:nblk + FAR_TERMS] = 1.0
    return aug


def _overlap_matrix_t(ncw, nsel):
    m = np.arange(ncw)[None, :]
    j = np.arange(nsel)[:, None]
    lo = np.maximum(m * CMP_STRIDE, j * SEL_BLK)
    hi = np.minimum(m * CMP_STRIDE + CMP_LEN, (j + 1) * SEL_BLK)
    return (np.maximum(hi - lo, 0).astype(np.float32) / CMP_LEN)


def nsa_attention(qt, kvc_pad, ks, vst, kw, vwt, glt, ts, wb, bc, *, s, tk, nb):
    ncw = s // CMP_STRIDE
    nsel = s // SEL_BLK
    off, ls, lw = _table_sizes(s, tk)
    rows1 = NSA_HPG * Q_BLK
    rows = nb * rows1
    ov = jnp.asarray(_overlap_matrix_t(ncw, nsel), BF16)
    kern = functools.partial(_nsa_kernel, tk=tk, ncw=ncw, nsel=nsel, off=off, nb=nb)
    per_group = dict(pipeline_mode=pl.Buffered(1))
    return pl.pallas_call(
        kern,
        grid=(NSA_GROUPS, s // (nb * Q_BLK)),
        in_specs=[
            pl.BlockSpec((1, nb, HEAD_DIM, rows1), lambda g, i: (g, i, 0, 0)),
            pl.BlockSpec((1, 2, 2 * ncw, LANES), lambda g, i: (g, 0, 0, 0), **per_group),
            pl.BlockSpec((1, s, LANES), lambda g, i: (g, 0, 0), **per_group),
            pl.BlockSpec((1, V_ROWS, s), lambda g, i: (g, 0, 0), **per_group),
            pl.BlockSpec((1, s, LANES), lambda g, i: (g, 0, 0), **per_group),
            pl.BlockSpec((1, V_ROWS, s), lambda g, i: (g, 0, 0), **per_group),
            pl.BlockSpec((1, 3 * NSA_HPG, nb * Q_BLK), lambda g, i: (g, 0, i)),
            pl.BlockSpec((NSA_HPG, ls, Q_BLK), lambda g, i: (g, 0, 0), **per_group),
            pl.BlockSpec((NSA_HPG, lw, Q_BLK), lambda g, i: (g, 0, 0), **per_group),
            pl.BlockSpec((NSA_HPG, ncw, Q_BLK), lambda g, i: (g, 0, 0), **per_group),
            pl.BlockSpec((nsel, ncw), lambda g, i: (0, 0), **per_group),
        ],
        out_specs=pl.BlockSpec((nb * Q_BLK, NSA_HPG * HEAD_DIM), lambda g, i: (i, g)),
        out_shape=jax.ShapeDtypeStruct((s, NSA_WIDTH), BF16),
        scratch_shapes=[pltpu.VMEM((nsel, rows), F32), pltpu.VMEM((tk, rows), F32),
                        pltpu.VMEM((tk, rows), F32), pltpu.VMEM((WIN + Q_BLK, rows), F32)],
        compiler_params=_cparams(("arbitrary", "arbitrary")),
        name="nsa_attention",
    )(qt, kvc_pad, ks, vst, kw, vwt, glt, ts, wb, bc, ov)


HALO = 16


def _pool_kernel(u_ref, halo_ref, w_ref, sc_ref, o_ref, ext_ref, *, tm):
    i = pl.program_id(0)
    halo = jnp.where(i > 0, halo_ref[...], 0.0)
    ext_ref[pl.ds(0, HALO), :] = halo
    ext_ref[pl.ds(HALO, tm), :] = u_ref[...]
    t = i * tm + lax.broadcasted_iota(jnp.int32, (tm, 1), 0)
    outs = []
    for gi, w in enumerate(POOL_WINDOWS):
        cols = pl.ds(gi * POOL_GW, POOL_GW)
        x = ext_ref[pl.ds(HALO, tm), cols]
        acc = x
        for k in range(1, w):
            acc = acc + ext_ref[pl.ds(HALO - k, tm), cols]
        cnt = jnp.minimum(t + 1, w).astype(F32)
        y = acc / cnt - x
        outs.append(_dot(y.astype(BF16), w_ref[gi]))
    o_ref[...] = (jnp.concatenate(outs, axis=-1) * sc_ref[...]).astype(o_ref.dtype)


def pool_mixer(z, w_pool, pool_scale, l, *, tm):
    s = z.shape[0]
    ucol = OFF_U // POOL_WIDTH
    return pl.pallas_call(
        functools.partial(_pool_kernel, tm=tm),
        grid=(s // tm,),
        in_specs=[
            pl.BlockSpec((tm, POOL_WIDTH), lambda i: (i, ucol)),
            pl.BlockSpec((HALO, POOL_WIDTH), lambda i: (jnp.maximum(i * (tm // HALO) - 1, 0), ucol)),
            pl.BlockSpec((None, len(POOL_WINDOWS), POOL_GW, POOL_GW), lambda i: (l, 0, 0, 0)),
            pl.BlockSpec((1, POOL_WIDTH), lambda i: (0, 0)),
        ],
        out_specs=pl.BlockSpec((tm, POOL_WIDTH), lambda i: (i, 0)),
        out_shape=jax.ShapeDtypeStruct((s, POOL_WIDTH), BF16),
        scratch_shapes=[pltpu.VMEM((HALO + tm, POOL_WIDTH), F32)],
        compiler_params=_cparams(("parallel",)),
        name="pool_mixer",
    )(z, z, w_pool, pool_scale.reshape(1, POOL_WIDTH))


def _merge_kernel(a_ref, p_ref, ml_ref, x_ref, wa_ref, wp_ref, wo_ref, g_ref, o_ref):
    pa = _dot(a_ref[...], wa_ref[...])
    pp = _dot(p_ref[...], wp_ref[...])
    ml = ml_ref[...]
    y = _sigmoid(ml[:, :D_MODEL]) * pa + _sigmoid(ml[:, D_MODEL:]) * pp
    y = _dot(y.astype(BF16), wo_ref[...])
    o_ref[...] = x_ref[...] + _rms(y, g_ref[...])


def merge_mix(a, p, z, x, wa, wp, wo, g, l, *, tm):
    s = x.shape[0]
    const = lambda i: (0, 0)
    layer = lambda i: (l, 0, 0)
    return pl.pallas_call(
        _merge_kernel,
        grid=(s // tm,),
        in_specs=[
            pl.BlockSpec((tm, NSA_WIDTH), lambda i: (i, 0)),
            pl.BlockSpec((tm, POOL_WIDTH), lambda i: (i, 0)),
            pl.BlockSpec((tm, 2 * D_MODEL), lambda i: (i, OFF_ML // (2 * D_MODEL))),
            pl.BlockSpec((tm, D_MODEL), lambda i: (i, 0)),
            pl.BlockSpec((None, NSA_WIDTH, D_MODEL), layer, pipeline_mode=pl.Buffered(1)),
            pl.BlockSpec((None, POOL_WIDTH, D_MODEL), layer, pipeline_mode=pl.Buffered(1)),
            pl.BlockSpec((None, D_MODEL, D_MODEL), layer, pipeline_mode=pl.Buffered(1)),
            pl.BlockSpec((1, D_MODEL), const),
        ],
        out_specs=pl.BlockSpec((tm, D_MODEL), lambda i: (i, 0)),
        out_shape=jax.ShapeDtypeStruct((s, D_MODEL), F32),
        compiler_params=_cparams(("parallel",)),
        name="merge_mix",
    )(a, p, z, x, wa, wp, wo, g.reshape(1, D_MODEL))


def _xattn_kernel(x_ref, gpre_ref, wq_ref, kv_ref, wo_ref, gpost_ref, o_ref):
    x = x_ref[...]
    h = _rms(x, gpre_ref[...]).astype(BF16)
    q = _dot(h, wq_ref[...]).astype(BF16)
    kv = kv_ref[...]
    outs = []
    for hh in range(X_HEADS):
        lo = hh * X_HEAD_DIM
        sc = _dot_nt(q[:, lo:lo + X_HEAD_DIM], kv[:, lo:lo + X_HEAD_DIM]) * (X_HEAD_DIM ** -0.5)
        m = jnp.max(sc, axis=-1, keepdims=True)
        e = jnp.exp(sc - m)
        p = (e / jnp.sum(e, axis=-1, keepdims=True)).astype(BF16)
        outs.append(_dot(p, kv[:, X_WIDTH + lo:X_WIDTH + lo + X_HEAD_DIM]))
    o = jnp.concatenate(outs, axis=-1).astype(BF16)
    y = _dot(o, wo_ref[...])
    o_ref[...] = x + _rms(y, gpost_ref[...])


def cross_attention(x, gpre, wq, memkv, wo, gpost, l, *, tm):
    s = x.shape[0]
    const = lambda i: (0, 0)
    layer = lambda i: (l, 0, 0)
    return pl.pallas_call(
        _xattn_kernel,
        grid=(s // tm,),
        in_specs=[
            pl.BlockSpec((tm, D_MODEL), lambda i: (i, 0)),
            pl.BlockSpec((1, D_MODEL), const),
            pl.BlockSpec((None, D_MODEL, X_WIDTH), layer),
            pl.BlockSpec((N_MEM, 2 * X_WIDTH), const),
            pl.BlockSpec((None, X_WIDTH, D_MODEL), layer),
            pl.BlockSpec((1, D_MODEL), const),
        ],
        out_specs=pl.BlockSpec((tm, D_MODEL), lambda i: (i, 0)),
        out_shape=jax.ShapeDtypeStruct((s, D_MODEL), F32),
        compiler_params=_cparams(("parallel",)),
        name="cross_attention",
    )(x, gpre.reshape(1, D_MODEL), wq, memkv, wo, gpost.reshape(1, D_MODEL))


def _ffn_kernel(x_ref, gpre_ref, wg_ref, wu_ref, wd_ref, gpost_ref, o_ref, h_ref, acc_ref):
    f = pl.program_id(1)

    @pl.when(f == 0)
    def _():
        h_ref[...] = _rms(x_ref[...], gpre_ref[...]).astype(BF16)
        acc_ref[...] = jnp.zeros_like(acc_ref)

    h = h_ref[...]
    a = _dot(h, wg_ref[...])
    b = _dot(h, wu_ref[...])
    t = (a * _sigmoid(a) * b).astype(BF16)
    acc_ref[...] += _dot(t, wd_ref[...])

    @pl.when(f == pl.num_programs(1) - 1)
    def _():
        o_ref[...] = x_ref[...] + _rms(acc_ref[...], gpost_ref[...])


def ffn(x, gpre, wg, wu, wd, gpost, l, *, tm, tf):
    s = x.shape[0]
    return pl.pallas_call(
        _ffn_kernel,
        grid=(s // tm, D_FF // tf),
        in_specs=[
            pl.BlockSpec((tm, D_MODEL), lambda i, f: (i, 0)),
            pl.BlockSpec((1, D_MODEL), lambda i, f: (0, 0)),
            pl.BlockSpec((None, D_MODEL, tf), lambda i, f: (l, 0, f)),
            pl.BlockSpec((None, D_MODEL, tf), lambda i, f: (l, 0, f)),
            pl.BlockSpec((None, tf, D_MODEL), lambda i, f: (l, f, 0)),
            pl.BlockSpec((1, D_MODEL), lambda i, f: (0, 0)),
        ],
        out_specs=pl.BlockSpec((tm, D_MODEL), lambda i, f: (i, 0)),
        out_shape=jax.ShapeDtypeStruct((s, D_MODEL), F32),
        scratch_shapes=[pltpu.VMEM((tm, D_MODEL), BF16), pltpu.VMEM((tm, D_MODEL), F32)],
        compiler_params=_cparams(("parallel", "arbitrary")),
        name="ffn_swiglu",
    )(x, gpre.reshape(1, D_MODEL), wg, wu, wd, gpost.reshape(1, D_MODEL))


def _tiles(s):
    big = s >= 4096
    return dict(
        tm_in=1024 if big else 256,
        tk_sel=512 if big else 256, nb_nsa=2,
        tm_pool=512 if big else 256,
        tm_merge=256,
        tm_x=512 if big else 256,
        tm_ffn=512 if big else 256, tf=512,
    )


def _pad_w_in(w_in):
    w = w_in.astype(BF16)
    n_gl = 3 * NSA_HEADS
    o_kv = NSA_WIDTH
    o_kv4 = o_kv + 2 * KV_WIDTH
    o_gl = NSA_WIDTH + 6 * KV_WIDTH
    o_u = o_gl + n_gl
    o_ml = o_u + POOL_WIDTH
    pad = jnp.zeros(w.shape[:2] + (GL_PAD - n_gl,), BF16)
    return jnp.concatenate(
        [w[..., o_ml:], w[..., o_u:o_ml], w[..., o_gl:o_u], pad, w[..., o_kv:o_kv4], w[..., :o_kv],
         w[..., o_kv4:o_gl]], axis=-1)


def _forward(x, mem, rel_bias, ln_mix_pre, ln_mix_post, ln_x_pre, ln_x_post, ln_mem,
             ln_ffn_pre, ln_ffn_post, w_in, cmp_pe, cmp_w1, cmp_w2, w_pool, pool_scale,
             w_br_attn, w_br_pool, w_mix_out, w_xq, w_xkv, w_xo, w_gate, w_up, w_down):
    b, s, _ = x.shape
    assert b == 1 and s % 1024 == 0
    depth = w_in.shape[0]
    tl = _tiles(s)
    nc = s // CMP_STRIDE
    nq = s // Q_BLK
    half = CMP_STRIDE * HEAD_DIM
    ts, wb, bc = bias_tables(rel_bias, s=s, tk=tl["tk_sel"])
    key_aug = jnp.asarray(_key_aug_lanes(s, tl["tk_sel"]), BF16)
    v_ones = jnp.zeros((NSA_GROUPS, V_ROWS - HEAD_DIM, s), BF16).at[:, 0, :].set(1.0)
    w_in_b = _pad_w_in(w_in)
    pe_b = cmp_pe.reshape(depth, 2, 2, 1, half)
    w1_b = cmp_w1.reshape(depth, 2, 2, half, CMP_HID).astype(BF16)
    w2_b, w_pool_b = cmp_w2.astype(BF16), w_pool.astype(BF16)
    wa_b, wp_b, wo_b = w_br_attn.astype(BF16), w_br_pool.astype(BF16), w_mix_out.astype(BF16)
    wxq_b, wxkv_b, wxo_b = w_xq.astype(BF16), w_xkv.astype(BF16), w_xo.astype(BF16)
    wg_b, wu_b, wd_b = w_gate.astype(BF16), w_up.astype(BF16), w_down.astype(BF16)
    xc = x[0]
    mem2 = mem[0]
    for l in range(depth):
        z, zq, zkv = in_proj(xc, ln_mix_pre[l], w_in_b, l, tm=tl["tm_in"])
        qt = zq.reshape(nq, Q_BLK, NSA_GROUPS, NSA_HPG, HEAD_DIM).transpose(2, 0, 4, 3, 1)
        qt = qt.reshape(NSA_GROUPS, nq, HEAD_DIM, NSA_HPG * Q_BLK)
        kv = zkv.reshape(s, 4, NSA_GROUPS, HEAD_DIM)
        c = z[:, OFF_CMP:OFF_CMP + 2 * KV_WIDTH].reshape(nc, CMP_STRIDE, 2, NSA_GROUPS, HEAD_DIM)
        c = c.transpose(2, 3, 0, 1, 4).reshape(2, NSA_GROUPS, nc, half)
        kvc = compress(c, pe_b, w1_b, w2_b, l, s=s)
        kvc = jnp.transpose(kvc, (1, 0, 2, 3))
        kvc_pad = jnp.pad(kvc, ((0, 0), (0, 0), (nc - 8, 8), (0, LANES - HEAD_DIM)))
        kvc_pad = kvc_pad.at[:, 0, :nc - 8, HEAD_DIM].set(1.0)
        ks = jnp.concatenate([kv[:, 0], jnp.broadcast_to(key_aug[:, None, :], (s, NSA_GROUPS, HEAD_DIM))],
                             axis=-1).transpose(1, 0, 2)
        kw = jnp.concatenate([kv[:, 2], kv[:, 3]], axis=-1).transpose(1, 0, 2)
        vst = jnp.concatenate([kv[:, 1].transpose(1, 2, 0), v_ones], axis=1)
        vwt = jnp.concatenate([kv[:, 3].transpose(1, 2, 0), v_ones], axis=1)
        glt = z[:, OFF_GL:OFF_GL + 3 * NSA_HEADS].reshape(s, NSA_GROUPS, 3 * NSA_HPG).transpose(1, 2, 0)
        a = nsa_attention(qt, kvc_pad, ks, vst, kw, vwt, glt, ts, wb, bc, s=s, tk=tl["tk_sel"],
                          nb=tl["nb_nsa"])
        p = pool_mixer(z, w_pool_b, pool_scale[l], l, tm=tl["tm_pool"])
        xc = merge_mix(a, p, z, xc, wa_b, wp_b, wo_b, ln_mix_post[l], l, tm=tl["tm_merge"])
        memkv = norm_matmul(mem2, ln_mem[l], wxkv_b, l, tm=N_MEM, tn=512, out_dtype=BF16)
        xc = cross_attention(xc, ln_x_pre[l], wxq_b, memkv, wxo_b, ln_x_post[l], l, tm=tl["tm_x"])
        xc = ffn(xc, ln_ffn_pre[l], wg_b, wu_b, wd_b, ln_ffn_post[l], l, tm=tl["tm_ffn"], tf=tl["tf"])
    return xc[None]


def kernel(x, mem, rel_bias, ln_mix_pre, ln_mix_post, ln_x_pre, ln_x_post, ln_mem, ln_ffn_pre,
           ln_ffn_post, w_in, cmp_pe, cmp_w1, cmp_w2, w_pool, pool_scale, w_br_attn, w_br_pool,
           w_mix_out, w_xq, w_xkv, w_xo, w_gate, w_up, w_down):
    return _forward(x, mem, rel_bias, ln_mix_pre, ln_mix_post, ln_x_pre, ln_x_post, ln_mem,
                    ln_ffn_pre, ln_ffn_post, w_in, cmp_pe, cmp_w1, cmp_w2, w_pool, pool_scale,
                    w_br_attn, w_br_pool, w_mix_out, w_xq, w_xkv, w_xo, w_gate, w_up, w_down)
```

```python
import functools
import math

import numpy as np
import jax
import jax.numpy as jnp
from jax import lax
from jax.experimental import pallas as pl
from jax.experimental.pallas import tpu as pltpu

F32 = jnp.float32
BF16 = jnp.bfloat16

D_MODEL = 2048
DEPTH = 4
N_MEM = 256
NSA_HEADS = 16
NSA_GROUPS = 4
NSA_HPG = 4
HEAD_DIM = 64
NSA_WIDTH = 1024
KV_WIDTH = 256
CMP_LEN = 32
CMP_STRIDE = 16
CMP_HID = 128
SEL_BLK = 64
SEL_TOPN = 16
WIN = 512
Q_BLK = 128
FORCE_SCORE = 1e4
POOL_WINDOWS = (2, 4, 8, 16)
POOL_GW = 256
POOL_WIDTH = 1024
REL_BUCKETS = 32
REL_MAX_DIST = 2048
X_HEADS = 4
X_HEAD_DIM = 128
X_WIDTH = 512
D_FF = 5632
NEG = -1e30
LOG2E = 1.4426950408889634

GL_PAD = 512
OFF_ML, OFF_U, OFF_GL, OFF_CMP = 0, 4096, 5120, 5632
IN_F32 = 6144
IN_KV4 = 4 * KV_WIDTH
IN_PAD = IN_F32 + NSA_WIDTH + IN_KV4
IN_TN = 512

VMEM_LIMIT_V7X = 56 * 1024 * 1024
LANES = 128


def _cparams(sem, flags=None):
    return pltpu.CompilerParams(dimension_semantics=sem, vmem_limit_bytes=VMEM_LIMIT_V7X, flags=flags)


def _rms(x, g):
    ms = jnp.mean(x * x, axis=-1, keepdims=True)
    return x * lax.rsqrt(ms + 1e-6) * g


def _sigmoid(x):
    return 1.0 / (1.0 + jnp.exp(-x))


def _dot(a, b):
    return jnp.dot(a, b, preferred_element_type=F32)


def _dot_nt(a, b):
    return lax.dot_general(a, b, (((1,), (1,)), ((), ())), preferred_element_type=F32)


def _norm_matmul_kernel(x_ref, g_ref, w_ref, o_ref, h_ref):
    @pl.when(pl.program_id(1) == 0)
    def _():
        h_ref[...] = _rms(x_ref[...], g_ref[...]).astype(BF16)

    o_ref[...] = _dot(h_ref[...], w_ref[...]).astype(o_ref.dtype)


def norm_matmul(x, g, w, l, *, tm, tn, out_dtype=F32):
    m, k = x.shape
    n = w.shape[2]
    return pl.pallas_call(
        _norm_matmul_kernel,
        grid=(m // tm, n // tn),
        in_specs=[
            pl.BlockSpec((tm, k), lambda i, j: (i, 0)),
            pl.BlockSpec((1, k), lambda i, j: (0, 0)),
            pl.BlockSpec((None, k, tn), lambda i, j: (l, 0, j)),
        ],
        out_specs=pl.BlockSpec((tm, tn), lambda i, j: (i, j)),
        out_shape=jax.ShapeDtypeStruct((m, n), out_dtype),
        scratch_shapes=[pltpu.VMEM((tm, k), BF16)],
        compiler_params=_cparams(("parallel", "arbitrary")),
        name="norm_matmul",
    )(x, g.reshape(1, k), w)


Q_SCALE = (HEAD_DIM ** -0.5) * LOG2E


def _in_proj_kernel(x_ref, g_ref, w_ref, of_ref, oq_ref, okv_ref, h_ref, *, nf, nq):
    j = pl.program_id(1)

    @pl.when(j == 0)
    def _():
        h_ref[...] = _rms(x_ref[...], g_ref[...]).astype(BF16)

    @pl.when(j < nf)
    def _():
        of_ref[...] = _dot(h_ref[...], w_ref[...])

    @pl.when((j >= nf) & (j < nf + nq))
    def _():
        oq_ref[...] = (_dot(h_ref[...], w_ref[...]) * Q_SCALE).astype(BF16)

    @pl.when(j >= nf + nq)
    def _():
        okv_ref[...] = _dot(h_ref[...], w_ref[...]).astype(BF16)


def in_proj(x, g, w, l, *, tm):
    m, k = x.shape
    tn = IN_TN
    nf, nq, nkv = IN_F32 // tn, NSA_WIDTH // tn, IN_KV4 // tn
    return pl.pallas_call(
        functools.partial(_in_proj_kernel, nf=nf, nq=nq),
        grid=(m // tm, nf + nq + nkv),
        in_specs=[
            pl.BlockSpec((tm, k), lambda i, j: (i, 0)),
            pl.BlockSpec((1, k), lambda i, j: (0, 0)),
            pl.BlockSpec((None, k, tn), lambda i, j: (l, 0, j)),
        ],
        out_specs=[
            pl.BlockSpec((tm, tn), lambda i, j: (i, jnp.minimum(j, nf - 1))),
            pl.BlockSpec((tm, tn), lambda i, j: (i, jnp.clip(j - nf, 0, nq - 1))),
            pl.BlockSpec((tm, tn), lambda i, j: (i, jnp.clip(j - nf - nq, 0, nkv - 1))),
        ],
        out_shape=[
            jax.ShapeDtypeStruct((m, IN_F32), F32),
            jax.ShapeDtypeStruct((m, NSA_WIDTH), BF16),
            jax.ShapeDtypeStruct((m, IN_KV4), BF16),
        ],
        scratch_shapes=[pltpu.VMEM((tm, k), BF16)],
        compiler_params=_cparams(("parallel", "arbitrary")),
        name="in_proj",
    )(x, g.reshape(1, k), w)


def _bucket(d):
    n = jnp.maximum(d, 0)
    exact = REL_BUCKETS // 2
    nf = jnp.maximum(n, 1).astype(F32)
    large = exact + (jnp.log(nf / exact) / math.log(REL_MAX_DIST / exact)
                     * (REL_BUCKETS - exact)).astype(jnp.int32)
    return jnp.where(n < exact, n, jnp.minimum(large, REL_BUCKETS - 1))


def _lookup(tab_ref, h, bk):
    out = jnp.full(bk.shape, tab_ref[h, 0], F32)
    for b in range(1, REL_BUCKETS):
        out = jnp.where(bk == b, tab_ref[h, b], out)
    return out


def _bias_tables_kernel(tab_ref, ts_ref, wb_ref, bc_ref, *, ls, off, lw, ncw, cmp_shift):
    h = pl.program_id(0)
    r = lax.broadcasted_iota(jnp.int32, (LANES, Q_BLK), 0)
    qi = lax.broadcasted_iota(jnp.int32, (LANES, Q_BLK), 1)

    def chunked(ref, n_rows, dist, valid):
        def body(ch, carry):
            d = dist(ch * LANES + r)
            val = _lookup(tab_ref, h, _bucket(d))
            ref[0, pl.ds(pl.multiple_of(ch * LANES, LANES), LANES), :] = jnp.where(valid(d), val * LOG2E, NEG)
            return carry
        lax.fori_loop(0, n_rows // LANES, body, 0)

    chunked(ts_ref, ls, lambda c: qi - c + off, lambda d: d >= 0)
    chunked(wb_ref, lw, lambda c: qi - c + WIN, lambda d: (d >= 0) & (d < WIN))
    chunked(bc_ref, ncw, lambda m: qi - CMP_STRIDE * m + cmp_shift, lambda d: d >= 0)


def _table_sizes(s, tk):
    off = REL_MAX_DIST + tk
    ls = off + 3 * tk
    lw = 2 * WIN + Q_BLK
    return off, ls, lw


def bias_tables(rel_bias, *, s, tk):
    off, ls, lw = _table_sizes(s, tk)
    ncw = s // CMP_STRIDE
    cmp_shift = CMP_STRIDE * (ncw - 8) - (CMP_LEN - 1)
    kern = functools.partial(_bias_tables_kernel, ls=ls, off=off, lw=lw, ncw=ncw, cmp_shift=cmp_shift)
    return pl.pallas_call(
        kern,
        grid=(NSA_HEADS,),
        in_specs=[pl.BlockSpec(memory_space=pltpu.SMEM)],
        out_specs=[
            pl.BlockSpec((1, ls, Q_BLK), lambda h: (h, 0, 0)),
            pl.BlockSpec((1, lw, Q_BLK), lambda h: (h, 0, 0)),
            pl.BlockSpec((1, ncw, Q_BLK), lambda h: (h, 0, 0)),
        ],
        out_shape=[
            jax.ShapeDtypeStruct((NSA_HEADS, ls, Q_BLK), F32),
            jax.ShapeDtypeStruct((NSA_HEADS, lw, Q_BLK), F32),
            jax.ShapeDtypeStruct((NSA_HEADS, ncw, Q_BLK), F32),
        ],
        compiler_params=_cparams(("arbitrary",)),
        name="bias_tables",
    )(rel_bias)


def _gelu_tanh(x):
    return 0.5 * x * (1.0 + jnp.tanh(math.sqrt(2.0 / math.pi) * (x + 0.044715 * x * x * x)))


def _compress_kernel(c_ref, pe_ref, w1_ref, w2_ref, o_ref, scr_ref, *, nc):
    c = c_ref[0, 0]
    ca = (c + pe_ref[0, 0]).astype(BF16)
    cb = (c + pe_ref[0, 1]).astype(BF16)
    h1 = _dot(ca, w1_ref[0, 0])
    h2 = _dot(cb, w1_ref[0, 1])
    scr_ref[pl.ds(0, nc), :] = h2
    scr_ref[pl.ds(nc, 8), :] = jnp.zeros((8, CMP_HID), F32)
    hid = h1 + scr_ref[pl.ds(1, nc), :]
    out = _dot(_gelu_tanh(hid).astype(BF16), w2_ref[0])
    row = lax.broadcasted_iota(jnp.int32, out.shape, 0)
    o_ref[0, 0] = jnp.where(row < nc - 1, out, 0.0)


def compress(c, pe, w1, w2, l, *, s):
    nc = s // CMP_STRIDE
    half = CMP_STRIDE * HEAD_DIM
    return pl.pallas_call(
        functools.partial(_compress_kernel, nc=nc),
        grid=(2, NSA_GROUPS),
        in_specs=[
            pl.BlockSpec((1, 1, nc, half), lambda a, g: (a, g, 0, 0)),
            pl.BlockSpec((None, 1, 2, 1, half), lambda a, g: (l, a, 0, 0, 0)),
            pl.BlockSpec((None, 1, 2, half, CMP_HID), lambda a, g: (l, a, 0, 0, 0)),
            pl.BlockSpec((None, 1, CMP_HID, HEAD_DIM), lambda a, g: (l, a, 0, 0)),
        ],
        out_specs=pl.BlockSpec((1, 1, nc, HEAD_DIM), lambda a, g: (a, g, 0, 0)),
        out_shape=jax.ShapeDtypeStruct((2, NSA_GROUPS, nc, HEAD_DIM), F32),
        scratch_shapes=[pltpu.VMEM((nc + 8, CMP_HID), F32)],
        compiler_params=_cparams(("arbitrary", "arbitrary")),
        name="compress_kv",
    )(c, pe, w1, w2)


V_ROWS = HEAD_DIM + 16


def _lanes4(fn):
    return jnp.concatenate([fn(hh) for hh in range(NSA_HPG)], axis=-1)


def _nsa_kernel(q_ref, kvc_ref, ks_ref, vs_ref, kw_ref, vw_ref, gl_ref, ts_ref, wb_ref, bc_ref,
                ov_ref, o_ref, mask_ref, sa_ref, sb_ref, sw_ref, *, tk, ncw, nsel, off, nb):
    rows1 = NSA_HPG * Q_BLK
    rows = nb * rows1
    ib = [nb * pl.program_id(1) + b for b in range(nb)]
    q0b = [i * Q_BLK for i in ib]

    def blk(b):
        return slice(b * rows1, (b + 1) * rows1)

    def per_block(fn):
        return jnp.concatenate([fn(b) for b in range(nb)], axis=-1)

    def table_rows(ref, row0, n):
        return per_block(lambda b: _lanes4(lambda hh: ref[hh, pl.ds(row0(b), n), :]))

    qt = per_block(lambda b: q_ref[0, b])
    qa = jnp.concatenate([qt, jnp.zeros_like(qt)], axis=0)

    wk = WIN + Q_BLK
    startb = [pl.multiple_of(jnp.maximum(q0 - WIN, 0), Q_BLK) for q0 in q0b]
    for b in range(nb):
        sw_ref[:, blk(b)] = _dot(kw_ref[0, pl.ds(startb[b], wk), :], qa[:, blk(b)])

    tc = min(tk, ncw)
    neg_row = jnp.where(lax.broadcasted_iota(jnp.int32, (AUG_ROWS, rows), 0) == 0, NEG, 0.0).astype(BF16)
    qa_c = jnp.concatenate([qt, neg_row, jnp.zeros((LANES - HEAD_DIM - AUG_ROWS, rows), BF16)], axis=0)
    ones_rows = (lax.broadcasted_iota(jnp.int32, (V_ROWS - HEAD_DIM, tc), 0) == 0).astype(BF16)

    def cmp_rows(b, t):
        return pl.ds(pl.multiple_of(8 * ib[b] + t * tc, 8), tc)

    def cmp_scores(buf, t):
        for b in range(nb):
            buf[pl.ds(0, tc), blk(b)] = _dot(kvc_ref[0, 0, cmp_rows(b, t), :].astype(BF16), qa_c[:, blk(b)])

    def cmp_tile(buf, t, carry):
        m, acc, impa = carry
        s = buf[pl.ds(0, tc), :] + table_rows(bc_ref, lambda b: t * tc, tc)
        m_new = jnp.maximum(m, jnp.max(s, axis=0, keepdims=True))
        alpha = jnp.exp2(m - m_new)
        p = jnp.exp2(s - m_new).astype(BF16)

        def pv(b):
            vc = kvc_ref[0, 1, cmp_rows(b, t), :]
            vct = jnp.concatenate([vc.T[:HEAD_DIM, :].astype(BF16), ones_rows], axis=0)
            return _dot(vct, p[:, blk(b)])

        return (m_new, alpha * acc + per_block(pv),
                alpha * impa + _dot(ov_ref[:, t * tc:(t + 1) * tc], p))

    cmp_bufs = (sa_ref, sb_ref)
    n_ct = ncw // tc
    cmp_scores(cmp_bufs[0], 0)

    carry_c = (jnp.full((1, rows), NEG, F32), jnp.zeros((V_ROWS, rows), F32), jnp.zeros((nsel, rows), F32))
    for t in range(n_ct):
        if t + 1 < n_ct:
            cmp_scores(cmp_bufs[(t + 1) % 2], t + 1)
        carry_c = cmp_tile(cmp_bufs[t % 2], t, carry_c)
    m_c, acc_c, imp_c = carry_c
    inv = jnp.where(m_c > 0.5 * NEG, 1.0 / acc_c[HEAD_DIM:HEAD_DIM + 1, :], 0.0)
    o_c = acc_c[:HEAD_DIM, :] * inv
    imp4 = imp_c * inv

    def head_sum(b):
        c = [imp4[:, b * rows1 + hh * Q_BLK:b * rows1 + (hh + 1) * Q_BLK] for hh in range(NSA_HPG)]
        return (c[0] + c[1]) + (c[2] + c[3])
    imp = per_block(head_sum)

    s_w = sw_ref[...] + table_rows(wb_ref, lambda b: pl.multiple_of(WIN - (q0b[b] - startb[b]), Q_BLK), wk)
    m_w = jnp.max(s_w, axis=0, keepdims=True)
    e_w = jnp.exp2(s_w - m_w).astype(BF16)
    oa_w = per_block(lambda b: _dot(vw_ref[0, :, pl.ds(startb[b], wk)], e_w[:, blk(b)]))
    o_w = oa_w[:HEAD_DIM, :] * (1.0 / oa_w[HEAD_DIM:HEAD_DIM + 1, :])

    nq = nb * Q_BLK
    jp = lax.broadcasted_iota(jnp.int32, (nsel, nq), 0)
    lane = lax.broadcasted_iota(jnp.int32, (nsel, nq), 1)
    i_lane = nb * pl.program_id(1) + (lane >> 7)
    jabs = jp + (2 * i_lane - (nsel - 2))
    cur = 2 * i_lane + ((lane & (Q_BLK - 1)) >= SEL_BLK).astype(jnp.int32)
    val = jnp.where(jabs == cur - 1, FORCE_SCORE, imp)
    val = jnp.where(jabs == cur, 2.0 * FORCE_SCORE, val)
    val = jnp.where(jabs == 0, 3.0 * FORCE_SCORE, val)
    val = jnp.where(jabs > cur, -1.0, val)
    val = jnp.where(jabs < 0, -3.0, val)

    work = val
    for _ in range(SEL_TOPN):
        work = jnp.where(work == jnp.max(work, axis=0, keepdims=True), -1e38, work)
    fast = jnp.where((work < -1e37) & (val >= 0.0), 1.0, 0.0)
    tie_seen = jnp.max(jnp.sum(fast, axis=0, keepdims=True)) > SEL_TOPN + 0.5

    def exact_topn():
        wk_, chosen = val, jnp.zeros((nsel, nq), F32)
        jpf = jp.astype(F32)
        for _ in range(SEL_TOPN):
            mx = jnp.max(wk_, axis=0, keepdims=True)
            jmin = jnp.min(jnp.where(wk_ == mx, jpf, 2.0 * nsel), axis=0, keepdims=True)
            hit = jpf == jmin
            chosen = jnp.where(hit, 1.0, chosen)
            wk_ = jnp.where(hit, -1e38, wk_)
        return jnp.where(val >= 0.0, chosen, 0.0)

    sel = lax.cond(tie_seen, exact_topn, lambda: fast)
    for b in range(nb):
        shift = (2 * ib[b] + 2) & (nsel - 1)
        sel_abs = pltpu.roll(sel[:, b * Q_BLK:(b + 1) * Q_BLK], shift, 0)
        mask_abs = (sel_abs - 1.0) * (-NEG)
        mask_ref[:, blk(b)] = jnp.concatenate([mask_abs] * NSA_HPG, axis=-1)

    nblk = tk // SEL_BLK
    far_bias = table_rows(ts_ref, lambda b: 0, 1)
    far_hi = far_bias.astype(BF16).astype(F32)
    far_lo = (far_bias - far_hi).astype(BF16).astype(F32)
    far_lo2 = (far_bias - far_hi - far_lo).astype(BF16).astype(F32)
    far_rows = jnp.concatenate([far_hi, far_lo, far_lo2, jnp.zeros((8 - FAR_TERMS, rows), F32)], axis=0)
    n_pad = AUG_ROWS - nblk - 8
    aug_pad = [jnp.zeros((n_pad, rows), F32)] if n_pad else []
    rhs_tail = jnp.zeros((LANES - HEAD_DIM - AUG_ROWS, rows), BF16)
    n_tiles = (q0b[-1] + Q_BLK + tk - 1) // tk
    n_pairs = (n_tiles + 1) // 2
    n_far_pairs = (jnp.maximum(q0b[0] - off + tk, 0) // tk) // (2 * PAIRS_PER_ITER) * PAIRS_PER_ITER
    last_tile = ks_ref.shape[1] // tk - 1

    def scores_into(buf, kt):
        kt = jnp.minimum(kt, last_tile)
        k0 = pl.multiple_of(kt * tk, tk)
        slab = mask_ref[pl.ds(pl.multiple_of(kt * nblk, nblk), nblk), :]
        far = jnp.where(kt < 2 * n_far_pairs, far_rows, 0.0)
        aug = jnp.concatenate([slab, far] + aug_pad, axis=0).astype(BF16)
        rhs = jnp.concatenate([qt, aug, rhs_tail], axis=0)
        buf[...] = _dot(ks_ref[0, pl.ds(k0, tk), :], rhs)

    def sel_tile(buf, kt, carry, near):
        m, acc = carry
        k0 = pl.multiple_of(kt * tk, tk)
        s = buf[...]
        vt = vs_ref[0, :, pl.ds(k0, tk)]
        if near:
            s = s + table_rows(
                ts_ref, lambda b: pl.multiple_of(jnp.maximum(off - (q0b[b] - k0), 0), Q_BLK), tk)
        m_new = jnp.maximum(m, jnp.max(s, axis=0, keepdims=True))
        p = jnp.exp2(s - m_new)
        acc = jnp.exp2(m - m_new) * acc + _dot(vt, p.astype(BF16))
        return m_new, acc

    def sel_pairs(it, carry, near, pairs, base):
        for j in range(pairs):
            kt = 2 * (base + pairs * it + j)
            scores_into(sb_ref, kt + 1)
            carry = sel_tile(sa_ref, kt, carry, near)
            scores_into(sa_ref, kt + 2)
            carry = sel_tile(sb_ref, kt + 1, carry, near)
        return carry

    scores_into(sa_ref, 0)
    carry = (jnp.full((1, rows), NEG, F32), jnp.zeros((V_ROWS, rows), F32))
    n_near_long = (n_pairs - n_far_pairs) // PAIRS_PER_ITER
    carry = lax.fori_loop(0, n_far_pairs // PAIRS_PER_ITER,
                          functools.partial(sel_pairs, near=False, pairs=PAIRS_PER_ITER, base=0), carry)
    carry = lax.fori_loop(0, n_near_long,
                          functools.partial(sel_pairs, near=True, pairs=PAIRS_PER_ITER, base=n_far_pairs), carry)
    _, acc_s = lax.fori_loop(n_far_pairs + PAIRS_PER_ITER * n_near_long, n_pairs,
                             functools.partial(sel_pairs, near=True, pairs=1, base=0), carry)
    o_s = acc_s[:HEAD_DIM, :] * (1.0 / acc_s[HEAD_DIM:HEAD_DIM + 1, :])

    gates = _sigmoid(gl_ref[0])
    def gate_row(c):
        return per_block(lambda b: _lanes4(
            lambda hh: gates[3 * hh + c:3 * hh + c + 1, b * Q_BLK:(b + 1) * Q_BLK]))
    o = gate_row(0) * o_c + gate_row(1) * o_s + gate_row(2) * o_w
    for b in range(nb):
        ob = jnp.concatenate([o[:, b * rows1 + hh * Q_BLK:b * rows1 + (hh + 1) * Q_BLK]
                              for hh in range(NSA_HPG)], axis=0)
        o_ref[b * Q_BLK:(b + 1) * Q_BLK, :] = ob.T.astype(o_ref.dtype)


PAIRS_PER_ITER = 3
FAR_TERMS = 3
AUG_ROWS = 16


def _key_aug_lanes(s, tk):
    nblk = tk // SEL_BLK
    aug = np.zeros((s, HEAD_DIM), np.float32)
    aug[np.arange(s), (np.arange(s) % tk) // SEL_BLK] = 1.0
    aug[:, nblk:nblk + FAR_TERMS] = 1.0
    return aug


def _overlap_matrix_t(ncw, nsel):
    m = np.arange(ncw)[None, :]
    j = np.arange(nsel)[:, None]
    lo = np.maximum(m * CMP_STRIDE, j * SEL_BLK)
    hi = np.minimum(m * CMP_STRIDE + CMP_LEN, (j + 1) * SEL_BLK)
    return (np.maximum(hi - lo, 0).astype(np.float32) / CMP_LEN)


def nsa_attention(qt, kvc_pad, ks, vst, kw, vwt, glt, ts, wb, bc, *, s, tk, nb):
    ncw = s // CMP_STRIDE
    nsel = s // SEL_BLK
    off, ls, lw = _table_sizes(s, tk)
    rows1 = NSA_HPG * Q_BLK
    rows = nb * rows1
    ov = jnp.asarray(_overlap_matrix_t(ncw, nsel), BF16)
    kern = functools.partial(_nsa_kernel, tk=tk, ncw=ncw, nsel=nsel, off=off, nb=nb)
    per_group = dict(pipeline_mode=pl.Buffered(1))
    return pl.pallas_call(
        kern,
        grid=(NSA_GROUPS, s // (nb * Q_BLK)),
        in_specs=[
            pl.BlockSpec((1, nb, HEAD_DIM, rows1), lambda g, i: (g, i, 0, 0)),
            pl.BlockSpec((1, 2, 2 * ncw, LANES), lambda g, i: (g, 0, 0, 0), **per_group),
            pl.BlockSpec((1, s, LANES), lambda g, i: (g, 0, 0), **per_group),
            pl.BlockSpec((1, V_ROWS, s), lambda g, i: (g, 0, 0), **per_group),
            pl.BlockSpec((1, s, LANES), lambda g, i: (g, 0, 0), **per_group),
            pl.BlockSpec((1, V_ROWS, s), lambda g, i: (g, 0, 0), **per_group),
            pl.BlockSpec((1, 3 * NSA_HPG, nb * Q_BLK), lambda g, i: (g, 0, i)),
            pl.BlockSpec((NSA_HPG, ls, Q_BLK), lambda g, i: (g, 0, 0), **per_group),
            pl.BlockSpec((NSA_HPG, lw, Q_BLK), lambda g, i: (g, 0, 0), **per_group),
            pl.BlockSpec((NSA_HPG, ncw, Q_BLK), lambda g, i: (g, 0, 0), **per_group),
            pl.BlockSpec((nsel, ncw), lambda g, i: (0, 0), **per_group),
        ],
        out_specs=pl.BlockSpec((nb * Q_BLK, NSA_HPG * HEAD_DIM), lambda g, i: (i, g)),
        out_shape=jax.ShapeDtypeStruct((s, NSA_WIDTH), BF16),
        scratch_shapes=[pltpu.VMEM((nsel, rows), F32), pltpu.VMEM((tk, rows), F32),
                        pltpu.VMEM((tk, rows), F32), pltpu.VMEM((WIN + Q_BLK, rows), F32)],
        compiler_params=_cparams(("arbitrary", "arbitrary")),
        name="nsa_attention",
    )(qt, kvc_pad, ks, vst, kw, vwt, glt, ts, wb, bc, ov)


HALO = 16


def _pool_kernel(u_ref, halo_ref, w_ref, sc_ref, o_ref, ext_ref, *, tm):
    i = pl.program_id(0)
    halo = jnp.where(i > 0, halo_ref[...], 0.0)
    ext_ref[pl.ds(0, HALO), :] = halo
    ext_ref[pl.ds(HALO, tm), :] = u_ref[...]
    t = i * tm + lax.broadcasted_iota(jnp.int32, (tm, 1), 0)
    outs = []
    for gi, w in enumerate(POOL_WINDOWS):
        cols = pl.ds(gi * POOL_GW, POOL_GW)
        x = ext_ref[pl.ds(HALO, tm), cols]
        acc = x
        for k in range(1, w):
            acc = acc + ext_ref[pl.ds(HALO - k, tm), cols]
        cnt = jnp.minimum(t + 1, w).astype(F32)
        y = acc / cnt - x
        outs.append(_dot(y.astype(BF16), w_ref[gi]))
    o_ref[...] = (jnp.concatenate(outs, axis=-1) * sc_ref[...]).astype(o_ref.dtype)


def pool_mixer(z, w_pool, pool_scale, l, *, tm):
    s = z.shape[0]
    ucol = OFF_U // POOL_WIDTH
    return pl.pallas_call(
        functools.partial(_pool_kernel, tm=tm),
        grid=(s // tm,),
        in_specs=[
            pl.BlockSpec((tm, POOL_WIDTH), lambda i: (i, ucol)),
            pl.BlockSpec((HALO, POOL_WIDTH), lambda i: (jnp.maximum(i * (tm // HALO) - 1, 0), ucol)),
            pl.BlockSpec((None, len(POOL_WINDOWS), POOL_GW, POOL_GW), lambda i: (l, 0, 0, 0)),
            pl.BlockSpec((1, POOL_WIDTH), lambda i: (0, 0)),
        ],
        out_specs=pl.BlockSpec((tm, POOL_WIDTH), lambda i: (i, 0)),
        out_shape=jax.ShapeDtypeStruct((s, POOL_WIDTH), BF16),
        scratch_shapes=[pltpu.VMEM((HALO + tm, POOL_WIDTH), F32)],
        compiler_params=_cparams(("parallel",)),
        name="pool_mixer",
    )(z, z, w_pool, pool_scale.reshape(1, POOL_WIDTH))


def _merge_kernel(a_ref, p_ref, ml_ref, x_ref, wa_ref, wp_ref, wo_ref, g_ref, o_ref):
    pa = _dot(a_ref[...], wa_ref[...])
    pp = _dot(p_ref[...], wp_ref[...])
    ml = ml_ref[...]
    y = _sigmoid(ml[:, :D_MODEL]) * pa + _sigmoid(ml[:, D_MODEL:]) * pp
    y = _dot(y.astype(BF16), wo_ref[...])
    o_ref[...] = x_ref[...] + _rms(y, g_ref[...])


def merge_mix(a, p, z, x, wa, wp, wo, g, l, *, tm):
    s = x.shape[0]
    const = lambda i: (0, 0)
    layer = lambda i: (l, 0, 0)
    return pl.pallas_call(
        _merge_kernel,
        grid=(s // tm,),
        in_specs=[
            pl.BlockSpec((tm, NSA_WIDTH), lambda i: (i, 0)),
            pl.BlockSpec((tm, POOL_WIDTH), lambda i: (i, 0)),
            pl.BlockSpec((tm, 2 * D_MODEL), lambda i: (i, OFF_ML // (2 * D_MODEL))),
            pl.BlockSpec((tm, D_MODEL), lambda i: (i, 0)),
            pl.BlockSpec((None, NSA_WIDTH, D_MODEL), layer, pipeline_mode=pl.Buffered(1)),
            pl.BlockSpec((None, POOL_WIDTH, D_MODEL), layer, pipeline_mode=pl.Buffered(1)),
            pl.BlockSpec((None, D_MODEL, D_MODEL), layer, pipeline_mode=pl.Buffered(1)),
            pl.BlockSpec((1, D_MODEL), const),
        ],
        out_specs=pl.BlockSpec((tm, D_MODEL), lambda i: (i, 0)),
        out_shape=jax.ShapeDtypeStruct((s, D_MODEL), F32),
        compiler_params=_cparams(("parallel",)),
        name="merge_mix",
    )(a, p, z, x, wa, wp, wo, g.reshape(1, D_MODEL))


def _xattn_kernel(x_ref, gpre_ref, wq_ref, kv_ref, wo_ref, gpost_ref, o_ref):
    x = x_ref[...]
    h = _rms(x, gpre_ref[...]).astype(BF16)
    q = _dot(h, wq_ref[...]).astype(BF16)
    kv = kv_ref[...]
    outs = []
    for hh in range(X_HEADS):
        lo = hh * X_HEAD_DIM
        sc = _dot_nt(q[:, lo:lo + X_HEAD_DIM], kv[:, lo:lo + X_HEAD_DIM]) * (X_HEAD_DIM ** -0.5)
        m = jnp.max(sc, axis=-1, keepdims=True)
        e = jnp.exp(sc - m)
        p = (e / jnp.sum(e, axis=-1, keepdims=True)).astype(BF16)
        outs.append(_dot(p, kv[:, X_WIDTH + lo:X_WIDTH + lo + X_HEAD_DIM]))
    o = jnp.concatenate(outs, axis=-1).astype(BF16)
    y = _dot(o, wo_ref[...])
    o_ref[...] = x + _rms(y, gpost_ref[...])


def cross_attention(x, gpre, wq, memkv, wo, gpost, l, *, tm):
    s = x.shape[0]
    const = lambda i: (0, 0)
    layer = lambda i: (l, 0, 0)
    return pl.pallas_call(
        _xattn_kernel,
        grid=(s // tm,),
        in_specs=[
            pl.BlockSpec((tm, D_MODEL), lambda i: (i, 0)),
            pl.BlockSpec((1, D_MODEL), const),
            pl.BlockSpec((None, D_MODEL, X_WIDTH), layer),
            pl.BlockSpec((N_MEM, 2 * X_WIDTH), const),
            pl.BlockSpec((None, X_WIDTH, D_MODEL), layer),
            pl.BlockSpec((1, D_MODEL), const),
        ],
        out_specs=pl.BlockSpec((tm, D_MODEL), lambda i: (i, 0)),
        out_shape=jax.ShapeDtypeStruct((s, D_MODEL), F32),
        compiler_params=_cparams(("parallel",)),
        name="cross_attention",
    )(x, gpre.reshape(1, D_MODEL), wq, memkv, wo, gpost.reshape(1, D_MODEL))


def _ffn_kernel(x_ref, gpre_ref, wg_ref, wu_ref, wd_ref, gpost_ref, o_ref, h_ref, acc_ref):
    f = pl.program_id(1)

    @pl.when(f == 0)
    def _():
        h_ref[...] = _rms(x_ref[...], gpre_ref[...]).astype(BF16)
        acc_ref[...] = jnp.zeros_like(acc_ref)

    h = h_ref[...]
    a = _dot(h, wg_ref[...])
    b = _dot(h, wu_ref[...])
    t = (a * _sigmoid(a) * b).astype(BF16)
    acc_ref[...] += _dot(t, wd_ref[...])

    @pl.when(f == pl.num_programs(1) - 1)
    def _():
        o_ref[...] = x_ref[...] + _rms(acc_ref[...], gpost_ref[...])


def ffn(x, gpre, wg, wu, wd, gpost, l, *, tm, tf):
    s = x.shape[0]
    return pl.pallas_call(
        _ffn_kernel,
        grid=(s // tm, D_FF // tf),
        in_specs=[
            pl.BlockSpec((tm, D_MODEL), lambda i, f: (i, 0)),
            pl.BlockSpec((1, D_MODEL), lambda i, f: (0, 0)),
            pl.BlockSpec((None, D_MODEL, tf), lambda i, f: (l, 0, f)),
            pl.BlockSpec((None, D_MODEL, tf), lambda i, f: (l, 0, f)),
            pl.BlockSpec((None, tf, D_MODEL), lambda i, f: (l, f, 0)),
            pl.BlockSpec((1, D_MODEL), lambda i, f: (0, 0)),
        ],
        out_specs=pl.BlockSpec((tm, D_MODEL), lambda i, f: (i, 0)),
        out_shape=jax.ShapeDtypeStruct((s, D_MODEL), F32),
        scratch_shapes=[pltpu.VMEM((tm, D_MODEL), BF16), pltpu.VMEM((tm, D_MODEL), F32)],
        compiler_params=_cparams(("parallel", "arbitrary")),
        name="ffn_swiglu",
    )(x, gpre.reshape(1, D_MODEL), wg, wu, wd, gpost.reshape(1, D_MODEL))


def _tiles(s):
    big = s >= 4096
    return dict(
        tm_in=1024 if big else 256,
        tk_sel=512 if big else 256, nb_nsa=2,
        tm_pool=512 if big else 256,
        tm_merge=256,
        tm_x=512 if big else 256,
        tm_ffn=512 if big else 256, tf=512,
    )


def _pad_w_in(w_in):
    w = w_in.astype(BF16)
    n_gl = 3 * NSA_HEADS
    o_kv = NSA_WIDTH
    o_kv4 = o_kv + 2 * KV_WIDTH
    o_gl = NSA_WIDTH + 6 * KV_WIDTH
    o_u = o_gl + n_gl
    o_ml = o_u + POOL_WIDTH
    pad = jnp.zeros(w.shape[:2] + (GL_PAD - n_gl,), BF16)
    return jnp.concatenate(
        [w[..., o_ml:], w[..., o_u:o_ml], w[..., o_gl:o_u], pad, w[..., o_kv:o_kv4], w[..., :o_kv],
         w[..., o_kv4:o_gl]], axis=-1)


def _forward(x, mem, rel_bias, ln_mix_pre, ln_mix_post, ln_x_pre, ln_x_post, ln_mem,
             ln_ffn_pre, ln_ffn_post, w_in, cmp_pe, cmp_w1, cmp_w2, w_pool, pool_scale,
             w_br_attn, w_br_pool, w_mix_out, w_xq, w_xkv, w_xo, w_gate, w_up, w_down):
    b, s, _ = x.shape
    assert b == 1 and s % 1024 == 0
    depth = w_in.shape[0]
    tl = _tiles(s)
    nc = s // CMP_STRIDE
    nq = s // Q_BLK
    half = CMP_STRIDE * HEAD_DIM
    ts, wb, bc = bias_tables(rel_bias, s=s, tk=tl["tk_sel"])
    key_aug = jnp.asarray(_key_aug_lanes(s, tl["tk_sel"]), BF16)
    v_ones = jnp.zeros((NSA_GROUPS, V_ROWS - HEAD_DIM, s), BF16).at[:, 0, :].set(1.0)
    w_in_b = _pad_w_in(w_in)
    pe_b = cmp_pe.reshape(depth, 2, 2, 1, half)
    w1_b = cmp_w1.reshape(depth, 2, 2, half, CMP_HID).astype(BF16)
    w2_b, w_pool_b = cmp_w2.astype(BF16), w_pool.astype(BF16)
    wa_b, wp_b, wo_b = w_br_attn.astype(BF16), w_br_pool.astype(BF16), w_mix_out.astype(BF16)
    wxq_b, wxkv_b, wxo_b = w_xq.astype(BF16), w_xkv.astype(BF16), w_xo.astype(BF16)
    wg_b, wu_b, wd_b = w_gate.astype(BF16), w_up.astype(BF16), w_down.astype(BF16)
    xc = x[0]
    mem2 = mem[0]
    for l in range(depth):
        z, zq, zkv = in_proj(xc, ln_mix_pre[l], w_in_b, l, tm=tl["tm_in"])
        qt = zq.reshape(nq, Q_BLK, NSA_GROUPS, NSA_HPG, HEAD_DIM).transpose(2, 0, 4, 3, 1)
        qt = qt.reshape(NSA_GROUPS, nq, HEAD_DIM, NSA_HPG * Q_BLK)
        kv = zkv.reshape(s, 4, NSA_GROUPS, HEAD_DIM)
        c = z[:, OFF_CMP:OFF_CMP + 2 * KV_WIDTH].reshape(nc, CMP_STRIDE, 2, NSA_GROUPS, HEAD_DIM)
        c = c.transpose(2, 3, 0, 1, 4).reshape(2, NSA_GROUPS, nc, half)
        kvc = compress(c, pe_b, w1_b, w2_b, l, s=s)
        kvc = jnp.transpose(kvc, (1, 0, 2, 3))
        kvc_pad = jnp.pad(kvc, ((0, 0), (0, 0), (nc - 8, 8), (0, LANES - HEAD_DIM)))
        kvc_pad = kvc_pad.at[:, 0, :nc - 8, HEAD_DIM].set(1.0)
        ks = jnp.concatenate([kv[:, 0], jnp.broadcast_to(key_aug[:, None, :], (s, NSA_GROUPS, HEAD_DIM))],
                             axis=-1).transpose(1, 0, 2)
        kw = jnp.concatenate([kv[:, 2], kv[:, 3]], axis=-1).transpose(1, 0, 2)
        vst = jnp.concatenate([kv[:, 1].transpose(1, 2, 0), v_ones], axis=1)
        vwt = jnp.concatenate([kv[:, 3].transpose(1, 2, 0), v_ones], axis=1)
        glt = z[:, OFF_GL:OFF_GL + 3 * NSA_HEADS].reshape(s, NSA_GROUPS, 3 * NSA_HPG).transpose(1, 2, 0)
        a = nsa_attention(qt, kvc_pad, ks, vst, kw, vwt, glt, ts, wb, bc, s=s, tk=tl["tk_sel"],
                          nb=tl["nb_nsa"])
        p = pool_mixer(z, w_pool_b, pool_scale[l], l, tm=tl["tm_pool"])
        xc = merge_mix(a, p, z, xc, wa_b, wp_b, wo_b, ln_mix_post[l], l, tm=tl["tm_merge"])
        memkv = norm_matmul(mem2, ln_mem[l], wxkv_b, l, tm=N_MEM, tn=512, out_dtype=BF16)
        xc = cross_attention(xc, ln_x_pre[l], wxq_b, memkv, wxo_b, ln_x_post[l], l, tm=tl["tm_x"])
        xc = ffn(xc, ln_ffn_pre[l], wg_b, wu_b, wd_b, ln_ffn_post[l], l, tm=tl["tm_ffn"], tf=tl["tf"])
    return xc[None]


def kernel(x, mem, rel_bias, ln_mix_pre, ln_mix_post, ln_x_pre, ln_x_post, ln_mem, ln_ffn_pre,
           ln_ffn_post, w_in, cmp_pe, cmp_w1, cmp_w2, w_pool, pool_scale, w_br_attn, w_br_pool,
           w_mix_out, w_xq, w_xkv, w_xo, w_gate, w_up, w_down):
    return _forward(x, mem, rel_bias, ln_mix_pre, ln_mix_post, ln_x_pre, ln_x_post, ln_mem,
                    ln_ffn_pre, ln_ffn_post, w_in, cmp_pe, cmp_w1, cmp_w2, w_pool, pool_scale,
                    w_br_attn, w_br_pool, w_mix_out, w_xq, w_xkv, w_xo, w_gate, w_up, w_down)
```

```python
import functools
import math

import numpy as np
import jax
import jax.numpy as jnp
from jax import lax
from jax.experimental import pallas as pl
from jax.experimental.pallas import tpu as pltpu

F32 = jnp.float32
BF16 = jnp.bfloat16

D_MODEL = 2048
DEPTH = 4
N_MEM = 256
NSA_HEADS = 16
NSA_GROUPS = 4
NSA_HPG = 4
HEAD_DIM = 64
NSA_WIDTH = 1024
KV_WIDTH = 256
CMP_LEN = 32
CMP_STRIDE = 16
CMP_HID = 128
SEL_BLK = 64
SEL_TOPN = 16
WIN = 512
Q_BLK = 128
FORCE_SCORE = 1e4
POOL_WINDOWS = (2, 4, 8, 16)
POOL_GW = 256
POOL_WIDTH = 1024
REL_BUCKETS = 32
REL_MAX_DIST = 2048
X_HEADS = 4
X_HEAD_DIM = 128
X_WIDTH = 512
D_FF = 5632
NEG = -1e30
LOG2E = 1.4426950408889634

GL_PAD = 512
OFF_ML, OFF_U, OFF_GL, OFF_CMP = 0, 4096, 5120, 5632
IN_F32 = 6144
IN_KV4 = 4 * KV_WIDTH
IN_PAD = IN_F32 + NSA_WIDTH + IN_KV4
IN_TN = 512

VMEM_LIMIT_V7X = 56 * 1024 * 1024
LANES = 128


def _cparams(sem, flags=None):
    return pltpu.CompilerParams(dimension_semantics=sem, vmem_limit_bytes=VMEM_LIMIT_V7X, flags=flags)


def _rms(x, g):
    ms = jnp.mean(x * x, axis=-1, keepdims=True)
    return x * lax.rsqrt(ms + 1e-6) * g


def _sigmoid(x):
    return 1.0 / (1.0 + jnp.exp(-x))


def _dot(a, b):
    return jnp.dot(a, b, preferred_element_type=F32)


def _dot_nt(a, b):
    return lax.dot_general(a, b, (((1,), (1,)), ((), ())), preferred_element_type=F32)


def _norm_matmul_kernel(x_ref, g_ref, w_ref, o_ref, h_ref):
    @pl.when(pl.program_id(1) == 0)
    def _():
        h_ref[...] = _rms(x_ref[...], g_ref[...]).astype(BF16)

    o_ref[...] = _dot(h_ref[...], w_ref[...]).astype(o_ref.dtype)


def norm_matmul(x, g, w, l, *, tm, tn, out_dtype=F32):
    m, k = x.shape
    n = w.shape[2]
    return pl.pallas_call(
        _norm_matmul_kernel,
        grid=(m // tm, n // tn),
        in_specs=[
            pl.BlockSpec((tm, k), lambda i, j: (i, 0)),
            pl.BlockSpec((1, k), lambda i, j: (0, 0)),
            pl.BlockSpec((None, k, tn), lambda i, j: (l, 0, j)),
        ],
        out_specs=pl.BlockSpec((tm, tn), lambda i, j: (i, j)),
        out_shape=jax.ShapeDtypeStruct((m, n), out_dtype),
        scratch_shapes=[pltpu.VMEM((tm, k), BF16)],
        compiler_params=_cparams(("parallel", "arbitrary")),
        name="norm_matmul",
    )(x, g.reshape(1, k), w)


Q_SCALE = (HEAD_DIM ** -0.5) * LOG2E


def _in_proj_kernel(x_ref, g_ref, w_ref, of_ref, oq_ref, okv_ref, h_ref, *, nf, nq):
    j = pl.program_id(1)

    @pl.when(j == 0)
    def _():
        h_ref[...] = _rms(x_ref[...], g_ref[...]).astype(BF16)

    @pl.when(j < nf)
    def _():
        of_ref[...] = _dot(h_ref[...], w_ref[...])

    @pl.when((j >= nf) & (j < nf + nq))
    def _():
        oq_ref[...] = (_dot(h_ref[...], w_ref[...]) * Q_SCALE).astype(BF16)

    @pl.when(j >= nf + nq)
    def _():
        okv_ref[...] = _dot(h_ref[...], w_ref[...]).astype(BF16)


def in_proj(x, g, w, l, *, tm):
    m, k = x.shape
    tn = IN_TN
    nf, nq, nkv = IN_F32 // tn, NSA_WIDTH // tn, IN_KV4 // tn
    return pl.pallas_call(
        functools.partial(_in_proj_kernel, nf=nf, nq=nq),
        grid=(m // tm, nf + nq + nkv),
        in_specs=[
            pl.BlockSpec((tm, k), lambda i, j: (i, 0)),
            pl.BlockSpec((1, k), lambda i, j: (0, 0)),
            pl.BlockSpec((None, k, tn), lambda i, j: (l, 0, j)),
        ],
        out_specs=[
            pl.BlockSpec((tm, tn), lambda i, j: (i, jnp.minimum(j, nf - 1))),
            pl.BlockSpec((tm, tn), lambda i, j: (i, jnp.clip(j - nf, 0, nq - 1))),
            pl.BlockSpec((tm, tn), lambda i, j: (i, jnp.clip(j - nf - nq, 0, nkv - 1))),
        ],
        out_shape=[
            jax.ShapeDtypeStruct((m, IN_F32), F32),
            jax.ShapeDtypeStruct((m, NSA_WIDTH), BF16),
            jax.ShapeDtypeStruct((m, IN_KV4), BF16),
        ],
        scratch_shapes=[pltpu.VMEM((tm, k), BF16)],
        compiler_params=_cparams(("parallel", "arbitrary")),
        name="in_proj",
    )(x, g.reshape(1, k), w)


def _bucket(d):
    n = jnp.maximum(d, 0)
    exact = REL_BUCKETS // 2
    nf = jnp.maximum(n, 1).astype(F32)
    large = exact + (jnp.log(nf / exact) / math.log(REL_MAX_DIST / exact)
                     * (REL_BUCKETS - exact)).astype(jnp.int32)
    return jnp.where(n < exact, n, jnp.minimum(large, REL_BUCKETS - 1))


def _lookup(tab_ref, h, bk):
    out = jnp.full(bk.shape, tab_ref[h, 0], F32)
    for b in range(1, REL_BUCKETS):
        out = jnp.where(bk == b, tab_ref[h, b], out)
    return out


def _bias_tables_kernel(tab_ref, ts_ref, wb_ref, bc_ref, *, ls, off, lw, ncw, cmp_shift):
    h = pl.program_id(0)
    r = lax.broadcasted_iota(jnp.int32, (LANES, Q_BLK), 0)
    qi = lax.broadcasted_iota(jnp.int32, (LANES, Q_BLK), 1)

    def chunked(ref, n_rows, dist, valid):
        def body(ch, carry):
            d = dist(ch * LANES + r)
            val = _lookup(tab_ref, h, _bucket(d))
            ref[0, pl.ds(pl.multiple_of(ch * LANES, LANES), LANES), :] = jnp.where(valid(d), val * LOG2E, NEG)
            return carry
        lax.fori_loop(0, n_rows // LANES, body, 0)

    chunked(ts_ref, ls, lambda c: qi - c + off, lambda d: d >= 0)
    chunked(wb_ref, lw, lambda c: qi - c + WIN, lambda d: (d >= 0) & (d < WIN))
    chunked(bc_ref, ncw, lambda m: qi - CMP_STRIDE * m + cmp_shift, lambda d: d >= 0)


def _table_sizes(s, tk):
    off = REL_MAX_DIST + tk
    ls = off + 3 * tk
    lw = 2 * WIN + Q_BLK
    return off, ls, lw


def bias_tables(rel_bias, *, s, tk):
    off, ls, lw = _table_sizes(s, tk)
    ncw = s // CMP_STRIDE
    cmp_shift = CMP_STRIDE * (ncw - 8) - (CMP_LEN - 1)
    kern = functools.partial(_bias_tables_kernel, ls=ls, off=off, lw=lw, ncw=ncw, cmp_shift=cmp_shift)
    return pl.pallas_call(
        kern,
        grid=(NSA_HEADS,),
        in_specs=[pl.BlockSpec(memory_space=pltpu.SMEM)],
        out_specs=[
            pl.BlockSpec((1, ls, Q_BLK), lambda h: (h, 0, 0)),
            pl.BlockSpec((1, lw, Q_BLK), lambda h: (h, 0, 0)),
            pl.BlockSpec((1, ncw, Q_BLK), lambda h: (h, 0, 0)),
        ],
        out_shape=[
            jax.ShapeDtypeStruct((NSA_HEADS, ls, Q_BLK), F32),
            jax.ShapeDtypeStruct((NSA_HEADS, lw, Q_BLK), F32),
            jax.ShapeDtypeStruct((NSA_HEADS, ncw, Q_BLK), F32),
        ],
        compiler_params=_cparams(("arbitrary",)),
        name="bias_tables",
    )(rel_bias)


def _gelu_tanh(x):
    return 0.5 * x * (1.0 + jnp.tanh(math.sqrt(2.0 / math.pi) * (x + 0.044715 * x * x * x)))


def _compress_kernel(c_ref, pe_ref, w1_ref, w2_ref, o_ref, scr_ref, *, nc):
    c = c_ref[0, 0]
    ca = (c + pe_ref[0, 0]).astype(BF16)
    cb = (c + pe_ref[0, 1]).astype(BF16)
    h1 = _dot(ca, w1_ref[0, 0])
    h2 = _dot(cb, w1_ref[0, 1])
    scr_ref[pl.ds(0, nc), :] = h2
    scr_ref[pl.ds(nc, 8), :] = jnp.zeros((8, CMP_HID), F32)
    hid = h1 + scr_ref[pl.ds(1, nc), :]
    out = _dot(_gelu_tanh(hid).astype(BF16), w2_ref[0])
    row = lax.broadcasted_iota(jnp.int32, out.shape, 0)
    o_ref[0, 0] = jnp.where(row < nc - 1, out, 0.0)


def compress(c, pe, w1, w2, l, *, s):
    nc = s // CMP_STRIDE
    half = CMP_STRIDE * HEAD_DIM
    return pl.pallas_call(
        functools.partial(_compress_kernel, nc=nc),
        grid=(2, NSA_GROUPS),
        in_specs=[
            pl.BlockSpec((1, 1, nc, half), lambda a, g: (a, g, 0, 0)),
            pl.BlockSpec((None, 1, 2, 1, half), lambda a, g: (l, a, 0, 0, 0)),
            pl.BlockSpec((None, 1, 2, half, CMP_HID), lambda a, g: (l, a, 0, 0, 0)),
            pl.BlockSpec((None, 1, CMP_HID, HEAD_DIM), lambda a, g: (l, a, 0, 0)),
        ],
        out_specs=pl.BlockSpec((1, 1, nc, HEAD_DIM), lambda a, g: (a, g, 0, 0)),
        out_shape=jax.ShapeDtypeStruct((2, NSA_GROUPS, nc, HEAD_DIM), F32),
        scratch_shapes=[pltpu.VMEM((nc + 8, CMP_HID), F32)],
        compiler_params=_cparams(("arbitrary", "arbitrary")),
        name="compress_kv",
    )(c, pe, w1, w2)


V_ROWS = HEAD_DIM + 16


def _lanes4(fn):
    return jnp.concatenate([fn(hh) for hh in range(NSA_HPG)], axis=-1)


def _nsa_kernel(q_ref, kvc_ref, ks_ref, vs_ref, kw_ref, vw_ref, gl_ref, ts_ref, wb_ref, bc_ref,
                ov_ref, o_ref, mask_ref, sa_ref, sb_ref, sw_ref, *, tk, ncw, nsel, off, nb):
    rows1 = NSA_HPG * Q_BLK
    rows = nb * rows1
    ib = [nb * pl.program_id(1) + b for b in range(nb)]
    q0b = [i * Q_BLK for i in ib]

    def blk(b):
        return slice(b * rows1, (b + 1) * rows1)

    def per_block(fn):
        return jnp.concatenate([fn(b) for b in range(nb)], axis=-1)

    def table_rows(ref, row0, n):
        return per_block(lambda b: _lanes4(lambda hh: ref[hh, pl.ds(row0(b), n), :]))

    qt = per_block(lambda b: q_ref[0, b])
    qa = jnp.concatenate([qt, jnp.zeros_like(qt)], axis=0)

    wk = WIN + Q_BLK
    startb = [pl.multiple_of(jnp.maximum(q0 - WIN, 0), Q_BLK) for q0 in q0b]
    for b in range(nb):
        sw_ref[:, blk(b)] = _dot(kw_ref[pl.ds(startb[b], wk), :], qa[:, blk(b)])

    tc = min(tk, ncw)
    neg_row = jnp.where(lax.broadcasted_iota(jnp.int32, (AUG_ROWS, rows), 0) == 0, NEG, 0.0).astype(BF16)
    qa_c = jnp.concatenate([qt, neg_row, jnp.zeros((LANES - HEAD_DIM - AUG_ROWS, rows), BF16)], axis=0)
    ones_rows = (lax.broadcasted_iota(jnp.int32, (V_ROWS - HEAD_DIM, tc), 0) == 0).astype(BF16)

    def cmp_rows(b, t):
        return pl.ds(pl.multiple_of(8 * ib[b] + t * tc, 8), tc)

    def cmp_scores(buf, t):
        for b in range(nb):
            buf[pl.ds(0, tc), blk(b)] = _dot(kvc_ref[0, 0, cmp_rows(b, t), :].astype(BF16), qa_c[:, blk(b)])

    def cmp_tile(buf, t, carry):
        m, acc, impa = carry
        s = buf[pl.ds(0, tc), :] + table_rows(bc_ref, lambda b: t * tc, tc)
        m_new = jnp.maximum(m, jnp.max(s, axis=0, keepdims=True))
        alpha = jnp.exp2(m - m_new)
        p = jnp.exp2(s - m_new).astype(BF16)

        def pv(b):
            vc = kvc_ref[0, 1, cmp_rows(b, t), :]
            vct = jnp.concatenate([vc.T[:HEAD_DIM, :].astype(BF16), ones_rows], axis=0)
            return _dot(vct, p[:, blk(b)])

        return (m_new, alpha * acc + per_block(pv),
                alpha * impa + _dot(ov_ref[:, t * tc:(t + 1) * tc], p))

    cmp_bufs = (sa_ref, sb_ref)
    n_ct = ncw // tc
    cmp_scores(cmp_bufs[0], 0)

    carry_c = (jnp.full((1, rows), NEG, F32), jnp.zeros((V_ROWS, rows), F32), jnp.zeros((nsel, rows), F32))
    for t in range(n_ct):
        if t + 1 < n_ct:
            cmp_scores(cmp_bufs[(t + 1) % 2], t + 1)
        carry_c = cmp_tile(cmp_bufs[t % 2], t, carry_c)
    m_c, acc_c, imp_c = carry_c
    inv = jnp.where(m_c > 0.5 * NEG, 1.0 / acc_c[HEAD_DIM:HEAD_DIM + 1, :], 0.0)
    o_c = acc_c[:HEAD_DIM, :] * inv
    imp4 = imp_c * inv

    def head_sum(b):
        c = [imp4[:, b * rows1 + hh * Q_BLK:b * rows1 + (hh + 1) * Q_BLK] for hh in range(NSA_HPG)]
        return (c[0] + c[1]) + (c[2] + c[3])
    imp = per_block(head_sum)

    s_w = sw_ref[...] + table_rows(wb_ref, lambda b: pl.multiple_of(WIN - (q0b[b] - startb[b]), Q_BLK), wk)
    m_w = jnp.max(s_w, axis=0, keepdims=True)
    e_w = jnp.exp2(s_w - m_w).astype(BF16)
    oa_w = per_block(lambda b: _dot(vw_ref[0, :, pl.ds(startb[b], wk)], e_w[:, blk(b)]))
    o_w = oa_w[:HEAD_DIM, :] * (1.0 / oa_w[HEAD_DIM:HEAD_DIM + 1, :])

    nq = nb * Q_BLK
    jp = lax.broadcasted_iota(jnp.int32, (nsel, nq), 0)
    lane = lax.broadcasted_iota(jnp.int32, (nsel, nq), 1)
    i_lane = nb * pl.program_id(1) + (lane >> 7)
    jabs = jp + (2 * i_lane - (nsel - 2))
    cur = 2 * i_lane + ((lane & (Q_BLK - 1)) >= SEL_BLK).astype(jnp.int32)
    val = jnp.where(jabs == cur - 1, FORCE_SCORE, imp)
    val = jnp.where(jabs == cur, 2.0 * FORCE_SCORE, val)
    val = jnp.where(jabs == 0, 3.0 * FORCE_SCORE, val)
    val = jnp.where(jabs > cur, -1.0, val)
    val = jnp.where(jabs < 0, -3.0, val)

    work = val
    for _ in range(SEL_TOPN):
        work = jnp.where(work == jnp.max(work, axis=0, keepdims=True), -1e38, work)
    fast = jnp.where((work < -1e37) & (val >= 0.0), 1.0, 0.0)
    tie_seen = jnp.max(jnp.sum(fast, axis=0, keepdims=True)) > SEL_TOPN + 0.5

    def exact_topn():
        wk_, chosen = val, jnp.zeros((nsel, nq), F32)
        jpf = jp.astype(F32)
        for _ in range(SEL_TOPN):
            mx = jnp.max(wk_, axis=0, keepdims=True)
            jmin = jnp.min(jnp.where(wk_ == mx, jpf, 2.0 * nsel), axis=0, keepdims=True)
            hit = jpf == jmin
            chosen = jnp.where(hit, 1.0, chosen)
            wk_ = jnp.where(hit, -1e38, wk_)
        return jnp.where(val >= 0.0, chosen, 0.0)

    sel = lax.cond(tie_seen, exact_topn, lambda: fast)
    for b in range(nb):
        shift = (2 * ib[b] + 2) & (nsel - 1)
        sel_abs = pltpu.roll(sel[:, b * Q_BLK:(b + 1) * Q_BLK], shift, 0)
        mask_abs = (sel_abs - 1.0) * (-NEG)
        mask_ref[:, blk(b)] = jnp.concatenate([mask_abs] * NSA_HPG, axis=-1)

    nblk = tk // SEL_BLK
    far_bias = table_rows(ts_ref, lambda b: 0, 1)
    far_hi = far_bias.astype(BF16).astype(F32)
    far_lo = (far_bias - far_hi).astype(BF16).astype(F32)
    far_lo2 = (far_bias - far_hi - far_lo).astype(BF16).astype(F32)
    far_rows = jnp.concatenate([far_hi, far_lo, far_lo2, jnp.zeros((8 - FAR_TERMS, rows), F32)], axis=0)
    n_pad = AUG_ROWS - nblk - 8
    aug_pad = [jnp.zeros((n_pad, rows), F32)] if n_pad else []
    rhs_tail = jnp.zeros((LANES - HEAD_DIM - AUG_ROWS, rows), BF16)
    n_tiles = (q0b[-1] + Q_BLK + tk - 1) // tk
    n_pairs = (n_tiles + 1) // 2
    n_far_pairs = (jnp.maximum(q0b[0] - off + tk, 0) // tk) // (2 * PAIRS_PER_ITER) * PAIRS_PER_ITER
    last_tile = ks_ref.shape[0] // tk - 1
    a_lane = lax.broadcasted_iota(jnp.int32, (tk, LANES), 1) - HEAD_DIM
    a_blk = lax.broadcasted_iota(jnp.int32, (tk, LANES), 0) >> 6
    key_const = jnp.where((a_lane == a_blk) | ((a_lane >= nblk) & (a_lane < nblk + FAR_TERMS)), 1.0, 0.0)
    key_const = key_const.astype(BF16)
    is_key_lane = a_lane < 0

    def scores_into(buf, kt):
        kt = jnp.minimum(kt, last_tile)
        k0 = pl.multiple_of(kt * tk, tk)
        slab = mask_ref[pl.ds(pl.multiple_of(kt * nblk, nblk), nblk), :]
        far = jnp.where(kt < 2 * n_far_pairs, far_rows, 0.0)
        aug = jnp.concatenate([slab, far] + aug_pad, axis=0).astype(BF16)
        rhs = jnp.concatenate([qt, aug, rhs_tail], axis=0)
        keys = jnp.where(is_key_lane, ks_ref[pl.ds(k0, tk), :], key_const)
        buf[...] = _dot(keys, rhs)

    def sel_tile(buf, kt, carry, near):
        m, acc = carry
        k0 = pl.multiple_of(kt * tk, tk)
        s = buf[...]
        vt = vs_ref[0, :, pl.ds(k0, tk)]
        if near:
            s = s + table_rows(
                ts_ref, lambda b: pl.multiple_of(jnp.maximum(off - (q0b[b] - k0), 0), Q_BLK), tk)
        m_new = jnp.maximum(m, jnp.max(s, axis=0, keepdims=True))
        p = jnp.exp2(s - m_new)
        acc = jnp.exp2(m - m_new) * acc + _dot(vt, p.astype(BF16))
        return m_new, acc

    def sel_pairs(it, carry, near, pairs, base):
        for j in range(pairs):
            kt = 2 * (base + pairs * it + j)
            scores_into(sb_ref, kt + 1)
            carry = sel_tile(sa_ref, kt, carry, near)
            scores_into(sa_ref, kt + 2)
            carry = sel_tile(sb_ref, kt + 1, carry, near)
        return carry

    scores_into(sa_ref, 0)
    carry = (jnp.full((1, rows), NEG, F32), jnp.zeros((V_ROWS, rows), F32))
    n_near_long = (n_pairs - n_far_pairs) // PAIRS_PER_ITER
    carry = lax.fori_loop(0, n_far_pairs // PAIRS_PER_ITER,
                          functools.partial(sel_pairs, near=False, pairs=PAIRS_PER_ITER, base=0), carry)
    carry = lax.fori_loop(0, n_near_long,
                          functools.partial(sel_pairs, near=True, pairs=PAIRS_PER_ITER, base=n_far_pairs), carry)
    _, acc_s = lax.fori_loop(n_far_pairs + PAIRS_PER_ITER * n_near_long, n_pairs,
                             functools.partial(sel_pairs, near=True, pairs=1, base=0), carry)
    o_s = acc_s[:HEAD_DIM, :] * (1.0 / acc_s[HEAD_DIM:HEAD_DIM + 1, :])

    gates = _sigmoid(gl_ref[0])
    def gate_row(c):
        return per_block(lambda b: _lanes4(
            lambda hh: gates[3 * hh + c:3 * hh + c + 1, b * Q_BLK:(b + 1) * Q_BLK]))
    o = gate_row(0) * o_c + gate_row(1) * o_s + gate_row(2) * o_w
    for b in range(nb):
        ob = jnp.concatenate([o[:, b * rows1 + hh * Q_BLK:b * rows1 + (hh + 1) * Q_BLK]
                              for hh in range(NSA_HPG)], axis=0)
        o_ref[b * Q_BLK:(b + 1) * Q_BLK, :] = ob.T.astype(o_ref.dtype)


PAIRS_PER_ITER = 3
FAR_TERMS = 3
AUG_ROWS = 16


def _overlap_matrix_t(ncw, nsel):
    m = np.arange(ncw)[None, :]
    j = np.arange(nsel)[:, None]
    lo = np.maximum(m * CMP_STRIDE, j * SEL_BLK)
    hi = np.minimum(m * CMP_STRIDE + CMP_LEN, (j + 1) * SEL_BLK)
    return (np.maximum(hi - lo, 0).astype(np.float32) / CMP_LEN)


def nsa_attention(qt, kvc_pad, zkv, vst, vwt, glt, ts, wb, bc, *, s, tk, nb):
    ncw = s // CMP_STRIDE
    nsel = s // SEL_BLK
    off, ls, lw = _table_sizes(s, tk)
    rows1 = NSA_HPG * Q_BLK
    rows = nb * rows1
    ov = jnp.asarray(_overlap_matrix_t(ncw, nsel), BF16)
    kern = functools.partial(_nsa_kernel, tk=tk, ncw=ncw, nsel=nsel, off=off, nb=nb)
    per_group = dict(pipeline_mode=pl.Buffered(1))
    return pl.pallas_call(
        kern,
        grid=(NSA_GROUPS, s // (nb * Q_BLK)),
        in_specs=[
            pl.BlockSpec((1, nb, HEAD_DIM, rows1), lambda g, i: (g, i, 0, 0)),
            pl.BlockSpec((1, 2, 2 * ncw, LANES), lambda g, i: (g, 0, 0, 0), **per_group),
            pl.BlockSpec((s, LANES), lambda g, i: (0, 2 * g), **per_group),
            pl.BlockSpec((1, V_ROWS, s), lambda g, i: (g, 0, 0), **per_group),
            pl.BlockSpec((s, LANES), lambda g, i: (0, 2 * g + 1), **per_group),
            pl.BlockSpec((1, V_ROWS, s), lambda g, i: (g, 0, 0), **per_group),
            pl.BlockSpec((1, 3 * NSA_HPG, nb * Q_BLK), lambda g, i: (g, 0, i)),
            pl.BlockSpec((NSA_HPG, ls, Q_BLK), lambda g, i: (g, 0, 0), **per_group),
            pl.BlockSpec((NSA_HPG, lw, Q_BLK), lambda g, i: (g, 0, 0), **per_group),
            pl.BlockSpec((NSA_HPG, ncw, Q_BLK), lambda g, i: (g, 0, 0), **per_group),
            pl.BlockSpec((nsel, ncw), lambda g, i: (0, 0), **per_group),
        ],
        out_specs=pl.BlockSpec((nb * Q_BLK, NSA_HPG * HEAD_DIM), lambda g, i: (i, g)),
        out_shape=jax.ShapeDtypeStruct((s, NSA_WIDTH), BF16),
        scratch_shapes=[pltpu.VMEM((nsel, rows), F32), pltpu.VMEM((tk, rows), F32),
                        pltpu.VMEM((tk, rows), F32), pltpu.VMEM((WIN + Q_BLK, rows), F32)],
        compiler_params=_cparams(("arbitrary", "arbitrary")),
        name="nsa_attention",
    )(qt, kvc_pad, zkv, vst, zkv, vwt, glt, ts, wb, bc, ov)


HALO = 16


def _pool_kernel(u_ref, halo_ref, w_ref, sc_ref, o_ref, ext_ref, *, tm):
    i = pl.program_id(0)
    halo = jnp.where(i > 0, halo_ref[...], 0.0)
    ext_ref[pl.ds(0, HALO), :] = halo
    ext_ref[pl.ds(HALO, tm), :] = u_ref[...]
    t = i * tm + lax.broadcasted_iota(jnp.int32, (tm, 1), 0)
    outs = []
    for gi, w in enumerate(POOL_WINDOWS):
        cols = pl.ds(gi * POOL_GW, POOL_GW)
        x = ext_ref[pl.ds(HALO, tm), cols]
        acc = x
        for k in range(1, w):
            acc = acc + ext_ref[pl.ds(HALO - k, tm), cols]
        cnt = jnp.minimum(t + 1, w).astype(F32)
        y = acc / cnt - x
        outs.append(_dot(y.astype(BF16), w_ref[gi]))
    o_ref[...] = (jnp.concatenate(outs, axis=-1) * sc_ref[...]).astype(o_ref.dtype)


def pool_mixer(z, w_pool, pool_scale, l, *, tm):
    s = z.shape[0]
    ucol = OFF_U // POOL_WIDTH
    return pl.pallas_call(
        functools.partial(_pool_kernel, tm=tm),
        grid=(s // tm,),
        in_specs=[
            pl.BlockSpec((tm, POOL_WIDTH), lambda i: (i, ucol)),
            pl.BlockSpec((HALO, POOL_WIDTH), lambda i: (jnp.maximum(i * (tm // HALO) - 1, 0), ucol)),
            pl.BlockSpec((None, len(POOL_WINDOWS), POOL_GW, POOL_GW), lambda i: (l, 0, 0, 0)),
            pl.BlockSpec((1, POOL_WIDTH), lambda i: (0, 0)),
        ],
        out_specs=pl.BlockSpec((tm, POOL_WIDTH), lambda i: (i, 0)),
        out_shape=jax.ShapeDtypeStruct((s, POOL_WIDTH), BF16),
        scratch_shapes=[pltpu.VMEM((HALO + tm, POOL_WIDTH), F32)],
        compiler_params=_cparams(("parallel",)),
        name="pool_mixer",
    )(z, z, w_pool, pool_scale.reshape(1, POOL_WIDTH))


def _merge_kernel(a_ref, p_ref, ml_ref, x_ref, wa_ref, wp_ref, wo_ref, g_ref, o_ref):
    pa = _dot(a_ref[...], wa_ref[...])
    pp = _dot(p_ref[...], wp_ref[...])
    ml = ml_ref[...]
    y = _sigmoid(ml[:, :D_MODEL]) * pa + _sigmoid(ml[:, D_MODEL:]) * pp
    y = _dot(y.astype(BF16), wo_ref[...])
    o_ref[...] = x_ref[...] + _rms(y, g_ref[...])


def merge_mix(a, p, z, x, wa, wp, wo, g, l, *, tm):
    s = x.shape[0]
    const = lambda i: (0, 0)
    layer = lambda i: (l, 0, 0)
    return pl.pallas_call(
        _merge_kernel,
        grid=(s // tm,),
        in_specs=[
            pl.BlockSpec((tm, NSA_WIDTH), lambda i: (i, 0)),
            pl.BlockSpec((tm, POOL_WIDTH), lambda i: (i, 0)),
            pl.BlockSpec((tm, 2 * D_MODEL), lambda i: (i, OFF_ML // (2 * D_MODEL))),
            pl.BlockSpec((tm, D_MODEL), lambda i: (i, 0)),
            pl.BlockSpec((None, NSA_WIDTH, D_MODEL), layer, pipeline_mode=pl.Buffered(1)),
            pl.BlockSpec((None, POOL_WIDTH, D_MODEL), layer, pipeline_mode=pl.Buffered(1)),
            pl.BlockSpec((None, D_MODEL, D_MODEL), layer, pipeline_mode=pl.Buffered(1)),
            pl.BlockSpec((1, D_MODEL), const),
        ],
        out_specs=pl.BlockSpec((tm, D_MODEL), lambda i: (i, 0)),
        out_shape=jax.ShapeDtypeStruct((s, D_MODEL), F32),
        compiler_params=_cparams(("parallel",)),
        name="merge_mix",
    )(a, p, z, x, wa, wp, wo, g.reshape(1, D_MODEL))


def _xattn_kernel(x_ref, gpre_ref, wq_ref, kv_ref, wo_ref, gpost_ref, o_ref):
    x = x_ref[...]
    h = _rms(x, gpre_ref[...]).astype(BF16)
    q = _dot(h, wq_ref[...]).astype(BF16)
    kv = kv_ref[...]
    outs = []
    for hh in range(X_HEADS):
        lo = hh * X_HEAD_DIM
        sc = _dot_nt(q[:, lo:lo + X_HEAD_DIM], kv[:, lo:lo + X_HEAD_DIM]) * (X_HEAD_DIM ** -0.5)
        m = jnp.max(sc, axis=-1, keepdims=True)
        e = jnp.exp(sc - m)
        p = (e / jnp.sum(e, axis=-1, keepdims=True)).astype(BF16)
        outs.append(_dot(p, kv[:, X_WIDTH + lo:X_WIDTH + lo + X_HEAD_DIM]))
    o = jnp.concatenate(outs, axis=-1).astype(BF16)
    y = _dot(o, wo_ref[...])
    o_ref[...] = x + _rms(y, gpost_ref[...])


def cross_attention(x, gpre, wq, memkv, wo, gpost, l, *, tm):
    s = x.shape[0]
    const = lambda i: (0, 0)
    layer = lambda i: (l, 0, 0)
    return pl.pallas_call(
        _xattn_kernel,
        grid=(s // tm,),
        in_specs=[
            pl.BlockSpec((tm, D_MODEL), lambda i: (i, 0)),
            pl.BlockSpec((1, D_MODEL), const),
            pl.BlockSpec((None, D_MODEL, X_WIDTH), layer),
            pl.BlockSpec((N_MEM, 2 * X_WIDTH), const),
            pl.BlockSpec((None, X_WIDTH, D_MODEL), layer),
            pl.BlockSpec((1, D_MODEL), const),
        ],
        out_specs=pl.BlockSpec((tm, D_MODEL), lambda i: (i, 0)),
        out_shape=jax.ShapeDtypeStruct((s, D_MODEL), F32),
        compiler_params=_cparams(("parallel",)),
        name="cross_attention",
    )(x, gpre.reshape(1, D_MODEL), wq, memkv, wo, gpost.reshape(1, D_MODEL))


def _ffn_kernel(x_ref, gpre_ref, wg_ref, wu_ref, wd_ref, gpost_ref, o_ref, h_ref, acc_ref):
    f = pl.program_id(1)

    @pl.when(f == 0)
    def _():
        h_ref[...] = _rms(x_ref[...], gpre_ref[...]).astype(BF16)
        acc_ref[...] = jnp.zeros_like(acc_ref)

    h = h_ref[...]
    a = _dot(h, wg_ref[...])
    b = _dot(h, wu_ref[...])
    t = (a * _sigmoid(a) * b).astype(BF16)
    acc_ref[...] += _dot(t, wd_ref[...])

    @pl.when(f == pl.num_programs(1) - 1)
    def _():
        o_ref[...] = x_ref[...] + _rms(acc_ref[...], gpost_ref[...])


def ffn(x, gpre, wg, wu, wd, gpost, l, *, tm, tf):
    s = x.shape[0]
    return pl.pallas_call(
        _ffn_kernel,
        grid=(s // tm, D_FF // tf),
        in_specs=[
            pl.BlockSpec((tm, D_MODEL), lambda i, f: (i, 0)),
            pl.BlockSpec((1, D_MODEL), lambda i, f: (0, 0)),
            pl.BlockSpec((None, D_MODEL, tf), lambda i, f: (l, 0, f)),
            pl.BlockSpec((None, D_MODEL, tf), lambda i, f: (l, 0, f)),
            pl.BlockSpec((None, tf, D_MODEL), lambda i, f: (l, f, 0)),
            pl.BlockSpec((1, D_MODEL), lambda i, f: (0, 0)),
        ],
        out_specs=pl.BlockSpec((tm, D_MODEL), lambda i, f: (i, 0)),
        out_shape=jax.ShapeDtypeStruct((s, D_MODEL), F32),
        scratch_shapes=[pltpu.VMEM((tm, D_MODEL), BF16), pltpu.VMEM((tm, D_MODEL), F32)],
        compiler_params=_cparams(("parallel", "arbitrary")),
        name="ffn_swiglu",
    )(x, gpre.reshape(1, D_MODEL), wg, wu, wd, gpost.reshape(1, D_MODEL))


def _tiles(s):
    big = s >= 4096
    return dict(
        tm_in=1024 if big else 256,
        tk_sel=512 if big else 256, nb_nsa=2,
        tm_pool=512 if big else 256,
        tm_merge=256,
        tm_x=512 if big else 256,
        tm_ffn=512 if big else 256, tf=512,
    )


def _pad_w_in(w_in):
    w = w_in.astype(BF16)
    n_gl = 3 * NSA_HEADS
    o_kv = NSA_WIDTH
    o_kv4 = o_kv + 2 * KV_WIDTH
    o_gl = NSA_WIDTH + 6 * KV_WIDTH
    o_u = o_gl + n_gl
    o_ml = o_u + POOL_WIDTH
    pad = jnp.zeros(w.shape[:2] + (GL_PAD - n_gl,), BF16)
    kv4 = w[..., o_kv4:o_gl].reshape(w.shape[:2] + (4, NSA_GROUPS, HEAD_DIM))
    kv4 = jnp.swapaxes(kv4, 2, 3).reshape(w.shape[:2] + (IN_KV4,))
    return jnp.concatenate(
        [w[..., o_ml:], w[..., o_u:o_ml], w[..., o_gl:o_u], pad, w[..., o_kv:o_kv4], w[..., :o_kv], kv4],
        axis=-1)


def _forward(x, mem, rel_bias, ln_mix_pre, ln_mix_post, ln_x_pre, ln_x_post, ln_mem,
             ln_ffn_pre, ln_ffn_post, w_in, cmp_pe, cmp_w1, cmp_w2, w_pool, pool_scale,
             w_br_attn, w_br_pool, w_mix_out, w_xq, w_xkv, w_xo, w_gate, w_up, w_down):
    b, s, _ = x.shape
    assert b == 1 and s % 1024 == 0
    depth = w_in.shape[0]
    tl = _tiles(s)
    nc = s // CMP_STRIDE
    nq = s // Q_BLK
    half = CMP_STRIDE * HEAD_DIM
    ts, wb, bc = bias_tables(rel_bias, s=s, tk=tl["tk_sel"])
    v_ones = jnp.zeros((NSA_GROUPS, V_ROWS - HEAD_DIM, s), BF16).at[:, 0, :].set(1.0)
    w_in_b = _pad_w_in(w_in)
    pe_b = cmp_pe.reshape(depth, 2, 2, 1, half)
    w1_b = cmp_w1.reshape(depth, 2, 2, half, CMP_HID).astype(BF16)
    w2_b, w_pool_b = cmp_w2.astype(BF16), w_pool.astype(BF16)
    wa_b, wp_b, wo_b = w_br_attn.astype(BF16), w_br_pool.astype(BF16), w_mix_out.astype(BF16)
    wxq_b, wxkv_b, wxo_b = w_xq.astype(BF16), w_xkv.astype(BF16), w_xo.astype(BF16)
    wg_b, wu_b, wd_b = w_gate.astype(BF16), w_up.astype(BF16), w_down.astype(BF16)
    xc = x[0]
    mem2 = mem[0]
    for l in range(depth):
        z, zq, zkv = in_proj(xc, ln_mix_pre[l], w_in_b, l, tm=tl["tm_in"])
        qt = zq.reshape(nq, Q_BLK, NSA_GROUPS, NSA_HPG, HEAD_DIM).transpose(2, 0, 4, 3, 1)
        qt = qt.reshape(NSA_GROUPS, nq, HEAD_DIM, NSA_HPG * Q_BLK)
        kv = zkv.reshape(s, NSA_GROUPS, 4, HEAD_DIM)
        c = z[:, OFF_CMP:OFF_CMP + 2 * KV_WIDTH].reshape(nc, CMP_STRIDE, 2, NSA_GROUPS, HEAD_DIM)
        c = c.transpose(2, 3, 0, 1, 4).reshape(2, NSA_GROUPS, nc, half)
        kvc = compress(c, pe_b, w1_b, w2_b, l, s=s)
        kvc = jnp.transpose(kvc, (1, 0, 2, 3))
        kvc_pad = jnp.pad(kvc, ((0, 0), (0, 0), (nc - 8, 8), (0, LANES - HEAD_DIM)))
        kvc_pad = kvc_pad.at[:, 0, :nc - 8, HEAD_DIM].set(1.0)
        vst = jnp.concatenate([kv[:, :, 1].transpose(1, 2, 0), v_ones], axis=1)
        vwt = jnp.concatenate([kv[:, :, 3].transpose(1, 2, 0), v_ones], axis=1)
        glt = z[:, OFF_GL:OFF_GL + 3 * NSA_HEADS].reshape(s, NSA_GROUPS, 3 * NSA_HPG).transpose(1, 2, 0)
        a = nsa_attention(qt, kvc_pad, zkv, vst, vwt, glt, ts, wb, bc, s=s, tk=tl["tk_sel"],
                          nb=tl["nb_nsa"])
        p = pool_mixer(z, w_pool_b, pool_scale[l], l, tm=tl["tm_pool"])
        xc = merge_mix(a, p, z, xc, wa_b, wp_b, wo_b, ln_mix_post[l], l, tm=tl["tm_merge"])
        memkv = norm_matmul(mem2, ln_mem[l], wxkv_b, l, tm=N_MEM, tn=512, out_dtype=BF16)
        xc = cross_attention(xc, ln_x_pre[l], wxq_b, memkv, wxo_b, ln_x_post[l], l, tm=tl["tm_x"])
        xc = ffn(xc, ln_ffn_pre[l], wg_b, wu_b, wd_b, ln_ffn_post[l], l, tm=tl["tm_ffn"], tf=tl["tf"])
    return xc[None]


def kernel(x, mem, rel_bias, ln_mix_pre, ln_mix_post, ln_x_pre, ln_x_post, ln_mem, ln_ffn_pre,
           ln_ffn_post, w_in, cmp_pe, cmp_w1, cmp_w2, w_pool, pool_scale, w_br_attn, w_br_pool,
           w_mix_out, w_xq, w_xkv, w_xo, w_gate, w_up, w_down):
    return _forward(x, mem, rel_bias, ln_mix_pre, ln_mix_post, ln_x_pre, ln_x_post, ln_mem,
                    ln_ffn_pre, ln_ffn_post, w_in, cmp_pe, cmp_w1, cmp_w2, w_pool, pool_scale,
                    w_br_attn, w_br_pool, w_mix_out, w_xq, w_xkv, w_xo, w_gate, w_up, w_down)
```

```python
import functools
import math

import numpy as np
import jax
import jax.numpy as jnp
from jax import lax
from jax.experimental import pallas as pl
from jax.experimental.pallas import tpu as pltpu

F32 = jnp.float32
BF16 = jnp.bfloat16

D_MODEL = 2048
N_MEM = 256
NSA_HEADS = 16
NSA_GROUPS = 4
NSA_HPG = 4
HEAD_DIM = 64
NSA_WIDTH = 1024
KV_WIDTH = 256
CMP_LEN = 32
CMP_STRIDE = 16
CMP_HID = 128
SEL_BLK = 64
SEL_TOPN = 16
WIN = 512
Q_BLK = 128
FORCE_SCORE = 1e4
POOL_WINDOWS = (2, 4, 8, 16)
POOL_GW = 256
POOL_WIDTH = 1024
REL_BUCKETS = 32
REL_MAX_DIST = 2048
X_HEADS = 4
X_HEAD_DIM = 128
X_WIDTH = 512
D_FF = 5632
NEG = -1e30
LOG2E = 1.4426950408889634

GL_PAD = 512
OFF_ML, OFF_U, OFF_GL, OFF_CMP = 0, 4096, 5120, 5632
IN_F32 = 6144
IN_KV4 = 4 * KV_WIDTH
IN_PAD = IN_F32 + NSA_WIDTH + IN_KV4
IN_TN = 512

VMEM_LIMIT_V7X = 56 * 1024 * 1024
LANES = 128


def _cparams(sem):
    return pltpu.CompilerParams(dimension_semantics=sem, vmem_limit_bytes=VMEM_LIMIT_V7X)


def _rms(x, g):
    ms = jnp.mean(x * x, axis=-1, keepdims=True)
    return x * lax.rsqrt(ms + 1e-6) * g


def _sigmoid(x):
    return 1.0 / (1.0 + jnp.exp(-x))


def _dot(a, b):
    return jnp.dot(a, b, preferred_element_type=F32)


def _dot_nt(a, b):
    return lax.dot_general(a, b, (((1,), (1,)), ((), ())), preferred_element_type=F32)


def _norm_matmul_kernel(x_ref, g_ref, w_ref, o_ref, h_ref):
    @pl.when(pl.program_id(1) == 0)
    def _():
        h_ref[...] = _rms(x_ref[...], g_ref[...]).astype(BF16)

    o_ref[...] = _dot(h_ref[...], w_ref[...]).astype(o_ref.dtype)


def norm_matmul(x, g, w, l, *, tm, tn, out_dtype=F32):
    m, k = x.shape
    n = w.shape[2]
    return pl.pallas_call(
        _norm_matmul_kernel,
        grid=(m // tm, n // tn),
        in_specs=[
            pl.BlockSpec((tm, k), lambda i, j: (i, 0)),
            pl.BlockSpec((1, k), lambda i, j: (0, 0)),
            pl.BlockSpec((None, k, tn), lambda i, j: (l, 0, j)),
        ],
        out_specs=pl.BlockSpec((tm, tn), lambda i, j: (i, j)),
        out_shape=jax.ShapeDtypeStruct((m, n), out_dtype),
        scratch_shapes=[pltpu.VMEM((tm, k), BF16)],
        compiler_params=_cparams(("parallel", "arbitrary")),
        name="norm_matmul",
    )(x, g.reshape(1, k), w)


Q_SCALE = (HEAD_DIM ** -0.5) * LOG2E


def _in_proj_kernel(x_ref, g_ref, w_ref, of_ref, oq_ref, okv_ref, h_ref, *, nf, nq):
    j = pl.program_id(1)

    @pl.when(j == 0)
    def _():
        h_ref[...] = _rms(x_ref[...], g_ref[...]).astype(BF16)

    @pl.when(j < nf)
    def _():
        of_ref[...] = _dot(h_ref[...], w_ref[...])

    @pl.when((j >= nf) & (j < nf + nq))
    def _():
        oq_ref[...] = (_dot(h_ref[...], w_ref[...]) * Q_SCALE).astype(BF16)

    @pl.when(j >= nf + nq)
    def _():
        okv_ref[...] = _dot(h_ref[...], w_ref[...]).astype(BF16)


def in_proj(x, g, w, l, *, tm):
    m, k = x.shape
    tn = IN_TN
    nf, nq, nkv = IN_F32 // tn, NSA_WIDTH // tn, IN_KV4 // tn
    return pl.pallas_call(
        functools.partial(_in_proj_kernel, nf=nf, nq=nq),
        grid=(m // tm, nf + nq + nkv),
        in_specs=[
            pl.BlockSpec((tm, k), lambda i, j: (i, 0)),
            pl.BlockSpec((1, k), lambda i, j: (0, 0)),
            pl.BlockSpec((None, k, tn), lambda i, j: (l, 0, j)),
        ],
        out_specs=[
            pl.BlockSpec((tm, tn), lambda i, j: (i, jnp.minimum(j, nf - 1))),
            pl.BlockSpec((tm, tn), lambda i, j: (i, jnp.clip(j - nf, 0, nq - 1))),
            pl.BlockSpec((tm, tn), lambda i, j: (i, jnp.clip(j - nf - nq, 0, nkv - 1))),
        ],
        out_shape=[
            jax.ShapeDtypeStruct((m, IN_F32), F32),
            jax.ShapeDtypeStruct((m, NSA_WIDTH), BF16),
            jax.ShapeDtypeStruct((m, IN_KV4), BF16),
        ],
        scratch_shapes=[pltpu.VMEM((tm, k), BF16)],
        compiler_params=_cparams(("parallel", "arbitrary")),
        name="in_proj",
    )(x, g.reshape(1, k), w)


def _bucket(d):
    n = jnp.maximum(d, 0)
    exact = REL_BUCKETS // 2
    nf = jnp.maximum(n, 1).astype(F32)
    large = exact + (jnp.log(nf / exact) / math.log(REL_MAX_DIST / exact)
                     * (REL_BUCKETS - exact)).astype(jnp.int32)
    return jnp.where(n < exact, n, jnp.minimum(large, REL_BUCKETS - 1))


def _lookup(tab_ref, h, bk):
    out = jnp.full(bk.shape, tab_ref[h, 0], F32)
    for b in range(1, REL_BUCKETS):
        out = jnp.where(bk == b, tab_ref[h, b], out)
    return out


def _bias_tables_kernel(tab_ref, ts_ref, wb_ref, bc_ref, *, ls, off, lw, ncw, cmp_shift):
    h = pl.program_id(0)
    r = lax.broadcasted_iota(jnp.int32, (LANES, Q_BLK), 0)
    qi = lax.broadcasted_iota(jnp.int32, (LANES, Q_BLK), 1)

    def chunked(ref, n_rows, dist, valid):
        def body(ch, carry):
            d = dist(ch * LANES + r)
            val = _lookup(tab_ref, h, _bucket(d))
            ref[0, pl.ds(pl.multiple_of(ch * LANES, LANES), LANES), :] = jnp.where(valid(d), val * LOG2E, NEG)
            return carry
        lax.fori_loop(0, n_rows // LANES, body, 0)

    chunked(ts_ref, ls, lambda c: qi - c + off, lambda d: d >= 0)
    chunked(wb_ref, lw, lambda c: qi - c + WIN, lambda d: (d >= 0) & (d < WIN))
    chunked(bc_ref, ncw, lambda m: qi - CMP_STRIDE * m + cmp_shift, lambda d: d >= 0)


def _table_sizes(s, tk):
    off = REL_MAX_DIST + tk
    ls = off + 3 * tk
    lw = 2 * WIN + Q_BLK
    return off, ls, lw


def bias_tables(rel_bias, *, s, tk):
    off, ls, lw = _table_sizes(s, tk)
    ncw = s // CMP_STRIDE
    cmp_shift = CMP_STRIDE * (ncw - 8) - (CMP_LEN - 1)
    kern = functools.partial(_bias_tables_kernel, ls=ls, off=off, lw=lw, ncw=ncw, cmp_shift=cmp_shift)
    return pl.pallas_call(
        kern,
        grid=(NSA_HEADS,),
        in_specs=[pl.BlockSpec(memory_space=pltpu.SMEM)],
        out_specs=[
            pl.BlockSpec((1, ls, Q_BLK), lambda h: (h, 0, 0)),
            pl.BlockSpec((1, lw, Q_BLK), lambda h: (h, 0, 0)),
            pl.BlockSpec((1, ncw, Q_BLK), lambda h: (h, 0, 0)),
        ],
        out_shape=[
            jax.ShapeDtypeStruct((NSA_HEADS, ls, Q_BLK), F32),
            jax.ShapeDtypeStruct((NSA_HEADS, lw, Q_BLK), F32),
            jax.ShapeDtypeStruct((NSA_HEADS, ncw, Q_BLK), F32),
        ],
        compiler_params=_cparams(("arbitrary",)),
        name="bias_tables",
    )(rel_bias)


def _gelu_tanh(x):
    return 0.5 * x * (1.0 + jnp.tanh(math.sqrt(2.0 / math.pi) * (x + 0.044715 * x * x * x)))


def _compress_kernel(c_ref, pe_ref, w1_ref, w2_ref, o_ref, scr_ref, *, nc):
    c = c_ref[0, 0]
    ca = (c + pe_ref[0, 0]).astype(BF16)
    cb = (c + pe_ref[0, 1]).astype(BF16)
    h1 = _dot(ca, w1_ref[0, 0])
    h2 = _dot(cb, w1_ref[0, 1])
    scr_ref[pl.ds(0, nc), :] = h2
    scr_ref[pl.ds(nc, 8), :] = jnp.zeros((8, CMP_HID), F32)
    hid = h1 + scr_ref[pl.ds(1, nc), :]
    out = _dot(_gelu_tanh(hid).astype(BF16), w2_ref[0])
    row = lax.broadcasted_iota(jnp.int32, out.shape, 0)
    o_ref[0, 0] = jnp.where(row < nc - 1, out, 0.0)


def compress(c, pe, w1, w2, l, *, s):
    nc = s // CMP_STRIDE
    half = CMP_STRIDE * HEAD_DIM
    return pl.pallas_call(
        functools.partial(_compress_kernel, nc=nc),
        grid=(2, NSA_GROUPS),
        in_specs=[
            pl.BlockSpec((1, 1, nc, half), lambda a, g: (a, g, 0, 0)),
            pl.BlockSpec((None, 1, 2, 1, half), lambda a, g: (l, a, 0, 0, 0)),
            pl.BlockSpec((None, 1, 2, half, CMP_HID), lambda a, g: (l, a, 0, 0, 0)),
            pl.BlockSpec((None, 1, CMP_HID, HEAD_DIM), lambda a, g: (l, a, 0, 0)),
        ],
        out_specs=pl.BlockSpec((1, 1, nc, HEAD_DIM), lambda a, g: (a, g, 0, 0)),
        out_shape=jax.ShapeDtypeStruct((2, NSA_GROUPS, nc, HEAD_DIM), F32),
        scratch_shapes=[pltpu.VMEM((nc + 8, CMP_HID), F32)],
        compiler_params=_cparams(("arbitrary", "arbitrary")),
        name="compress_kv",
    )(c, pe, w1, w2)


V_ROWS = HEAD_DIM + 16
PAIRS_PER_ITER = 3
FAR_TERMS = 3
AUG_ROWS = 16


def _lanes4(fn):
    return jnp.concatenate([fn(hh) for hh in range(NSA_HPG)], axis=-1)


def _nsa_kernel(q_ref, kvc_ref, ks_ref, vs_ref, kw_ref, vw_ref, gl_ref, ts_ref, wb_ref, bc_ref,
                ov_ref, o_ref, mask_ref, sa_ref, sb_ref, sw_ref, *, tk, ncw, nsel, off, nb):
    rows1 = NSA_HPG * Q_BLK
    rows = nb * rows1
    ib = [nb * pl.program_id(1) + b for b in range(nb)]
    q0b = [i * Q_BLK for i in ib]

    def blk(b):
        return slice(b * rows1, (b + 1) * rows1)

    def per_block(fn):
        return jnp.concatenate([fn(b) for b in range(nb)], axis=-1)

    def table_rows(ref, row0, n):
        return per_block(lambda b: _lanes4(lambda hh: ref[hh, pl.ds(row0(b), n), :]))

    q_t = q_ref[...].astype(F32).T.astype(BF16)
    qt = per_block(lambda b: _lanes4(
        lambda hh: q_t[hh * HEAD_DIM:(hh + 1) * HEAD_DIM, b * Q_BLK:(b + 1) * Q_BLK]))
    qa = jnp.concatenate([qt, jnp.zeros_like(qt)], axis=0)

    wk = WIN + Q_BLK
    startb = [pl.multiple_of(jnp.maximum(q0 - WIN, 0), Q_BLK) for q0 in q0b]
    for b in range(nb):
        sw_ref[:, blk(b)] = _dot(kw_ref[pl.ds(startb[b], wk), :], qa[:, blk(b)])

    tc = min(tk, ncw)
    neg_row = jnp.where(lax.broadcasted_iota(jnp.int32, (AUG_ROWS, rows), 0) == 0, NEG, 0.0).astype(BF16)
    qa_c = jnp.concatenate([qt, neg_row, jnp.zeros((LANES - HEAD_DIM - AUG_ROWS, rows), BF16)], axis=0)
    ones_rows = (lax.broadcasted_iota(jnp.int32, (V_ROWS - HEAD_DIM, tc), 0) == 0).astype(BF16)

    def cmp_rows(b, t):
        return pl.ds(pl.multiple_of(8 * ib[b] + t * tc, 8), tc)

    def cmp_scores(buf, t):
        for b in range(nb):
            buf[pl.ds(0, tc), blk(b)] = _dot(kvc_ref[0, 0, cmp_rows(b, t), :].astype(BF16), qa_c[:, blk(b)])

    def cmp_tile(buf, t, carry):
        m, acc, impa = carry
        s = buf[pl.ds(0, tc), :] + table_rows(bc_ref, lambda b: t * tc, tc)
        m_new = jnp.maximum(m, jnp.max(s, axis=0, keepdims=True))
        alpha = jnp.exp2(m - m_new)
        p = jnp.exp2(s - m_new).astype(BF16)

        def pv(b):
            vc = kvc_ref[0, 1, cmp_rows(b, t), :]
            vct = jnp.concatenate([vc.T[:HEAD_DIM, :].astype(BF16), ones_rows], axis=0)
            return _dot(vct, p[:, blk(b)])

        return (m_new, alpha * acc + per_block(pv),
                alpha * impa + _dot(ov_ref[:, t * tc:(t + 1) * tc], p))

    cmp_bufs = (sa_ref, sb_ref)
    n_ct = ncw // tc
    cmp_scores(cmp_bufs[0], 0)

    carry_c = (jnp.full((1, rows), NEG, F32), jnp.zeros((V_ROWS, rows), F32), jnp.zeros((nsel, rows), F32))
    for t in range(n_ct):
        if t + 1 < n_ct:
            cmp_scores(cmp_bufs[(t + 1) % 2], t + 1)
        carry_c = cmp_tile(cmp_bufs[t % 2], t, carry_c)
    m_c, acc_c, imp_c = carry_c
    inv = jnp.where(m_c > 0.5 * NEG, 1.0 / acc_c[HEAD_DIM:HEAD_DIM + 1, :], 0.0)
    o_c = acc_c[:HEAD_DIM, :] * inv
    imp4 = imp_c * inv

    def head_sum(b):
        c = [imp4[:, b * rows1 + hh * Q_BLK:b * rows1 + (hh + 1) * Q_BLK] for hh in range(NSA_HPG)]
        return (c[0] + c[1]) + (c[2] + c[3])
    imp = per_block(head_sum)

    s_w = sw_ref[...] + table_rows(wb_ref, lambda b: pl.multiple_of(WIN - (q0b[b] - startb[b]), Q_BLK), wk)
    m_w = jnp.max(s_w, axis=0, keepdims=True)
    e_w = jnp.exp2(s_w - m_w).astype(BF16)
    oa_w = per_block(lambda b: _dot(vw_ref[0, :, pl.ds(startb[b], wk)], e_w[:, blk(b)]))
    o_w = oa_w[:HEAD_DIM, :] * (1.0 / oa_w[HEAD_DIM:HEAD_DIM + 1, :])

    nq = nb * Q_BLK
    jp = lax.broadcasted_iota(jnp.int32, (nsel, nq), 0)
    lane = lax.broadcasted_iota(jnp.int32, (nsel, nq), 1)
    i_lane = nb * pl.program_id(1) + (lane >> 7)
    jabs = jp + (2 * i_lane - (nsel - 2))
    cur = 2 * i_lane + ((lane & (Q_BLK - 1)) >= SEL_BLK).astype(jnp.int32)
    val = jnp.where(jabs == cur - 1, FORCE_SCORE, imp)
    val = jnp.where(jabs == cur, 2.0 * FORCE_SCORE, val)
    val = jnp.where(jabs == 0, 3.0 * FORCE_SCORE, val)
    val = jnp.where(jabs > cur, -1.0, val)
    val = jnp.where(jabs < 0, -3.0, val)

    work = val
    for _ in range(SEL_TOPN):
        work = jnp.where(work == jnp.max(work, axis=0, keepdims=True), -1e38, work)
    fast = jnp.where((work < -1e37) & (val >= 0.0), 1.0, 0.0)
    tie_seen = jnp.max(jnp.sum(fast, axis=0, keepdims=True)) > SEL_TOPN + 0.5

    def exact_topn():
        wk_, chosen = val, jnp.zeros((nsel, nq), F32)
        jpf = jp.astype(F32)
        for _ in range(SEL_TOPN):
            mx = jnp.max(wk_, axis=0, keepdims=True)
            jmin = jnp.min(jnp.where(wk_ == mx, jpf, 2.0 * nsel), axis=0, keepdims=True)
            hit = jpf == jmin
            chosen = jnp.where(hit, 1.0, chosen)
            wk_ = jnp.where(hit, -1e38, wk_)
        return jnp.where(val >= 0.0, chosen, 0.0)

    sel = lax.cond(tie_seen, exact_topn, lambda: fast)
    for b in range(nb):
        shift = (2 * ib[b] + 2) & (nsel - 1)
        sel_abs = pltpu.roll(sel[:, b * Q_BLK:(b + 1) * Q_BLK], shift, 0)
        mask_abs = (sel_abs - 1.0) * (-NEG)
        mask_ref[:, blk(b)] = jnp.concatenate([mask_abs] * NSA_HPG, axis=-1)

    nblk = tk // SEL_BLK
    far_bias = table_rows(ts_ref, lambda b: 0, 1)
    far_hi = far_bias.astype(BF16).astype(F32)
    far_lo = (far_bias - far_hi).astype(BF16).astype(F32)
    far_lo2 = (far_bias - far_hi - far_lo).astype(BF16).astype(F32)
    far_rows = jnp.concatenate([far_hi, far_lo, far_lo2, jnp.zeros((8 - FAR_TERMS, rows), F32)], axis=0)
    n_pad = AUG_ROWS - nblk - 8
    aug_pad = [jnp.zeros((n_pad, rows), F32)] if n_pad else []
    rhs_tail = jnp.zeros((LANES - HEAD_DIM - AUG_ROWS, rows), BF16)
    n_tiles = (q0b[-1] + Q_BLK + tk - 1) // tk
    n_pairs = (n_tiles + 1) // 2
    n_far_pairs = (jnp.maximum(q0b[0] - off + tk, 0) // tk) // (2 * PAIRS_PER_ITER) * PAIRS_PER_ITER
    last_tile = ks_ref.shape[0] // tk - 1
    a_lane = lax.broadcasted_iota(jnp.int32, (tk, LANES), 1) - HEAD_DIM
    a_blk = lax.broadcasted_iota(jnp.int32, (tk, LANES), 0) >> 6
    key_const = jnp.where((a_lane == a_blk) | ((a_lane >= nblk) & (a_lane < nblk + FAR_TERMS)), 1.0, 0.0)
    key_const = key_const.astype(BF16)
    is_key_lane = a_lane < 0

    def scores_into(buf, kt):
        kt = jnp.minimum(kt, last_tile)
        k0 = pl.multiple_of(kt * tk, tk)
        slab = mask_ref[pl.ds(pl.multiple_of(kt * nblk, nblk), nblk), :]
        far = jnp.where(kt < 2 * n_far_pairs, far_rows, 0.0)
        aug = jnp.concatenate([slab, far] + aug_pad, axis=0).astype(BF16)
        rhs = jnp.concatenate([qt, aug, rhs_tail], axis=0)
        keys = jnp.where(is_key_lane, ks_ref[pl.ds(k0, tk), :], key_const)
        buf[...] = _dot(keys, rhs)

    def sel_tile(buf, kt, carry, near):
        m, acc = carry
        k0 = pl.multiple_of(kt * tk, tk)
        s = buf[...]
        vt = vs_ref[0, :, pl.ds(k0, tk)]
        if near:
            s = s + table_rows(
                ts_ref, lambda b: pl.multiple_of(jnp.maximum(off - (q0b[b] - k0), 0), Q_BLK), tk)
        m_new = jnp.maximum(m, jnp.max(s, axis=0, keepdims=True))
        p = jnp.exp2(s - m_new)
        acc = jnp.exp2(m - m_new) * acc + _dot(vt, p.astype(BF16))
        return m_new, acc

    def sel_pairs(it, carry, near, pairs, base):
        for j in range(pairs):
            kt = 2 * (base + pairs * it + j)
            scores_into(sb_ref, kt + 1)
            carry = sel_tile(sa_ref, kt, carry, near)
            scores_into(sa_ref, kt + 2)
            carry = sel_tile(sb_ref, kt + 1, carry, near)
        return carry

    scores_into(sa_ref, 0)
    carry = (jnp.full((1, rows), NEG, F32), jnp.zeros((V_ROWS, rows), F32))
    n_near_long = (n_pairs - n_far_pairs) // PAIRS_PER_ITER
    carry = lax.fori_loop(0, n_far_pairs // PAIRS_PER_ITER,
                          functools.partial(sel_pairs, near=False, pairs=PAIRS_PER_ITER, base=0), carry)
    carry = lax.fori_loop(0, n_near_long,
                          functools.partial(sel_pairs, near=True, pairs=PAIRS_PER_ITER, base=n_far_pairs), carry)
    _, acc_s = lax.fori_loop(n_far_pairs + PAIRS_PER_ITER * n_near_long, n_pairs,
                             functools.partial(sel_pairs, near=True, pairs=1, base=0), carry)
    o_s = acc_s[:HEAD_DIM, :] * (1.0 / acc_s[HEAD_DIM:HEAD_DIM + 1, :])

    gates = _sigmoid(gl_ref[0])
    def gate_row(c):
        return per_block(lambda b: _lanes4(
            lambda hh: gates[3 * hh + c:3 * hh + c + 1, b * Q_BLK:(b + 1) * Q_BLK]))
    o = gate_row(0) * o_c + gate_row(1) * o_s + gate_row(2) * o_w
    for b in range(nb):
        ob = jnp.concatenate([o[:, b * rows1 + hh * Q_BLK:b * rows1 + (hh + 1) * Q_BLK]
                              for hh in range(NSA_HPG)], axis=0)
        o_ref[b * Q_BLK:(b + 1) * Q_BLK, :] = ob.T.astype(o_ref.dtype)


def _overlap_matrix_t(ncw, nsel):
    m = np.arange(ncw)[None, :]
    j = np.arange(nsel)[:, None]
    lo = np.maximum(m * CMP_STRIDE, j * SEL_BLK)
    hi = np.minimum(m * CMP_STRIDE + CMP_LEN, (j + 1) * SEL_BLK)
    return (np.maximum(hi - lo, 0).astype(np.float32) / CMP_LEN)


def nsa_attention(zq, kvc_pad, zkv, vst, vwt, glt, ts, wb, bc, *, s, tk, nb):
    ncw = s // CMP_STRIDE
    nsel = s // SEL_BLK
    off, ls, lw = _table_sizes(s, tk)
    rows1 = NSA_HPG * Q_BLK
    rows = nb * rows1
    ov = jnp.asarray(_overlap_matrix_t(ncw, nsel), BF16)
    kern = functools.partial(_nsa_kernel, tk=tk, ncw=ncw, nsel=nsel, off=off, nb=nb)
    per_group = dict(pipeline_mode=pl.Buffered(1))
    return pl.pallas_call(
        kern,
        grid=(NSA_GROUPS, s // (nb * Q_BLK)),
        in_specs=[
            pl.BlockSpec((nb * Q_BLK, NSA_HPG * HEAD_DIM), lambda g, i: (i, g)),
            pl.BlockSpec((1, 2, 2 * ncw, LANES), lambda g, i: (g, 0, 0, 0), **per_group),
            pl.BlockSpec((s, LANES), lambda g, i: (0, 2 * g), **per_group),
            pl.BlockSpec((1, V_ROWS, s), lambda g, i: (g, 0, 0), **per_group),
            pl.BlockSpec((s, LANES), lambda g, i: (0, 2 * g + 1), **per_group),
            pl.BlockSpec((1, V_ROWS, s), lambda g, i: (g, 0, 0), **per_group),
            pl.BlockSpec((1, 3 * NSA_HPG, nb * Q_BLK), lambda g, i: (g, 0, i)),
            pl.BlockSpec((NSA_HPG, ls, Q_BLK), lambda g, i: (g, 0, 0), **per_group),
            pl.BlockSpec((NSA_HPG, lw, Q_BLK), lambda g, i: (g, 0, 0), **per_group),
            pl.BlockSpec((NSA_HPG, ncw, Q_BLK), lambda g, i: (g, 0, 0), **per_group),
            pl.BlockSpec((nsel, ncw), lambda g, i: (0, 0), **per_group),
        ],
        out_specs=pl.BlockSpec((nb * Q_BLK, NSA_HPG * HEAD_DIM), lambda g, i: (i, g)),
        out_shape=jax.ShapeDtypeStruct((s, NSA_WIDTH), BF16),
        scratch_shapes=[pltpu.VMEM((nsel, rows), F32), pltpu.VMEM((tk, rows), F32),
                        pltpu.VMEM((tk, rows), F32), pltpu.VMEM((WIN + Q_BLK, rows), F32)],
        compiler_params=_cparams(("arbitrary", "arbitrary")),
        name="nsa_attention",
    )(zq, kvc_pad, zkv, vst, zkv, vwt, glt, ts, wb, bc, ov)


HALO = 16


def _pool_kernel(u_ref, halo_ref, w_ref, sc_ref, o_ref, ext_ref, *, tm):
    i = pl.program_id(0)
    halo = jnp.where(i > 0, halo_ref[...], 0.0)
    ext_ref[pl.ds(0, HALO), :] = halo
    ext_ref[pl.ds(HALO, tm), :] = u_ref[...]
    t = i * tm + lax.broadcasted_iota(jnp.int32, (tm, 1), 0)
    outs = []
    for gi, w in enumerate(POOL_WINDOWS):
        cols = pl.ds(gi * POOL_GW, POOL_GW)
        x = ext_ref[pl.ds(HALO, tm), cols]
        acc = x
        for k in range(1, w):
            acc = acc + ext_ref[pl.ds(HALO - k, tm), cols]
        cnt = jnp.minimum(t + 1, w).astype(F32)
        y = acc / cnt - x
        outs.append(_dot(y.astype(BF16), w_ref[gi]))
    o_ref[...] = (jnp.concatenate(outs, axis=-1) * sc_ref[...]).astype(o_ref.dtype)


def pool_mixer(z, w_pool, pool_scale, l, *, tm):
    s = z.shape[0]
    ucol = OFF_U // POOL_WIDTH
    return pl.pallas_call(
        functools.partial(_pool_kernel, tm=tm),
        grid=(s // tm,),
        in_specs=[
            pl.BlockSpec((tm, POOL_WIDTH), lambda i: (i, ucol)),
            pl.BlockSpec((HALO, POOL_WIDTH), lambda i: (jnp.maximum(i * (tm // HALO) - 1, 0), ucol)),
            pl.BlockSpec((None, len(POOL_WINDOWS), POOL_GW, POOL_GW), lambda i: (l, 0, 0, 0)),
            pl.BlockSpec((1, POOL_WIDTH), lambda i: (0, 0)),
        ],
        out_specs=pl.BlockSpec((tm, POOL_WIDTH), lambda i: (i, 0)),
        out_shape=jax.ShapeDtypeStruct((s, POOL_WIDTH), BF16),
        scratch_shapes=[pltpu.VMEM((HALO + tm, POOL_WIDTH), F32)],
        compiler_params=_cparams(("parallel",)),
        name="pool_mixer",
    )(z, z, w_pool, pool_scale.reshape(1, POOL_WIDTH))


def _merge_kernel(a_ref, p_ref, ml_ref, x_ref, wa_ref, wp_ref, wo_ref, g_ref, o_ref):
    pa = _dot(a_ref[...], wa_ref[...])
    pp = _dot(p_ref[...], wp_ref[...])
    ml = ml_ref[...]
    y = _sigmoid(ml[:, :D_MODEL]) * pa + _sigmoid(ml[:, D_MODEL:]) * pp
    y = _dot(y.astype(BF16), wo_ref[...])
    o_ref[...] = x_ref[...] + _rms(y, g_ref[...])


def merge_mix(a, p, z, x, wa, wp, wo, g, l, *, tm):
    s = x.shape[0]
    const = lambda i: (0, 0)
    layer = lambda i: (l, 0, 0)
    return pl.pallas_call(
        _merge_kernel,
        grid=(s // tm,),
        in_specs=[
            pl.BlockSpec((tm, NSA_WIDTH), lambda i: (i, 0)),
            pl.BlockSpec((tm, POOL_WIDTH), lambda i: (i, 0)),
            pl.BlockSpec((tm, 2 * D_MODEL), lambda i: (i, OFF_ML // (2 * D_MODEL))),
            pl.BlockSpec((tm, D_MODEL), lambda i: (i, 0)),
            pl.BlockSpec((None, NSA_WIDTH, D_MODEL), layer, pipeline_mode=pl.Buffered(1)),
            pl.BlockSpec((None, POOL_WIDTH, D_MODEL), layer, pipeline_mode=pl.Buffered(1)),
            pl.BlockSpec((None, D_MODEL, D_MODEL), layer, pipeline_mode=pl.Buffered(1)),
            pl.BlockSpec((1, D_MODEL), const),
        ],
        out_specs=pl.BlockSpec((tm, D_MODEL), lambda i: (i, 0)),
        out_shape=jax.ShapeDtypeStruct((s, D_MODEL), F32),
        compiler_params=_cparams(("parallel",)),
        name="merge_mix",
    )(a, p, z, x, wa, wp, wo, g.reshape(1, D_MODEL))


def _xattn_kernel(x_ref, gpre_ref, wq_ref, kv_ref, wo_ref, gpost_ref, o_ref):
    x = x_ref[...]
    h = _rms(x, gpre_ref[...]).astype(BF16)
    q = _dot(h, wq_ref[...]).astype(BF16)
    kv = kv_ref[...]
    outs = []
    for hh in range(X_HEADS):
        lo = hh * X_HEAD_DIM
        sc = _dot_nt(q[:, lo:lo + X_HEAD_DIM], kv[:, lo:lo + X_HEAD_DIM]) * (X_HEAD_DIM ** -0.5)
        m = jnp.max(sc, axis=-1, keepdims=True)
        e = jnp.exp(sc - m)
        p = (e / jnp.sum(e, axis=-1, keepdims=True)).astype(BF16)
        outs.append(_dot(p, kv[:, X_WIDTH + lo:X_WIDTH + lo + X_HEAD_DIM]))
    o = jnp.concatenate(outs, axis=-1).astype(BF16)
    y = _dot(o, wo_ref[...])
    o_ref[...] = x + _rms(y, gpost_ref[...])


def cross_attention(x, gpre, wq, memkv, wo, gpost, l, *, tm):
    s = x.shape[0]
    const = lambda i: (0, 0)
    layer = lambda i: (l, 0, 0)
    return pl.pallas_call(
        _xattn_kernel,
        grid=(s // tm,),
        in_specs=[
            pl.BlockSpec((tm, D_MODEL), lambda i: (i, 0)),
            pl.BlockSpec((1, D_MODEL), const),
            pl.BlockSpec((None, D_MODEL, X_WIDTH), layer),
            pl.BlockSpec((N_MEM, 2 * X_WIDTH), const),
            pl.BlockSpec((None, X_WIDTH, D_MODEL), layer),
            pl.BlockSpec((1, D_MODEL), const),
        ],
        out_specs=pl.BlockSpec((tm, D_MODEL), lambda i: (i, 0)),
        out_shape=jax.ShapeDtypeStruct((s, D_MODEL), F32),
        compiler_params=_cparams(("parallel",)),
        name="cross_attention",
    )(x, gpre.reshape(1, D_MODEL), wq, memkv, wo, gpost.reshape(1, D_MODEL))


def _ffn_kernel(x_ref, gpre_ref, wg_ref, wu_ref, wd_ref, gpost_ref, o_ref, h_ref, acc_ref):
    f = pl.program_id(1)

    @pl.when(f == 0)
    def _():
        h_ref[...] = _rms(x_ref[...], gpre_ref[...]).astype(BF16)
        acc_ref[...] = jnp.zeros_like(acc_ref)

    h = h_ref[...]
    a = _dot(h, wg_ref[...])
    b = _dot(h, wu_ref[...])
    t = (a * _sigmoid(a) * b).astype(BF16)
    acc_ref[...] += _dot(t, wd_ref[...])

    @pl.when(f == pl.num_programs(1) - 1)
    def _():
        o_ref[...] = x_ref[...] + _rms(acc_ref[...], gpost_ref[...])


def ffn(x, gpre, wg, wu, wd, gpost, l, *, tm, tf):
    s = x.shape[0]
    return pl.pallas_call(
        _ffn_kernel,
        grid=(s // tm, D_FF // tf),
        in_specs=[
            pl.BlockSpec((tm, D_MODEL), lambda i, f: (i, 0)),
            pl.BlockSpec((1, D_MODEL), lambda i, f: (0, 0)),
            pl.BlockSpec((None, D_MODEL, tf), lambda i, f: (l, 0, f)),
            pl.BlockSpec((None, D_MODEL, tf), lambda i, f: (l, 0, f)),
            pl.BlockSpec((None, tf, D_MODEL), lambda i, f: (l, f, 0)),
            pl.BlockSpec((1, D_MODEL), lambda i, f: (0, 0)),
        ],
        out_specs=pl.BlockSpec((tm, D_MODEL), lambda i, f: (i, 0)),
        out_shape=jax.ShapeDtypeStruct((s, D_MODEL), F32),
        scratch_shapes=[pltpu.VMEM((tm, D_MODEL), BF16), pltpu.VMEM((tm, D_MODEL), F32)],
        compiler_params=_cparams(("parallel", "arbitrary")),
        name="ffn_swiglu",
    )(x, gpre.reshape(1, D_MODEL), wg, wu, wd, gpost.reshape(1, D_MODEL))


def _tiles(s):
    big = s >= 4096
    return dict(
        tm_in=1024 if big else 256,
        tk_sel=512 if big else 256, nb_nsa=2,
        tm_pool=512 if big else 256,
        tm_merge=256,
        tm_x=512 if big else 256,
        tm_ffn=512 if big else 256, tf=512,
    )


def _pad_w_in(w_in):
    w = w_in.astype(BF16)
    n_gl = 3 * NSA_HEADS
    o_kv = NSA_WIDTH
    o_kv4 = o_kv + 2 * KV_WIDTH
    o_gl = NSA_WIDTH + 6 * KV_WIDTH
    o_u = o_gl + n_gl
    o_ml = o_u + POOL_WIDTH
    pad = jnp.zeros(w.shape[:2] + (GL_PAD - n_gl,), BF16)
    kv4 = w[..., o_kv4:o_gl].reshape(w.shape[:2] + (4, NSA_GROUPS, HEAD_DIM))
    kv4 = jnp.swapaxes(kv4, 2, 3).reshape(w.shape[:2] + (IN_KV4,))
    out = jnp.concatenate(
        [w[..., o_ml:], w[..., o_u:o_ml], w[..., o_gl:o_u], pad, w[..., o_kv:o_kv4], w[..., :o_kv], kv4],
        axis=-1)
    assert out.shape[-1] == IN_PAD
    return out


def _forward(x, mem, rel_bias, ln_mix_pre, ln_mix_post, ln_x_pre, ln_x_post, ln_mem,
             ln_ffn_pre, ln_ffn_post, w_in, cmp_pe, cmp_w1, cmp_w2, w_pool, pool_scale,
             w_br_attn, w_br_pool, w_mix_out, w_xq, w_xkv, w_xo, w_gate, w_up, w_down):
    b, s, _ = x.shape
    assert b == 1 and s % 1024 == 0
    depth = w_in.shape[0]
    tl = _tiles(s)
    nc = s // CMP_STRIDE
    half = CMP_STRIDE * HEAD_DIM
    ts, wb, bc = bias_tables(rel_bias, s=s, tk=tl["tk_sel"])
    v_ones = jnp.zeros((NSA_GROUPS, V_ROWS - HEAD_DIM, s), BF16).at[:, 0, :].set(1.0)
    w_in_b = _pad_w_in(w_in)
    pe_b = cmp_pe.reshape(depth, 2, 2, 1, half)
    w1_b = cmp_w1.reshape(depth, 2, 2, half, CMP_HID).astype(BF16)
    w2_b, w_pool_b = cmp_w2.astype(BF16), w_pool.astype(BF16)
    wa_b, wp_b, wo_b = w_br_attn.astype(BF16), w_br_pool.astype(BF16), w_mix_out.astype(BF16)
    wxq_b, wxkv_b, wxo_b = w_xq.astype(BF16), w_xkv.astype(BF16), w_xo.astype(BF16)
    wg_b, wu_b, wd_b = w_gate.astype(BF16), w_up.astype(BF16), w_down.astype(BF16)
    xc = x[0]
    mem2 = mem[0]
    for l in range(depth):
        z, zq, zkv = in_proj(xc, ln_mix_pre[l], w_in_b, l, tm=tl["tm_in"])
        kv = zkv.reshape(s, NSA_GROUPS, 4, HEAD_DIM)
        c = z[:, OFF_CMP:OFF_CMP + 2 * KV_WIDTH].reshape(nc, CMP_STRIDE, 2, NSA_GROUPS, HEAD_DIM)
        c = c.transpose(2, 3, 0, 1, 4).reshape(2, NSA_GROUPS, nc, half)
        kvc = compress(c, pe_b, w1_b, w2_b, l, s=s)
        kvc = jnp.transpose(kvc, (1, 0, 2, 3))
        kvc_pad = jnp.pad(kvc, ((0, 0), (0, 0), (nc - 8, 8), (0, LANES - HEAD_DIM)))
        kvc_pad = kvc_pad.at[:, 0, :nc - 8, HEAD_DIM].set(1.0)
        vst = jnp.concatenate([kv[:, :, 1].transpose(1, 2, 0), v_ones], axis=1)
        vwt = jnp.concatenate([kv[:, :, 3].transpose(1, 2, 0), v_ones], axis=1)
        glt = z[:, OFF_GL:OFF_GL + 3 * NSA_HEADS].reshape(s, NSA_GROUPS, 3 * NSA_HPG).transpose(1, 2, 0)
        a = nsa_attention(zq, kvc_pad, zkv, vst, vwt, glt, ts, wb, bc, s=s, tk=tl["tk_sel"],
                          nb=tl["nb_nsa"])
        p = pool_mixer(z, w_pool_b, pool_scale[l], l, tm=tl["tm_pool"])
        xc = merge_mix(a, p, z, xc, wa_b, wp_b, wo_b, ln_mix_post[l], l, tm=tl["tm_merge"])
        memkv = norm_matmul(mem2, ln_mem[l], wxkv_b, l, tm=N_MEM, tn=512, out_dtype=BF16)
        xc = cross_attention(xc, ln_x_pre[l], wxq_b, memkv, wxo_b, ln_x_post[l], l, tm=tl["tm_x"])
        xc = ffn(xc, ln_ffn_pre[l], wg_b, wu_b, wd_b, ln_ffn_post[l], l, tm=tl["tm_ffn"], tf=tl["tf"])
    return xc[None]


def kernel(x, mem, rel_bias, ln_mix_pre, ln_mix_post, ln_x_pre, ln_x_post, ln_mem, ln_ffn_pre,
           ln_ffn_post, w_in, cmp_pe, cmp_w1, cmp_w2, w_pool, pool_scale, w_br_attn, w_br_pool,
           w_mix_out, w_xq, w_xkv, w_xo, w_gate, w_up, w_down):
    return _forward(x, mem, rel_bias, ln_mix_pre, ln_mix_post, ln_x_pre, ln_x_post, ln_mem,
                    ln_ffn_pre, ln_ffn_post, w_in, cmp_pe, cmp_w1, cmp_w2, w_pool, pool_scale,
                    w_br_attn, w_br_pool, w_mix_out, w_xq, w_xkv, w_xo, w_gate, w_up, w_down)
```

```python
import functools
import math

import numpy as np
import jax
import jax.numpy as jnp
from jax import lax
from jax.experimental import pallas as pl
from jax.experimental.pallas import tpu as pltpu

F32 = jnp.float32
BF16 = jnp.bfloat16

D_MODEL = 2048
N_MEM = 256
NSA_HEADS = 16
NSA_GROUPS = 4
NSA_HPG = 4
HEAD_DIM = 64
NSA_WIDTH = 1024
KV_WIDTH = 256
CMP_LEN = 32
CMP_STRIDE = 16
CMP_HID = 128
SEL_BLK = 64
SEL_TOPN = 16
WIN = 512
Q_BLK = 128
FORCE_SCORE = 1e4
POOL_WINDOWS = (2, 4, 8, 16)
POOL_GW = 256
POOL_WIDTH = 1024
REL_BUCKETS = 32
REL_MAX_DIST = 2048
X_HEADS = 4
X_HEAD_DIM = 128
X_WIDTH = 512
D_FF = 5632
NEG = -1e30
LOG2E = 1.4426950408889634

GL_PAD = 512
OFF_ML, OFF_U, OFF_GL, OFF_CMP = 0, 4096, 5120, 5632
IN_F32 = 6144
IN_KV4 = 4 * KV_WIDTH
IN_PAD = IN_F32 + NSA_WIDTH + IN_KV4
IN_TN = 512

VMEM_LIMIT_V7X = 56 * 1024 * 1024
LANES = 128


def _cparams(sem):
    return pltpu.CompilerParams(dimension_semantics=sem, vmem_limit_bytes=VMEM_LIMIT_V7X)


def _rms(x, g):
    ms = jnp.mean(x * x, axis=-1, keepdims=True)
    return x * lax.rsqrt(ms + 1e-6) * g


def _sigmoid(x):
    return 1.0 / (1.0 + jnp.exp(-x))


def _dot(a, b):
    return jnp.dot(a, b, preferred_element_type=F32)


def _dot_nt(a, b):
    return lax.dot_general(a, b, (((1,), (1,)), ((), ())), preferred_element_type=F32)


def _norm_matmul_kernel(x_ref, g_ref, w_ref, o_ref, h_ref):
    @pl.when(pl.program_id(1) == 0)
    def _():
        h_ref[...] = _rms(x_ref[...], g_ref[...]).astype(BF16)

    o_ref[...] = _dot(h_ref[...], w_ref[...]).astype(o_ref.dtype)


def norm_matmul(x, g, w, l, *, tm, tn, out_dtype=F32):
    m, k = x.shape
    n = w.shape[2]
    return pl.pallas_call(
        _norm_matmul_kernel,
        grid=(m // tm, n // tn),
        in_specs=[
            pl.BlockSpec((tm, k), lambda i, j: (i, 0)),
            pl.BlockSpec((1, k), lambda i, j: (0, 0)),
            pl.BlockSpec((None, k, tn), lambda i, j: (l, 0, j)),
        ],
        out_specs=pl.BlockSpec((tm, tn), lambda i, j: (i, j)),
        out_shape=jax.ShapeDtypeStruct((m, n), out_dtype),
        scratch_shapes=[pltpu.VMEM((tm, k), BF16)],
        compiler_params=_cparams(("parallel", "arbitrary")),
        name="norm_matmul",
    )(x, g.reshape(1, k), w)


Q_SCALE = (HEAD_DIM ** -0.5) * LOG2E


def _in_proj_kernel(x_ref, g_ref, w_ref, of_ref, oq_ref, okv_ref, h_ref, *, nf, nq):
    j = pl.program_id(1)

    @pl.when(j == 0)
    def _():
        h_ref[...] = _rms(x_ref[...], g_ref[...]).astype(BF16)

    @pl.when(j < nf)
    def _():
        of_ref[...] = _dot(h_ref[...], w_ref[...])

    @pl.when((j >= nf) & (j < nf + nq))
    def _():
        oq_ref[...] = (_dot(h_ref[...], w_ref[...]) * Q_SCALE).astype(BF16)

    @pl.when(j >= nf + nq)
    def _():
        okv_ref[...] = _dot(h_ref[...], w_ref[...]).astype(BF16)


def in_proj(x, g, w, l, *, tm):
    m, k = x.shape
    tn = IN_TN
    nf, nq, nkv = IN_F32 // tn, NSA_WIDTH // tn, IN_KV4 // tn
    return pl.pallas_call(
        functools.partial(_in_proj_kernel, nf=nf, nq=nq),
        grid=(m // tm, nf + nq + nkv),
        in_specs=[
            pl.BlockSpec((tm, k), lambda i, j: (i, 0)),
            pl.BlockSpec((1, k), lambda i, j: (0, 0)),
            pl.BlockSpec((None, k, tn), lambda i, j: (l, 0, j)),
        ],
        out_specs=[
            pl.BlockSpec((tm, tn), lambda i, j: (i, jnp.minimum(j, nf - 1))),
            pl.BlockSpec((tm, tn), lambda i, j: (i, jnp.clip(j - nf, 0, nq - 1))),
            pl.BlockSpec((tm, tn), lambda i, j: (i, jnp.clip(j - nf - nq, 0, nkv - 1))),
        ],
        out_shape=[
            jax.ShapeDtypeStruct((m, IN_F32), F32),
            jax.ShapeDtypeStruct((m, NSA_WIDTH), BF16),
            jax.ShapeDtypeStruct((m, IN_KV4), BF16),
        ],
        scratch_shapes=[pltpu.VMEM((tm, k), BF16)],
        compiler_params=_cparams(("parallel", "arbitrary")),
        name="in_proj",
    )(x, g.reshape(1, k), w)


def _bucket(d):
    n = jnp.maximum(d, 0)
    exact = REL_BUCKETS // 2
    nf = jnp.maximum(n, 1).astype(F32)
    large = exact + (jnp.log(nf / exact) / math.log(REL_MAX_DIST / exact)
                     * (REL_BUCKETS - exact)).astype(jnp.int32)
    return jnp.where(n < exact, n, jnp.minimum(large, REL_BUCKETS - 1))


def _lookup(tab_ref, h, bk):
    out = jnp.full(bk.shape, tab_ref[h, 0], F32)
    for b in range(1, REL_BUCKETS):
        out = jnp.where(bk == b, tab_ref[h, b], out)
    return out


def _bias_tables_kernel(tab_ref, ts_ref, wb_ref, bc_ref, *, ls, off, lw, ncw, cmp_shift):
    h = pl.program_id(0)
    r = lax.broadcasted_iota(jnp.int32, (LANES, Q_BLK), 0)
    qi = lax.broadcasted_iota(jnp.int32, (LANES, Q_BLK), 1)

    def chunked(ref, n_rows, dist, valid):
        def body(ch, carry):
            d = dist(ch * LANES + r)
            val = _lookup(tab_ref, h, _bucket(d))
            ref[0, pl.ds(pl.multiple_of(ch * LANES, LANES), LANES), :] = jnp.where(valid(d), val * LOG2E, NEG)
            return carry
        lax.fori_loop(0, n_rows // LANES, body, 0)

    chunked(ts_ref, ls, lambda c: qi - c + off, lambda d: d >= 0)
    chunked(wb_ref, lw, lambda c: qi - c + WIN, lambda d: (d >= 0) & (d < WIN))
    chunked(bc_ref, ncw, lambda m: qi - CMP_STRIDE * m + cmp_shift, lambda d: d >= 0)


def _table_sizes(s, tk):
    off = REL_MAX_DIST + tk
    ls = off + 3 * tk
    lw = 2 * WIN + Q_BLK
    return off, ls, lw


def bias_tables(rel_bias, *, s, tk):
    off, ls, lw = _table_sizes(s, tk)
    ncw = s // CMP_STRIDE
    cmp_shift = CMP_STRIDE * (ncw - 8) - (CMP_LEN - 1)
    kern = functools.partial(_bias_tables_kernel, ls=ls, off=off, lw=lw, ncw=ncw, cmp_shift=cmp_shift)
    return pl.pallas_call(
        kern,
        grid=(NSA_HEADS,),
        in_specs=[pl.BlockSpec(memory_space=pltpu.SMEM)],
        out_specs=[
            pl.BlockSpec((1, ls, Q_BLK), lambda h: (h, 0, 0)),
            pl.BlockSpec((1, lw, Q_BLK), lambda h: (h, 0, 0)),
            pl.BlockSpec((1, ncw, Q_BLK), lambda h: (h, 0, 0)),
        ],
        out_shape=[
            jax.ShapeDtypeStruct((NSA_HEADS, ls, Q_BLK), F32),
            jax.ShapeDtypeStruct((NSA_HEADS, lw, Q_BLK), F32),
            jax.ShapeDtypeStruct((NSA_HEADS, ncw, Q_BLK), F32),
        ],
        compiler_params=_cparams(("arbitrary",)),
        name="bias_tables",
    )(rel_bias)


def _gelu_tanh(x):
    return 0.5 * x * (1.0 + jnp.tanh(math.sqrt(2.0 / math.pi) * (x + 0.044715 * x * x * x)))


def _compress_kernel(c_ref, pe_ref, w1_ref, w2_ref, o_ref, scr_ref, *, nc):
    c = c_ref[0, 0]
    ca = (c + pe_ref[0, 0]).astype(BF16)
    cb = (c + pe_ref[0, 1]).astype(BF16)
    h1 = _dot(ca, w1_ref[0, 0])
    h2 = _dot(cb, w1_ref[0, 1])
    scr_ref[pl.ds(0, nc), :] = h2
    scr_ref[pl.ds(nc, 8), :] = jnp.zeros((8, CMP_HID), F32)
    hid = h1 + scr_ref[pl.ds(1, nc), :]
    out = _dot(_gelu_tanh(hid).astype(BF16), w2_ref[0])
    row = lax.broadcasted_iota(jnp.int32, out.shape, 0)
    o_ref[0, 0] = jnp.where(row < nc - 1, out, 0.0)


def compress(c, pe, w1, w2, l, *, s):
    nc = s // CMP_STRIDE
    half = CMP_STRIDE * HEAD_DIM
    return pl.pallas_call(
        functools.partial(_compress_kernel, nc=nc),
        grid=(2, NSA_GROUPS),
        in_specs=[
            pl.BlockSpec((1, 1, nc, half), lambda a, g: (a, g, 0, 0)),
            pl.BlockSpec((None, 1, 2, 1, half), lambda a, g: (l, a, 0, 0, 0)),
            pl.BlockSpec((None, 1, 2, half, CMP_HID), lambda a, g: (l, a, 0, 0, 0)),
            pl.BlockSpec((None, 1, CMP_HID, HEAD_DIM), lambda a, g: (l, a, 0, 0)),
        ],
        out_specs=pl.BlockSpec((1, 1, nc, HEAD_DIM), lambda a, g: (a, g, 0, 0)),
        out_shape=jax.ShapeDtypeStruct((2, NSA_GROUPS, nc, HEAD_DIM), F32),
        scratch_shapes=[pltpu.VMEM((nc + 8, CMP_HID), F32)],
        compiler_params=_cparams(("arbitrary", "arbitrary")),
        name="compress_kv",
    )(c, pe, w1, w2)


V_ROWS = HEAD_DIM + 16
PAIRS_PER_ITER = 3
FAR_TERMS = 3
AUG_ROWS = 16


def _lanes4(fn):
    return jnp.concatenate([fn(hh) for hh in range(NSA_HPG)], axis=-1)


def _nsa_kernel(q_ref, kvc_ref, ks_ref, vs_ref, kw_ref, vw_ref, gl_ref, ts_ref, wb_ref, bc_ref,
                ov_ref, o_ref, mask_ref, sa_ref, sb_ref, sw_ref, *, tk, ncw, nsel, off, nb):
    rows1 = NSA_HPG * Q_BLK
    rows = nb * rows1
    ib = [nb * pl.program_id(1) + b for b in range(nb)]
    q0b = [i * Q_BLK for i in ib]

    def blk(b):
        return slice(b * rows1, (b + 1) * rows1)

    def per_block(fn):
        return jnp.concatenate([fn(b) for b in range(nb)], axis=-1)

    def table_rows(ref, row0, n):
        return per_block(lambda b: _lanes4(lambda hh: ref[hh, pl.ds(row0(b), n), :]))

    q_t = q_ref[...].astype(F32).T.astype(BF16)
    qt = per_block(lambda b: _lanes4(
        lambda hh: q_t[hh * HEAD_DIM:(hh + 1) * HEAD_DIM, b * Q_BLK:(b + 1) * Q_BLK]))
    qa = jnp.concatenate([qt, jnp.zeros_like(qt)], axis=0)

    wk = WIN + Q_BLK
    startb = [pl.multiple_of(jnp.maximum(q0 - WIN, 0), Q_BLK) for q0 in q0b]
    for b in range(nb):
        sw_ref[:, blk(b)] = _dot(kw_ref[pl.ds(startb[b], wk), :], qa[:, blk(b)])

    tc = min(tk, ncw)
    neg_row = jnp.where(lax.broadcasted_iota(jnp.int32, (AUG_ROWS, rows), 0) == 0, NEG, 0.0).astype(BF16)
    qa_c = jnp.concatenate([qt, neg_row, jnp.zeros((LANES - HEAD_DIM - AUG_ROWS, rows), BF16)], axis=0)
    ones_rows = (lax.broadcasted_iota(jnp.int32, (V_ROWS - HEAD_DIM, tc), 0) == 0).astype(BF16)

    def cmp_rows(b, t):
        return pl.ds(pl.multiple_of(8 * ib[b] + t * tc, 8), tc)

    def cmp_scores(buf, t):
        for b in range(nb):
            buf[pl.ds(0, tc), blk(b)] = _dot(kvc_ref[0, 0, cmp_rows(b, t), :].astype(BF16), qa_c[:, blk(b)])

    def cmp_tile(buf, t, carry):
        m, acc, impa = carry
        s = buf[pl.ds(0, tc), :] + table_rows(bc_ref, lambda b: t * tc, tc)
        m_new = jnp.maximum(m, jnp.max(s, axis=0, keepdims=True))
        alpha = jnp.exp2(m - m_new)
        p = jnp.exp2(s - m_new).astype(BF16)

        def pv(b):
            vc = kvc_ref[0, 1, cmp_rows(b, t), :]
            vct = jnp.concatenate([vc.T[:HEAD_DIM, :].astype(BF16), ones_rows], axis=0)
            return _dot(vct, p[:, blk(b)])

        return (m_new, alpha * acc + per_block(pv),
                alpha * impa + _dot(ov_ref[:, t * tc:(t + 1) * tc], p))

    cmp_bufs = (sa_ref, sb_ref)
    n_ct = ncw // tc
    cmp_scores(cmp_bufs[0], 0)

    carry_c = (jnp.full((1, rows), NEG, F32), jnp.zeros((V_ROWS, rows), F32), jnp.zeros((nsel, rows), F32))
    for t in range(n_ct):
        if t + 1 < n_ct:
            cmp_scores(cmp_bufs[(t + 1) % 2], t + 1)
        carry_c = cmp_tile(cmp_bufs[t % 2], t, carry_c)
    m_c, acc_c, imp_c = carry_c
    inv = jnp.where(m_c > 0.5 * NEG, 1.0 / acc_c[HEAD_DIM:HEAD_DIM + 1, :], 0.0)
    o_c = acc_c[:HEAD_DIM, :] * inv
    imp4 = imp_c * inv

    def head_sum(b):
        c = [imp4[:, b * rows1 + hh * Q_BLK:b * rows1 + (hh + 1) * Q_BLK] for hh in range(NSA_HPG)]
        return (c[0] + c[1]) + (c[2] + c[3])
    imp = per_block(head_sum)

    s_w = sw_ref[...] + table_rows(wb_ref, lambda b: pl.multiple_of(WIN - (q0b[b] - startb[b]), Q_BLK), wk)
    m_w = jnp.max(s_w, axis=0, keepdims=True)
    e_w = jnp.exp2(s_w - m_w).astype(BF16)
    oa_w = per_block(lambda b: _dot(vw_ref[0, :, pl.ds(startb[b], wk)], e_w[:, blk(b)]))
    o_w = oa_w[:HEAD_DIM, :] * (1.0 / oa_w[HEAD_DIM:HEAD_DIM + 1, :])

    nq = nb * Q_BLK
    jp = lax.broadcasted_iota(jnp.int32, (nsel, nq), 0)
    lane = lax.broadcasted_iota(jnp.int32, (nsel, nq), 1)
    i_lane = nb * pl.program_id(1) + (lane >> 7)
    jabs = jp + (2 * i_lane - (nsel - 2))
    cur = 2 * i_lane + ((lane & (Q_BLK - 1)) >= SEL_BLK).astype(jnp.int32)
    val = jnp.where(jabs == cur - 1, FORCE_SCORE, imp)
    val = jnp.where(jabs == cur, 2.0 * FORCE_SCORE, val)
    val = jnp.where(jabs == 0, 3.0 * FORCE_SCORE, val)
    val = jnp.where(jabs > cur, -1.0, val)
    val = jnp.where(jabs < 0, -3.0, val)

    work = val
    for _ in range(SEL_TOPN):
        work = jnp.where(work == jnp.max(work, axis=0, keepdims=True), -1e38, work)
    fast = jnp.where((work < -1e37) & (val >= 0.0), 1.0, 0.0)
    tie_seen = jnp.max(jnp.sum(fast, axis=0, keepdims=True)) > SEL_TOPN + 0.5

    def exact_topn():
        wk_, chosen = val, jnp.zeros((nsel, nq), F32)
        jpf = jp.astype(F32)
        for _ in range(SEL_TOPN):
            mx = jnp.max(wk_, axis=0, keepdims=True)
            jmin = jnp.min(jnp.where(wk_ == mx, jpf, 2.0 * nsel), axis=0, keepdims=True)
            hit = jpf == jmin
            chosen = jnp.where(hit, 1.0, chosen)
            wk_ = jnp.where(hit, -1e38, wk_)
        return jnp.where(val >= 0.0, chosen, 0.0)

    sel = lax.cond(tie_seen, exact_topn, lambda: fast)
    for b in range(nb):
        shift = (2 * ib[b] + 2) & (nsel - 1)
        sel_abs = pltpu.roll(sel[:, b * Q_BLK:(b + 1) * Q_BLK], shift, 0)
        mask_abs = (sel_abs - 1.0) * (-NEG)
        mask_ref[:, blk(b)] = jnp.concatenate([mask_abs] * NSA_HPG, axis=-1)

    nblk = tk // SEL_BLK
    far_bias = table_rows(ts_ref, lambda b: 0, 1)
    far_hi = far_bias.astype(BF16).astype(F32)
    far_lo = (far_bias - far_hi).astype(BF16).astype(F32)
    far_lo2 = (far_bias - far_hi - far_lo).astype(BF16).astype(F32)
    far_rows = jnp.concatenate([far_hi, far_lo, far_lo2, jnp.zeros((8 - FAR_TERMS, rows), F32)], axis=0)
    n_pad = AUG_ROWS - nblk - 8
    aug_pad = [jnp.zeros((n_pad, rows), F32)] if n_pad else []
    rhs_tail = jnp.zeros((LANES - HEAD_DIM - AUG_ROWS, rows), BF16)
    n_tiles = (q0b[-1] + Q_BLK + tk - 1) // tk
    n_pairs = (n_tiles + 1) // 2
    n_far_pairs = (jnp.maximum(q0b[0] - off + tk, 0) // tk) // (2 * PAIRS_PER_ITER) * PAIRS_PER_ITER
    last_tile = ks_ref.shape[0] // tk - 1
    a_lane = lax.broadcasted_iota(jnp.int32, (tk, LANES), 1) - HEAD_DIM
    a_blk = lax.broadcasted_iota(jnp.int32, (tk, LANES), 0) >> 6
    key_const = jnp.where((a_lane == a_blk) | ((a_lane >= nblk) & (a_lane < nblk + FAR_TERMS)), 1.0, 0.0)
    key_const = key_const.astype(BF16)
    is_key_lane = a_lane < 0

    def scores_into(buf, kt):
        kt = jnp.minimum(kt, last_tile)
        k0 = pl.multiple_of(kt * tk, tk)
        slab = mask_ref[pl.ds(pl.multiple_of(kt * nblk, nblk), nblk), :]
        far = jnp.where(kt < 2 * n_far_pairs, far_rows, 0.0)
        aug = jnp.concatenate([slab, far] + aug_pad, axis=0).astype(BF16)
        rhs = jnp.concatenate([qt, aug, rhs_tail], axis=0)
        keys = jnp.where(is_key_lane, ks_ref[pl.ds(k0, tk), :], key_const)
        buf[...] = _dot(keys, rhs)

    def sel_tile(buf, kt, carry, near):
        m, acc = carry
        k0 = pl.multiple_of(kt * tk, tk)
        s = buf[...]
        vt = vs_ref[0, :, pl.ds(k0, tk)]
        if near:
            s = s + table_rows(
                ts_ref, lambda b: pl.multiple_of(jnp.maximum(off - (q0b[b] - k0), 0), Q_BLK), tk)
        m_new = jnp.maximum(m, jnp.max(s, axis=0, keepdims=True))
        p = jnp.exp2(s - m_new)
        acc = jnp.exp2(m - m_new) * acc + _dot(vt, p.astype(BF16))
        return m_new, acc

    def sel_pairs(it, carry, near, pairs, base):
        for j in range(pairs):
            kt = 2 * (base + pairs * it + j)
            scores_into(sb_ref, kt + 1)
            carry = sel_tile(sa_ref, kt, carry, near)
            scores_into(sa_ref, kt + 2)
            carry = sel_tile(sb_ref, kt + 1, carry, near)
        return carry

    scores_into(sa_ref, 0)
    carry = (jnp.full((1, rows), NEG, F32), jnp.zeros((V_ROWS, rows), F32))
    n_near_long = (n_pairs - n_far_pairs) // PAIRS_PER_ITER
    carry = lax.fori_loop(0, n_far_pairs // PAIRS_PER_ITER,
                          functools.partial(sel_pairs, near=False, pairs=PAIRS_PER_ITER, base=0), carry)
    carry = lax.fori_loop(0, n_near_long,
                          functools.partial(sel_pairs, near=True, pairs=PAIRS_PER_ITER, base=n_far_pairs), carry)
    _, acc_s = lax.fori_loop(n_far_pairs + PAIRS_PER_ITER * n_near_long, n_pairs,
                             functools.partial(sel_pairs, near=True, pairs=1, base=0), carry)
    o_s = acc_s[:HEAD_DIM, :] * (1.0 / acc_s[HEAD_DIM:HEAD_DIM + 1, :])

    gates = _sigmoid(gl_ref[0])
    def gate_row(c):
        return per_block(lambda b: _lanes4(
            lambda hh: gates[3 * hh + c:3 * hh + c + 1, b * Q_BLK:(b + 1) * Q_BLK]))
    o = gate_row(0) * o_c + gate_row(1) * o_s + gate_row(2) * o_w
    for b in range(nb):
        ob = jnp.concatenate([o[:, b * rows1 + hh * Q_BLK:b * rows1 + (hh + 1) * Q_BLK]
                              for hh in range(NSA_HPG)], axis=0)
        o_ref[b * Q_BLK:(b + 1) * Q_BLK, :] = ob.T.astype(o_ref.dtype)


def _overlap_matrix_t(ncw, nsel):
    m = np.arange(ncw)[None, :]
    j = np.arange(nsel)[:, None]
    lo = np.maximum(m * CMP_STRIDE, j * SEL_BLK)
    hi = np.minimum(m * CMP_STRIDE + CMP_LEN, (j + 1) * SEL_BLK)
    return (np.maximum(hi - lo, 0).astype(np.float32) / CMP_LEN)


def nsa_attention(zq, kvc_pad, zkv, vst, vwt, glt, ts, wb, bc, *, s, tk, nb):
    ncw = s // CMP_STRIDE
    nsel = s // SEL_BLK
    off, ls, lw = _table_sizes(s, tk)
    rows1 = NSA_HPG * Q_BLK
    rows = nb * rows1
    ov = jnp.asarray(_overlap_matrix_t(ncw, nsel), BF16)
    kern = functools.partial(_nsa_kernel, tk=tk, ncw=ncw, nsel=nsel, off=off, nb=nb)
    per_group = dict(pipeline_mode=pl.Buffered(1))
    return pl.pallas_call(
        kern,
        grid=(NSA_GROUPS, s // (nb * Q_BLK)),
        in_specs=[
            pl.BlockSpec((nb * Q_BLK, NSA_HPG * HEAD_DIM), lambda g, i: (i, g)),
            pl.BlockSpec((1, 2, 2 * ncw, LANES), lambda g, i: (g, 0, 0, 0), **per_group),
            pl.BlockSpec((s, LANES), lambda g, i: (0, 2 * g), **per_group),
            pl.BlockSpec((1, V_ROWS, s), lambda g, i: (g, 0, 0), **per_group),
            pl.BlockSpec((s, LANES), lambda g, i: (0, 2 * g + 1), **per_group),
            pl.BlockSpec((1, V_ROWS, s), lambda g, i: (g, 0, 0), **per_group),
            pl.BlockSpec((1, 3 * NSA_HPG, nb * Q_BLK), lambda g, i: (g, 0, i)),
            pl.BlockSpec((NSA_HPG, ls, Q_BLK), lambda g, i: (g, 0, 0), **per_group),
            pl.BlockSpec((NSA_HPG, lw, Q_BLK), lambda g, i: (g, 0, 0), **per_group),
            pl.BlockSpec((NSA_HPG, ncw, Q_BLK), lambda g, i: (g, 0, 0), **per_group),
            pl.BlockSpec((nsel, ncw), lambda g, i: (0, 0), **per_group),
        ],
        out_specs=pl.BlockSpec((nb * Q_BLK, NSA_HPG * HEAD_DIM), lambda g, i: (i, g)),
        out_shape=jax.ShapeDtypeStruct((s, NSA_WIDTH), BF16),
        scratch_shapes=[pltpu.VMEM((nsel, rows), F32), pltpu.VMEM((tk, rows), F32),
                        pltpu.VMEM((tk, rows), F32), pltpu.VMEM((WIN + Q_BLK, rows), F32)],
        compiler_params=_cparams(("arbitrary", "arbitrary")),
        name="nsa_attention",
    )(zq, kvc_pad, zkv, vst, zkv, vwt, glt, ts, wb, bc, ov)


HALO = 16


def _pool_tile(u_ref, halo_ref, w_ref, sc_ref, ext_ref, tm):
    i = pl.program_id(0)
    halo = jnp.where(i > 0, halo_ref[...], 0.0)
    ext_ref[pl.ds(0, HALO), :] = halo
    ext_ref[pl.ds(HALO, tm), :] = u_ref[...]
    t = i * tm + lax.broadcasted_iota(jnp.int32, (tm, 1), 0)
    outs = []
    for gi, w in enumerate(POOL_WINDOWS):
        cols = pl.ds(gi * POOL_GW, POOL_GW)
        x = ext_ref[pl.ds(HALO, tm), cols]
        acc = x
        for k in range(1, w):
            acc = acc + ext_ref[pl.ds(HALO - k, tm), cols]
        cnt = jnp.minimum(t + 1, w).astype(F32)
        y = acc / cnt - x
        outs.append(_dot(y.astype(BF16), w_ref[gi]))
    return (jnp.concatenate(outs, axis=-1) * sc_ref[...]).astype(BF16)


def _merge_kernel(a_ref, u_ref, halo_ref, wpool_ref, sc_ref, ml_ref, x_ref, wa_ref, wp_ref, wo_ref, g_ref,
                  o_ref, ext_ref, *, tm):
    pa = _dot(a_ref[...], wa_ref[...])
    pp = _dot(_pool_tile(u_ref, halo_ref, wpool_ref, sc_ref, ext_ref, tm), wp_ref[...])
    ml = ml_ref[...]
    y = _sigmoid(ml[:, :D_MODEL]) * pa + _sigmoid(ml[:, D_MODEL:]) * pp
    y = _dot(y.astype(BF16), wo_ref[...])
    o_ref[...] = x_ref[...] + _rms(y, g_ref[...])


def merge_mix(a, z, x, w_pool, pool_scale, wa, wp, wo, g, l, *, tm):
    s = x.shape[0]
    const = lambda i: (0, 0)
    layer = lambda i: (l, 0, 0)
    ucol = OFF_U // POOL_WIDTH
    return pl.pallas_call(
        functools.partial(_merge_kernel, tm=tm),
        grid=(s // tm,),
        in_specs=[
            pl.BlockSpec((tm, NSA_WIDTH), lambda i: (i, 0)),
            pl.BlockSpec((tm, POOL_WIDTH), lambda i: (i, ucol)),
            pl.BlockSpec((HALO, POOL_WIDTH), lambda i: (jnp.maximum(i * (tm // HALO) - 1, 0), ucol)),
            pl.BlockSpec((None, len(POOL_WINDOWS), POOL_GW, POOL_GW), lambda i: (l, 0, 0, 0)),
            pl.BlockSpec((1, POOL_WIDTH), const),
            pl.BlockSpec((tm, 2 * D_MODEL), lambda i: (i, OFF_ML // (2 * D_MODEL))),
            pl.BlockSpec((tm, D_MODEL), lambda i: (i, 0)),
            pl.BlockSpec((None, NSA_WIDTH, D_MODEL), layer, pipeline_mode=pl.Buffered(1)),
            pl.BlockSpec((None, POOL_WIDTH, D_MODEL), layer, pipeline_mode=pl.Buffered(1)),
            pl.BlockSpec((None, D_MODEL, D_MODEL), layer, pipeline_mode=pl.Buffered(1)),
            pl.BlockSpec((1, D_MODEL), const),
        ],
        out_specs=pl.BlockSpec((tm, D_MODEL), lambda i: (i, 0)),
        out_shape=jax.ShapeDtypeStruct((s, D_MODEL), F32),
        scratch_shapes=[pltpu.VMEM((HALO + tm, POOL_WIDTH), F32)],
        compiler_params=_cparams(("parallel",)),
        name="merge_mix",
    )(a, z, z, w_pool, pool_scale.reshape(1, POOL_WIDTH), z, x, wa, wp, wo, g.reshape(1, D_MODEL))


def _xattn_kernel(x_ref, gpre_ref, wq_ref, kv_ref, wo_ref, gpost_ref, o_ref):
    x = x_ref[...]
    h = _rms(x, gpre_ref[...]).astype(BF16)
    q = _dot(h, wq_ref[...]).astype(BF16)
    kv = kv_ref[...]
    outs = []
    for hh in range(X_HEADS):
        lo = hh * X_HEAD_DIM
        sc = _dot_nt(q[:, lo:lo + X_HEAD_DIM], kv[:, lo:lo + X_HEAD_DIM]) * (X_HEAD_DIM ** -0.5)
        m = jnp.max(sc, axis=-1, keepdims=True)
        e = jnp.exp(sc - m)
        p = (e / jnp.sum(e, axis=-1, keepdims=True)).astype(BF16)
        outs.append(_dot(p, kv[:, X_WIDTH + lo:X_WIDTH + lo + X_HEAD_DIM]))
    o = jnp.concatenate(outs, axis=-1).astype(BF16)
    y = _dot(o, wo_ref[...])
    o_ref[...] = x + _rms(y, gpost_ref[...])


def cross_attention(x, gpre, wq, memkv, wo, gpost, l, *, tm):
    s = x.shape[0]
    const = lambda i: (0, 0)
    layer = lambda i: (l, 0, 0)
    return pl.pallas_call(
        _xattn_kernel,
        grid=(s // tm,),
        in_specs=[
            pl.BlockSpec((tm, D_MODEL), lambda i: (i, 0)),
            pl.BlockSpec((1, D_MODEL), const),
            pl.BlockSpec((None, D_MODEL, X_WIDTH), layer),
            pl.BlockSpec((N_MEM, 2 * X_WIDTH), const),
            pl.BlockSpec((None, X_WIDTH, D_MODEL), layer),
            pl.BlockSpec((1, D_MODEL), const),
        ],
        out_specs=pl.BlockSpec((tm, D_MODEL), lambda i: (i, 0)),
        out_shape=jax.ShapeDtypeStruct((s, D_MODEL), F32),
        compiler_params=_cparams(("parallel",)),
        name="cross_attention",
    )(x, gpre.reshape(1, D_MODEL), wq, memkv, wo, gpost.reshape(1, D_MODEL))


def _ffn_kernel(x_ref, gpre_ref, wg_ref, wu_ref, wd_ref, gpost_ref, o_ref, h_ref, acc_ref):
    f = pl.program_id(1)

    @pl.when(f == 0)
    def _():
        h_ref[...] = _rms(x_ref[...], gpre_ref[...]).astype(BF16)
        acc_ref[...] = jnp.zeros_like(acc_ref)

    h = h_ref[...]
    a = _dot(h, wg_ref[...])
    b = _dot(h, wu_ref[...])
    t = (a * _sigmoid(a) * b).astype(BF16)
    acc_ref[...] += _dot(t, wd_ref[...])

    @pl.when(f == pl.num_programs(1) - 1)
    def _():
        o_ref[...] = x_ref[...] + _rms(acc_ref[...], gpost_ref[...])


def ffn(x, gpre, wg, wu, wd, gpost, l, *, tm, tf):
    s = x.shape[0]
    return pl.pallas_call(
        _ffn_kernel,
        grid=(s // tm, D_FF // tf),
        in_specs=[
            pl.BlockSpec((tm, D_MODEL), lambda i, f: (i, 0)),
            pl.BlockSpec((1, D_MODEL), lambda i, f: (0, 0)),
            pl.BlockSpec((None, D_MODEL, tf), lambda i, f: (l, 0, f)),
            pl.BlockSpec((None, D_MODEL, tf), lambda i, f: (l, 0, f)),
            pl.BlockSpec((None, tf, D_MODEL), lambda i, f: (l, f, 0)),
            pl.BlockSpec((1, D_MODEL), lambda i, f: (0, 0)),
        ],
        out_specs=pl.BlockSpec((tm, D_MODEL), lambda i, f: (i, 0)),
        out_shape=jax.ShapeDtypeStruct((s, D_MODEL), F32),
        scratch_shapes=[pltpu.VMEM((tm, D_MODEL), BF16), pltpu.VMEM((tm, D_MODEL), F32)],
        compiler_params=_cparams(("parallel", "arbitrary")),
        name="ffn_swiglu",
    )(x, gpre.reshape(1, D_MODEL), wg, wu, wd, gpost.reshape(1, D_MODEL))


def _tiles(s):
    big = s >= 4096
    return dict(
        tm_in=1024 if big else 256,
        tk_sel=512 if big else 256, nb_nsa=2,
        tm_merge=256,
        tm_x=512 if big else 256,
        tm_ffn=512 if big else 256, tf=512,
    )


def _pad_w_in(w_in):
    w = w_in.astype(BF16)
    n_gl = 3 * NSA_HEADS
    o_kv = NSA_WIDTH
    o_kv4 = o_kv + 2 * KV_WIDTH
    o_gl = NSA_WIDTH + 6 * KV_WIDTH
    o_u = o_gl + n_gl
    o_ml = o_u + POOL_WIDTH
    pad = jnp.zeros(w.shape[:2] + (GL_PAD - n_gl,), BF16)
    kv4 = w[..., o_kv4:o_gl].reshape(w.shape[:2] + (4, NSA_GROUPS, HEAD_DIM))
    kv4 = jnp.swapaxes(kv4, 2, 3).reshape(w.shape[:2] + (IN_KV4,))
    out = jnp.concatenate(
        [w[..., o_ml:], w[..., o_u:o_ml], w[..., o_gl:o_u], pad, w[..., o_kv:o_kv4], w[..., :o_kv], kv4],
        axis=-1)
    assert out.shape[-1] == IN_PAD
    return out


def _forward(x, mem, rel_bias, ln_mix_pre, ln_mix_post, ln_x_pre, ln_x_post, ln_mem,
             ln_ffn_pre, ln_ffn_post, w_in, cmp_pe, cmp_w1, cmp_w2, w_pool, pool_scale,
             w_br_attn, w_br_pool, w_mix_out, w_xq, w_xkv, w_xo, w_gate, w_up, w_down):
    b, s, _ = x.shape
    assert b == 1 and s % 1024 == 0
    depth = w_in.shape[0]
    tl = _tiles(s)
    nc = s // CMP_STRIDE
    half = CMP_STRIDE * HEAD_DIM
    ts, wb, bc = bias_tables(rel_bias, s=s, tk=tl["tk_sel"])
    v_ones = jnp.zeros((NSA_GROUPS, V_ROWS - HEAD_DIM, s), BF16).at[:, 0, :].set(1.0)
    w_in_b = _pad_w_in(w_in)
    pe_b = cmp_pe.reshape(depth, 2, 2, 1, half)
    w1_b = cmp_w1.reshape(depth, 2, 2, half, CMP_HID).astype(BF16)
    w2_b, w_pool_b = cmp_w2.astype(BF16), w_pool.astype(BF16)
    wa_b, wp_b, wo_b = w_br_attn.astype(BF16), w_br_pool.astype(BF16), w_mix_out.astype(BF16)
    wxq_b, wxkv_b, wxo_b = w_xq.astype(BF16), w_xkv.astype(BF16), w_xo.astype(BF16)
    wg_b, wu_b, wd_b = w_gate.astype(BF16), w_up.astype(BF16), w_down.astype(BF16)
    xc = x[0]
    mem2 = mem[0]
    for l in range(depth):
        z, zq, zkv = in_proj(xc, ln_mix_pre[l], w_in_b, l, tm=tl["tm_in"])
        kv = zkv.reshape(s, NSA_GROUPS, 4, HEAD_DIM)
        c = z[:, OFF_CMP:OFF_CMP + 2 * KV_WIDTH].reshape(nc, CMP_STRIDE, 2, NSA_GROUPS, HEAD_DIM)
        c = c.transpose(2, 3, 0, 1, 4).reshape(2, NSA_GROUPS, nc, half)
        kvc = compress(c, pe_b, w1_b, w2_b, l, s=s)
        kvc = jnp.transpose(kvc, (1, 0, 2, 3))
        kvc_pad = jnp.pad(kvc, ((0, 0), (0, 0), (nc - 8, 8), (0, LANES - HEAD_DIM)))
        kvc_pad = kvc_pad.at[:, 0, :nc - 8, HEAD_DIM].set(1.0)
        vst = jnp.concatenate([kv[:, :, 1].transpose(1, 2, 0), v_ones], axis=1)
        vwt = jnp.concatenate([kv[:, :, 3].transpose(1, 2, 0), v_ones], axis=1)
        glt = z[:, OFF_GL:OFF_GL + 3 * NSA_HEADS].reshape(s, NSA_GROUPS, 3 * NSA_HPG).transpose(1, 2, 0)
        a = nsa_attention(zq, kvc_pad, zkv, vst, vwt, glt, ts, wb, bc, s=s, tk=tl["tk_sel"],
                          nb=tl["nb_nsa"])
        xc = merge_mix(a, z, xc, w_pool_b, pool_scale[l], wa_b, wp_b, wo_b, ln_mix_post[l], l, tm=tl["tm_merge"])
        memkv = norm_matmul(mem2, ln_mem[l], wxkv_b, l, tm=N_MEM, tn=512, out_dtype=BF16)
        xc = cross_attention(xc, ln_x_pre[l], wxq_b, memkv, wxo_b, ln_x_post[l], l, tm=tl["tm_x"])
        xc = ffn(xc, ln_ffn_pre[l], wg_b, wu_b, wd_b, ln_ffn_post[l], l, tm=tl["tm_ffn"], tf=tl["tf"])
    return xc[None]


def kernel(x, mem, rel_bias, ln_mix_pre, ln_mix_post, ln_x_pre, ln_x_post, ln_mem, ln_ffn_pre,
           ln_ffn_post, w_in, cmp_pe, cmp_w1, cmp_w2, w_pool, pool_scale, w_br_attn, w_br_pool,
           w_mix_out, w_xq, w_xkv, w_xo, w_gate, w_up, w_down):
    return _forward(x, mem, rel_bias, ln_mix_pre, ln_mix_post, ln_x_pre, ln_x_post, ln_mem,
                    ln_ffn_pre, ln_ffn_post, w_in, cmp_pe, cmp_w1, cmp_w2, w_pool, pool_scale,
                    w_br_attn, w_br_pool, w_mix_out, w_xq, w_xkv, w_xo, w_gate, w_up, w_down)
```
